```python
import jax, jax.numpy as jnp
from jax import lax
import numpy as np


D_MODEL = 1024
BATCH = 2
SEQ = 16384
DEPTH = 4
DEC_BATCH = 8
DEC_SEQ = 32
PAST_LEN = 1024

CHUNK = 64
N_A = DEPTH // 2
N_B = DEPTH - N_A
N_DENSE = (DEPTH + 1) // 2
N_MOE = DEPTH // 2
CONV_WIDTH = 31
N_HEADS = 16
HEAD_DIM = D_MODEL // N_HEADS
LEFT_CHUNKS = 8
WINDOW = LEFT_CHUNKS * CHUNK
BAND_LEN = WINDOW + CHUNK
REL_CLIP = 128
N_REL = 2 * REL_CLIP + 1
D_FF = 2816
N_EXPERTS = 8
TOP_K = 2
D_EXPERT = 1408
ALPHA = (2.0 * DEPTH) ** 0.25
BETA = (8.0 * DEPTH) ** -0.25
LN_EPS = 1e-5
NEG_INF = -1e30

kernel_name = "yoco_conformer_chunkband_moe_stream_step"


def layer_norm(x, g, b):
    xf = x.astype(jnp.float32)
    mu = xf.mean(-1, keepdims=True)
    var = jnp.square(xf - mu).mean(-1, keepdims=True)
    return ((xf - mu) * lax.rsqrt(var + LN_EPS) * g + b).astype(x.dtype)


def ada_split(c, w, b, n):
    m = jax.nn.silu(c) @ w + b
    return [t[:, None, :] for t in jnp.split(m, n, axis=-1)]


def conv_module(h, conv_buf, w_pw1, b_pw1, w_dw, b_dw, g_n, b_n, w_pw2, b_pw2):
    a, g = jnp.split(h @ w_pw1 + b_pw1, 2, axis=-1)
    u = a * jax.nn.sigmoid(g)
    up = jnp.concatenate([conv_buf.astype(u.dtype), u], axis=1)
    y = lax.conv_general_dilated(up, w_dw[:, None, :], (1,), 'VALID',
                                 dimension_numbers=('NWC', 'WIO', 'NWC'),
                                 feature_group_count=D_MODEL) + b_dw
    y = jax.nn.silu(layer_norm(y, g_n, b_n))
    return y @ w_pw2 + b_pw2, up[:, -(CONV_WIDTH - 1):]


def rel_bias(table, offset, nq, nk):
    d = offset + jnp.arange(nq)[:, None] - jnp.arange(nk)[None, :]
    return table[:, jnp.clip(d, -REL_CLIP, REL_CLIP) + REL_CLIP]


def attend(q, k, v, bias, mask):
    s = jnp.einsum('bqhd,bkhd->bhqk', q, k).astype(jnp.float32) * (HEAD_DIM ** -0.5) + bias
    if mask is not None:
        s = jnp.where(mask, s, NEG_INF)
    p = jax.nn.softmax(s, axis=-1).astype(v.dtype)
    return jnp.einsum('bhqk,bkhd->bqhd', p, v)


def band_attention_prompt(q, k, v, table):
    B, T, H, Dh = q.shape
    nc = T // CHUNK
    pad = ((0, 0), (WINDOW, 0), (0, 0), (0, 0))
    kp, vp = jnp.pad(k, pad), jnp.pad(v, pad)
    qc = q.reshape(B, nc, CHUNK, H, Dh).transpose(1, 0, 2, 3, 4)
    bias = rel_bias(table, WINDOW, CHUNK, BAND_LEN)
    key_off = jnp.arange(BAND_LEN) - WINDOW

    def one_chunk(args):
        ci, qi = args
        start = ci * CHUNK
        kb = lax.dynamic_slice_in_dim(kp, start, BAND_LEN, axis=1)
        vb = lax.dynamic_slice_in_dim(vp, start, BAND_LEN, axis=1)
        mask = (start + key_off >= 0)[None, None, None, :]
        return attend(qi, kb, vb, bias, mask)

    out = lax.map(one_chunk, (jnp.arange(nc), qc))
    return out.transpose(1, 0, 2, 3, 4).reshape(B, T, H, Dh)


def band_attention_sample(q, k_cache, v_cache, k_new, v_new, table):
    kb = jnp.concatenate([k_cache.astype(k_new.dtype), k_new], axis=1)
    vb = jnp.concatenate([v_cache.astype(v_new.dtype), v_new], axis=1)
    bias = rel_bias(table, k_cache.shape[1], q.shape[1], kb.shape[1])
    return attend(q, kb, vb, bias, None)


def swiglu(h, w_gate, w_up, w_down):
    return (jax.nn.silu(h @ w_gate) * (h @ w_up)) @ w_down


def moe(h, w_router, b_router, w_gate, w_up, w_down):
    logits = (h @ w_router).astype(jnp.float32) + b_router
    probs = jax.nn.softmax(logits, axis=-1)
    top_p, top_i = lax.top_k(probs, TOP_K)
    top_p = top_p / top_p.sum(-1, keepdims=True)
    gates = jnp.sum(jax.nn.one_hot(top_i, N_EXPERTS, dtype=top_p.dtype) * top_p[..., None], axis=-2)
    gates = gates.astype(h.dtype)
    y = jnp.zeros_like(h)
    for e in range(N_EXPERTS):
        y = y + gates[..., e:e + 1] * swiglu(h, w_gate[e], w_up[e], w_down[e])
    return y


def trunk(x, c, conv_state, k_cache, v_cache, p):
    B, T, _ = x.shape
    new_conv = []
    k_sh = v_sh = None
    for l in range(DEPTH):
        sh1, sc1, g1, sh2, sc2, g2 = ada_split(c, p['w_ada'][l], p['b_ada'][l], 6)
        h = x * (1 + sc1) + sh1
        if l < N_A:
            out, buf = conv_module(h, conv_state[l], p['w_pw1'][l], p['b_pw1'][l], p['w_dw'][l],
                                   p['b_dw'][l], p['conv_ln_g'][l], p['conv_ln_b'][l],
                                   p['w_pw2'][l], p['b_pw2'][l])
            new_conv.append(buf)
        else:
            j = l - N_A
            q = (h @ p['w_q'][j]).reshape(B, T, N_HEADS, HEAD_DIM)
            if k_cache is None:
                a = band_attention_prompt(q, k_sh, v_sh, p['rel_bias'][j])
            else:
                a = band_attention_sample(q, k_cache, v_cache, k_sh, v_sh, p['rel_bias'][j])
            out = a.reshape(B, T, D_MODEL) @ p['w_o'][j]
        x = layer_norm(ALPHA * x + g1 * out, p['ln_g'][l, 0], p['ln_b'][l, 0])
        h = x * (1 + sc2) + sh2
        i = l // 2
        if l % 2 == 0:
            f = swiglu(h, p['w_ff_gate'][i], p['w_ff_up'][i], p['w_ff_down'][i])
        else:
            f = moe(h, p['w_router'][i], p['b_router'][i], p['w_e_gate'][i], p['w_e_up'][i],
                    p['w_e_down'][i])
        x = layer_norm(ALPHA * x + g2 * f, p['ln_g'][l, 1], p['ln_b'][l, 1])
        if l == N_A - 1:
            kv_shift, kv_scale = ada_split(c, p['w_ada_kv'], p['b_ada_kv'], 2)
            hkv = x * (1 + kv_scale) + kv_shift
            k_sh = (hkv @ p['w_k']).reshape(B, T, N_HEADS, HEAD_DIM)
            v_sh = (hkv @ p['w_v']).reshape(B, T, N_HEADS, HEAD_DIM)
    if k_cache is None:
        n_keep = min(WINDOW, T)
        k_out, v_out = k_sh[:, T - n_keep:], v_sh[:, T - n_keep:]
    else:
        L = k_cache.shape[1]
        k_out = jnp.concatenate([k_cache.astype(k_sh.dtype), k_sh], axis=1)[:, -L:]
        v_out = jnp.concatenate([v_cache.astype(v_sh.dtype), v_sh], axis=1)[:, -L:]
    return x, jnp.stack(new_conv), k_out, v_out


def setup_inputs(seed: int = 0) -> dict:
    key = jax.random.key(seed)
    ks = iter(jax.random.split(key, 40))

    def nrm(shape, s):
        return jax.random.normal(next(ks), shape, jnp.float32) * s

    kv_len = min(WINDOW, PAST_LEN)
    inv = D_MODEL ** -0.5
    return {
        'x_prompt': nrm((BATCH, SEQ, D_MODEL), 1.0),
        'x_sample': nrm((DEC_BATCH, DEC_SEQ, D_MODEL), 1.0),
        'c_prompt': nrm((BATCH, D_MODEL), 1.0),
        'c_sample': nrm((DEC_BATCH, D_MODEL), 1.0),
        'state_conv': nrm((N_A, DEC_BATCH, CONV_WIDTH - 1, D_MODEL), 0.5),
        'cache_k': nrm((DEC_BATCH, kv_len, N_HEADS, HEAD_DIM), 1.0),
        'cache_v': nrm((DEC_BATCH, kv_len, N_HEADS, HEAD_DIM), 0.5),
        'w_ada': nrm((DEPTH, D_MODEL, 6 * D_MODEL), inv),
        'b_ada': nrm((DEPTH, 6 * D_MODEL), 0.02),
        'ln_g': 1.0 + nrm((DEPTH, 2, D_MODEL), 0.02),
        'ln_b': nrm((DEPTH, 2, D_MODEL), 0.02),
        'w_pw1': nrm((N_A, D_MODEL, 2 * D_MODEL), inv),
        'b_pw1': nrm((N_A, 2 * D_MODEL), 0.02),
        'w_dw': nrm((N_A, CONV_WIDTH, D_MODEL), CONV_WIDTH ** -0.5),
        'b_dw': nrm((N_A, D_MODEL), 0.02),
        'conv_ln_g': 1.0 + nrm((N_A, D_MODEL), 0.02),
        'conv_ln_b': nrm((N_A, D_MODEL), 0.02),
        'w_pw2': nrm((N_A, D_MODEL, D_MODEL), inv * BETA),
        'b_pw2': nrm((N_A, D_MODEL), 0.02),
        'w_ada_kv': nrm((D_MODEL, 2 * D_MODEL), inv),
        'b_ada_kv': nrm((2 * D_MODEL,), 0.02),
        'w_k': nrm((D_MODEL, D_MODEL), inv),
        'w_v': nrm((D_MODEL, D_MODEL), inv * BETA),
        'w_q': nrm((N_B, D_MODEL, D_MODEL), inv),
        'w_o': nrm((N_B, D_MODEL, D_MODEL), inv * BETA),
        'rel_bias': nrm((N_B, N_HEADS, N_REL), 0.5),
        'w_ff_gate': nrm((N_DENSE, D_MODEL, D_FF), inv),
        'w_ff_up': nrm((N_DENSE, D_MODEL, D_FF), inv),
        'w_ff_down': nrm((N_DENSE, D_FF, D_MODEL), D_FF ** -0.5 * BETA),
        'w_router': nrm((N_MOE, D_MODEL, N_EXPERTS), inv),
        'b_router': nrm((N_MOE, N_EXPERTS), 0.01),
        'w_e_gate': nrm((N_MOE, N_EXPERTS, D_MODEL, D_EXPERT), inv),
        'w_e_up': nrm((N_MOE, N_EXPERTS, D_MODEL, D_EXPERT), inv),
        'w_e_down': nrm((N_MOE, N_EXPERTS, D_EXPERT, D_MODEL), D_EXPERT ** -0.5 * BETA),
    }


def reference(x_prompt, x_sample, c_prompt, c_sample, state_conv, cache_k, cache_v,
              w_ada, b_ada, ln_g, ln_b, w_pw1, b_pw1, w_dw, b_dw, conv_ln_g, conv_ln_b,
              w_pw2, b_pw2, w_ada_kv, b_ada_kv, w_k, w_v, w_q, w_o, rel_bias,
              w_ff_gate, w_ff_up, w_ff_down, w_router, b_router, w_e_gate, w_e_up, w_e_down):
    p = dict(w_ada=w_ada, b_ada=b_ada, ln_g=ln_g, ln_b=ln_b, w_pw1=w_pw1, b_pw1=b_pw1,
             w_dw=w_dw, b_dw=b_dw, conv_ln_g=conv_ln_g, conv_ln_b=conv_ln_b, w_pw2=w_pw2,
             b_pw2=b_pw2, w_ada_kv=w_ada_kv, b_ada_kv=b_ada_kv, w_k=w_k, w_v=w_v, w_q=w_q,
             w_o=w_o, rel_bias=rel_bias, w_ff_gate=w_ff_gate, w_ff_up=w_ff_up,
             w_ff_down=w_ff_down, w_router=w_router, b_router=b_router, w_e_gate=w_e_gate,
             w_e_up=w_e_up, w_e_down=w_e_down)
    B = x_prompt.shape[0]
    conv_zero = jnp.zeros((N_A, B, CONV_WIDTH - 1, D_MODEL), x_prompt.dtype)
    y_prompt, conv_p, k_p, v_p = trunk(x_prompt, c_prompt, conv_zero, None, None, p)
    y_sample, conv_s, k_s, v_s = trunk(x_sample, c_sample, state_conv, cache_k, cache_v, p)
    return (y_prompt, y_sample, conv_p, k_p, v_p, conv_s, k_s, v_s)
```

```python
import functools

import jax
import jax.numpy as jnp
from jax import lax
from jax.experimental import pallas as pl
from jax.experimental.pallas import tpu as pltpu

CHUNK = 64
WINDOW = 512
REL_CLIP = 128
CONV_WIDTH = 31
N_HEADS = 16
LN_EPS = 1e-5
NEG_INF = -1e30

LANES = 128
HALO = 32
MOD_ROWS = 16
VMEM_LIMIT = 56 * 1024 * 1024

F32 = jnp.float32
BF16 = jnp.bfloat16


def _layer_norm(v, g, b):
    mu = jnp.mean(v, axis=-1, keepdims=True)
    d = v - mu
    var = jnp.mean(d * d, axis=-1, keepdims=True)
    return d * lax.rsqrt(var + LN_EPS) * g + b


def _silu(v):
    return v * jax.nn.sigmoid(v)


def _const_spec(shape):
    nd = len(shape)
    return pl.BlockSpec(shape, lambda *_: (0,) * nd, pipeline_mode=pl.Buffered(1))


def _params(n_grid):
    return pltpu.CompilerParams(dimension_semantics=("arbitrary",) * n_grid,
                                vmem_limit_bytes=VMEM_LIMIT)


def _ada_kernel(c_ref, w_ref, b_ref, o_ref):
    a = _silu(c_ref[...]).astype(BF16)
    o_ref[0] = jnp.dot(a, w_ref[0].astype(BF16), preferred_element_type=F32) + b_ref[0]


def _ada(c, w, b, tn):
    L, D, N = w.shape
    return pl.pallas_call(
        _ada_kernel,
        grid=(L, N // tn),
        in_specs=[pl.BlockSpec((MOD_ROWS, D), lambda l, j: (0, 0)),
                  pl.BlockSpec((1, D, tn), lambda l, j: (l, 0, j)),
                  pl.BlockSpec((1, 1, tn), lambda l, j: (l, 0, j))],
        out_specs=pl.BlockSpec((1, MOD_ROWS, tn), lambda l, j: (l, 0, j)),
        out_shape=jax.ShapeDtypeStruct((L, MOD_ROWS, N), F32),
        compiler_params=_params(2),
        name="ada",
    )(c, w, b.reshape(L, 1, N))


def _pw1_kernel(x_ref, mod_ref, w_ref, b_ref, u_ref):
    bb, tt, D = x_ref.shape
    shift = mod_ref[:, 0:1, :]
    scale = mod_ref[:, 1:2, :]
    h = (x_ref[...] * (1 + scale) + shift).astype(BF16).reshape(bb * tt, D)
    z = jnp.dot(h, w_ref[...], preferred_element_type=F32) + b_ref[...]
    u = z[:, :D] * jax.nn.sigmoid(z[:, D:])
    u_ref[...] = u.reshape(bb, tt, D)


def _pw1(x, mod, mod_row0, w, b, bb, tt):
    B, T, D = x.shape
    mb = mod_row0 // bb
    return pl.pallas_call(
        _pw1_kernel,
        grid=(B // bb, T // tt),
        in_specs=[pl.BlockSpec((bb, tt, D), lambda b_, i: (b_, i, 0)),
                  pl.BlockSpec((bb, 6, D), lambda b_, i: (b_ + mb, 0, 0)),
                  _const_spec((D, 2 * D)),
                  _const_spec((1, 2 * D))],
        out_specs=pl.BlockSpec((bb, tt, D), lambda b_, i: (b_, i, 0)),
        out_shape=jax.ShapeDtypeStruct((B, T, D), F32),
        compiler_params=_params(2),
        name="pw1",
    )(x, mod, w, b.reshape(1, 2 * D))


CONV_ROWS = 64
CONV_LANES = 256
SUBLANES = 8


def _conv_kernel(x_ref, u_ref, halo_ref, st_ref, mod_ref, wdw_ref, bdw_ref, cg_ref, cb_ref,
                 w2_ref, b2_ref, lng_ref, lnb_ref, o_ref, ext_ref, sh_ref, y_ref, *, alpha):
    bb, tt, D = x_ref.shape
    i = pl.program_id(1)

    @pl.when(i == 0)
    def _():
        ext_ref[:, 0:HALO, :] = st_ref[...]

    @pl.when(i > 0)
    def _():
        ext_ref[:, 0:HALO, :] = halo_ref[...]

    ext_ref[:, HALO:, :] = u_ref[...]

    first = HALO - (CONV_WIDTH - 1)
    rows = min(CONV_ROWS, tt)
    for b in range(bb):
        for lb in range(D // CONV_LANES):
            lanes = slice(lb * CONV_LANES, (lb + 1) * CONV_LANES)
            sh_ref[0] = ext_ref[b, :, lanes]
            for s in range(1, SUBLANES):
                sh_ref[s, 0:tt + HALO - SUBLANES, :] = ext_ref[b, s:s + tt + HALO - SUBLANES, lanes]

            def chunk(c, carry, b=b, lanes=lanes):
                r0 = pl.multiple_of(c * rows, rows)
                acc = jnp.zeros((rows, CONV_LANES), F32)
                for j in range(CONV_WIDTH):
                    a, s = divmod(j + first, SUBLANES)
                    acc = acc + wdw_ref[j:j + 1, lanes] * sh_ref[s, pl.ds(r0 + SUBLANES * a, rows), :]
                y_ref[b, pl.ds(r0, rows), lanes] = acc + bdw_ref[:, lanes]
                return carry
            lax.fori_loop(0, tt // rows, chunk, 0)

    yn = _silu(_layer_norm(y_ref[...], cg_ref[...], cb_ref[...]))
    out = jnp.dot(yn.astype(BF16).reshape(bb * tt, D), w2_ref[...],
                  preferred_element_type=F32) + b2_ref[...]
    gate = mod_ref[:, 2:3, :]
    v = alpha * x_ref[...] + gate * out.reshape(bb, tt, D)
    o_ref[...] = _layer_norm(v, lng_ref[...], lnb_ref[...])


def _conv(x, u, state, mod, mod_row0, w_dw, b_dw, cg, cb, w2, b2, lng, lnb, bb, tt, alpha):
    B, T, D = x.shape
    mb = mod_row0 // bb
    hb = tt // HALO
    wdw = jnp.pad(w_dw, ((0, HALO - CONV_WIDTH), (0, 0)))
    row = lambda a: a.reshape(1, D)
    return pl.pallas_call(
        functools.partial(_conv_kernel, alpha=alpha),
        grid=(B // bb, T // tt),
        in_specs=[pl.BlockSpec((bb, tt, D), lambda b_, i: (b_, i, 0)),
                  pl.BlockSpec((bb, tt, D), lambda b_, i: (b_, i, 0)),
                  pl.BlockSpec((bb, HALO, D), lambda b_, i: (b_, jnp.maximum(i * hb - 1, 0), 0)),
                  pl.BlockSpec((bb, HALO, D), lambda b_, i: (b_, 0, 0)),
                  pl.BlockSpec((bb, 6, D), lambda b_, i: (b_ + mb, 0, 0)),
                  _const_spec((HALO, D)),
                  _const_spec((1, D)), _const_spec((1, D)), _const_spec((1, D)),
                  _const_spec((D, D)),
                  _const_spec((1, D)), _const_spec((1, D)), _const_spec((1, D))],
        out_specs=pl.BlockSpec((bb, tt, D), lambda b_, i: (b_, i, 0)),
        out_shape=jax.ShapeDtypeStruct((B, T, D), F32),
        scratch_shapes=[pltpu.VMEM((bb, HALO + tt, D), F32),
                        pltpu.VMEM((SUBLANES, HALO + tt, CONV_LANES), F32),
                        pltpu.VMEM((bb, tt, D), F32)],
        compiler_params=_params(2),
        name="conv",
    )(x, u, u, state, mod, wdw, row(b_dw), row(cg), row(cb), w2, row(b2), row(lng), row(lnb))


def _ff_chunks(F, width):
    return [(f0, min(width, F - f0)) for f0 in range(0, F, width)]


def _swiglu(h, wg_ref, wu_ref, wd_ref, lead):
    F = wg_ref.shape[-1]
    acc = None
    for f0, fw in _ff_chunks(F, 1024):
        g = jnp.dot(h, wg_ref[lead + (slice(None), slice(f0, f0 + fw))], preferred_element_type=F32)
        u = jnp.dot(h, wu_ref[lead + (slice(None), slice(f0, f0 + fw))], preferred_element_type=F32)
        a = (_silu(g) * u).astype(BF16)
        d = jnp.dot(a, wd_ref[lead + (slice(f0, f0 + fw), slice(None))], preferred_element_type=F32)
        acc = d if acc is None else acc + d
    return acc


def _ffn_kernel(x_ref, mod_ref, wg_ref, wu_ref, wd_ref, lng_ref, lnb_ref, o_ref, *, alpha):
    bb, tt, D = x_ref.shape
    shift = mod_ref[:, 3:4, :]
    scale = mod_ref[:, 4:5, :]
    gate = mod_ref[:, 5:6, :]
    x = x_ref[...]
    h = (x * (1 + scale) + shift).astype(BF16).reshape(bb * tt, D)
    f = _swiglu(h, wg_ref, wu_ref, wd_ref, ())
    v = alpha * x + gate * f.reshape(bb, tt, D)
    o_ref[...] = _layer_norm(v, lng_ref[...], lnb_ref[...])


def _ffn(x, mod, mod_row0, wg, wu, wd, lng, lnb, bb, tt, alpha):
    B, T, D = x.shape
    F = wg.shape[-1]
    mb = mod_row0 // bb
    return pl.pallas_call(
        functools.partial(_ffn_kernel, alpha=alpha),
        grid=(B // bb, T // tt),
        in_specs=[pl.BlockSpec((bb, tt, D), lambda b_, i: (b_, i, 0)),
                  pl.BlockSpec((bb, 6, D), lambda b_, i: (b_ + mb, 0, 0)),
                  _const_spec((D, F)), _const_spec((D, F)), _const_spec((F, D)),
                  _const_spec((1, D)), _const_spec((1, D))],
        out_specs=pl.BlockSpec((bb, tt, D), lambda b_, i: (b_, i, 0)),
        out_shape=jax.ShapeDtypeStruct((B, T, D), F32),
        compiler_params=_params(2),
        name="ffn",
    )(x, mod, wg, wu, wd, lng.reshape(1, D), lnb.reshape(1, D))


def _router_gates(h32, wr_ref, br_ref, n_exp):
    logits = [jnp.sum(h32 * wr_ref[e:e + 1, :], axis=-1, keepdims=True) + br_ref[0:1, e:e + 1]
              for e in range(n_exp)]
    m = functools.reduce(jnp.maximum, logits)
    ex = [jnp.exp(l - m) for l in logits]
    s = functools.reduce(lambda a, b: a + b, ex)
    p = [e_ / s for e_ in ex]

    def argbest(cols):
        best, idx = cols[0], jnp.zeros_like(cols[0], dtype=jnp.int32)
        for e in range(1, n_exp):
            better = cols[e] > best
            best = jnp.where(better, cols[e], best)
            idx = jnp.where(better, e, idx)
        return best, idx

    p1, i1 = argbest(p)
    p2, i2 = argbest([jnp.where(i1 == e, -1.0, p[e]) for e in range(n_exp)])
    tot = p1 + p2
    g1, g2 = p1 / tot, p2 / tot
    return [jnp.where(i1 == e, g1, 0.0) + jnp.where(i2 == e, g2, 0.0) for e in range(n_exp)]


def _moe_kernel(x_ref, mod_ref, wr_ref, br_ref, wg_ref, wu_ref, wd_ref, lng_ref, lnb_ref, o_ref,
                h_ref, gate_ref, acc_ref, *, alpha, n_exp):
    bb, tt, D = x_ref.shape
    e = pl.program_id(2)

    @pl.when(e == 0)
    def _():
        shift = mod_ref[:, 3:4, :]
        scale = mod_ref[:, 4:5, :]
        h32 = (x_ref[...] * (1 + scale) + shift).reshape(bb * tt, D)
        h_ref[...] = h32.astype(BF16)
        gates = _router_gates(h32, wr_ref, br_ref, n_exp)
        for k in range(n_exp):
            gate_ref[k] = gates[k]
        acc_ref[...] = jnp.zeros_like(acc_ref)

    f = _swiglu(h_ref[...], wg_ref, wu_ref, wd_ref, (0,))
    acc_ref[...] += gate_ref[e] * f

    @pl.when(e == n_exp - 1)
    def _():
        v = alpha * x_ref[...] + mod_ref[:, 5:6, :] * acc_ref[...].reshape(bb, tt, D)
        o_ref[...] = _layer_norm(v, lng_ref[...], lnb_ref[...])


def _moe(x, mod, mod_row0, wr, br, wg, wu, wd, lng, lnb, bb, tt, alpha):
    B, T, D = x.shape
    E, _, Fe = wg.shape
    mb = mod_row0 // bb
    M = bb * tt
    return pl.pallas_call(
        functools.partial(_moe_kernel, alpha=alpha, n_exp=E),
        grid=(B // bb, T // tt, E),
        in_specs=[pl.BlockSpec((bb, tt, D), lambda b_, i, e: (b_, i, 0)),
                  pl.BlockSpec((bb, 6, D), lambda b_, i, e: (b_ + mb, 0, 0)),
                  _const_spec((E, D)), _const_spec((1, E)),
                  pl.BlockSpec((1, D, Fe), lambda b_, i, e: (e, 0, 0)),
                  pl.BlockSpec((1, D, Fe), lambda b_, i, e: (e, 0, 0)),
                  pl.BlockSpec((1, Fe, D), lambda b_, i, e: (e, 0, 0)),
                  _const_spec((1, D)), _const_spec((1, D))],
        out_specs=pl.BlockSpec((bb, tt, D), lambda b_, i, e: (b_, i, 0)),
        out_shape=jax.ShapeDtypeStruct((B, T, D), F32),
        scratch_shapes=[pltpu.VMEM((M, D), BF16), pltpu.VMEM((E, M, 1), F32),
                        pltpu.VMEM((M, D), F32)],
        compiler_params=_params(3),
        name="moe",
    )(x, mod, wr.T, br.reshape(1, E), wg, wu, wd, lng.reshape(1, D), lnb.reshape(1, D))


def _kv_kernel(x_ref, mod_ref, wk_ref, wv_ref, kb_ref, vb_ref, kf_ref, vf_ref):
    bb, tt, D = x_ref.shape
    i = pl.program_id(1)
    shift = mod_ref[:, 0:1, :]
    scale = mod_ref[:, 1:2, :]
    h = (x_ref[...] * (1 + scale) + shift).astype(BF16).reshape(bb * tt, D)
    k = jnp.dot(h, wk_ref[...], preferred_element_type=F32).reshape(bb, tt, D)
    v = jnp.dot(h, wv_ref[...], preferred_element_type=F32).reshape(bb, tt, D)
    kb_ref[...] = k.astype(BF16)
    vb_ref[...] = v.astype(BF16)

    @pl.when(i == pl.num_programs(1) - 1)
    def _():
        kf_ref[...] = k
        vf_ref[...] = v


def _kv(x, mod, mod_row0, wk, wv, bb, tt):
    B, T, D = x.shape
    mb = mod_row0 // bb
    blk = pl.BlockSpec((bb, tt, D), lambda b_, i: (b_, i, 0))
    last = pl.BlockSpec((bb, tt, D), lambda b_, i: (b_, 0, 0))
    return pl.pallas_call(
        _kv_kernel,
        grid=(B // bb, T // tt),
        in_specs=[blk,
                  pl.BlockSpec((bb, 2, D), lambda b_, i: (b_ + mb, 0, 0)),
                  _const_spec((D, D)), _const_spec((D, D))],
        out_specs=[blk, blk, last, last],
        out_shape=[jax.ShapeDtypeStruct((B, T, D), BF16), jax.ShapeDtypeStruct((B, T, D), BF16),
                   jax.ShapeDtypeStruct((B, tt, D), F32), jax.ShapeDtypeStruct((B, tt, D), F32)],
        compiler_params=_params(2),
        name="kv",
    )(x, mod, wk, wv)


def _attn_kernel(x_ref, mod_ref, kp_ref, kc_ref, vp_ref, vc_ref, bias_ref, wq_ref, wo_ref,
                 lng_ref, lnb_ref, o_ref, kcat_ref, vcat_ref, q_ref, a_ref,
                 *, alpha, chunk, past, mask_first, q_scale):
    _, tt, D = x_ref.shape
    n_chunks = tt // chunk
    win = past + chunk
    head_dim = LANES // 2
    i = pl.program_id(1)

    kcat_ref[0:past, :] = kp_ref[0]
    kcat_ref[past:, :] = kc_ref[0]
    vcat_ref[0:past, :] = vp_ref[0]
    vcat_ref[past:, :] = vc_ref[0]

    x = x_ref[0]
    h = (x * (1 + mod_ref[0, 1:2, :]) + mod_ref[0, 0:1, :]).astype(BF16)
    q = jnp.dot(h, wq_ref[...], preferred_element_type=F32) * q_scale
    q_ref[...] = q.astype(BF16)

    row_head = lax.broadcasted_iota(jnp.int32, (2 * chunk, LANES), 0) >= chunk
    lane_head = lax.broadcasted_iota(jnp.int32, (2 * chunk, LANES), 1) >= head_dim
    own = row_head == lane_head
    out_lane_head0 = lax.broadcasted_iota(jnp.int32, (chunk, LANES), 1) < head_dim
    key_row = lax.broadcasted_iota(jnp.int32, (win, 1), 0)

    def one_chunk(g, carry):
        r0 = pl.multiple_of(g * chunk, chunk)
        if mask_first:
            valid = key_row + r0 >= jnp.where(i > 0, 0, past)
        for pr in range(D // LANES):
            lanes = slice(pr * LANES, (pr + 1) * LANES)
            qp = q_ref[pl.ds(r0, chunk), lanes]
            qbd = jnp.where(own, jnp.concatenate([qp, qp], axis=0), jnp.zeros((), BF16))
            kw = kcat_ref[pl.ds(r0, win), lanes]
            s = lax.dot_general(kw, qbd, (((1,), (1,)), ((), ())), preferred_element_type=F32)
            s = s + bias_ref[pr]
            if mask_first:
                s = jnp.where(valid, s, NEG_INF)
            m = jnp.max(s, axis=0, keepdims=True)
            ex = jnp.exp(s - m)
            l = jnp.sum(ex, axis=0, keepdims=True)
            p = (ex * (1.0 / l)).astype(BF16)
            vw = vcat_ref[pl.ds(r0, win), lanes]
            r = lax.dot_general(p, vw, (((0,), (0,)), ((), ())), preferred_element_type=F32)
            a_ref[pl.ds(r0, chunk), lanes] = jnp.where(out_lane_head0, r[:chunk], r[chunk:]).astype(BF16)
        return carry

    lax.fori_loop(0, n_chunks, one_chunk, 0)

    out = jnp.dot(a_ref[...], wo_ref[...], preferred_element_type=F32)
    v = alpha * x + mod_ref[0, 2:3, :] * out
    o_ref[0] = _layer_norm(v, lng_ref[...], lnb_ref[...])


def _pair_bias(table, past, chunk):
    H = table.shape[0]
    d = past + jnp.arange(chunk)[None, :] - jnp.arange(past + chunk)[:, None]
    bias = table[:, jnp.clip(d, -REL_CLIP, REL_CLIP) + REL_CLIP]
    bias = bias.reshape(H // 2, 2, past + chunk, chunk).transpose(0, 2, 1, 3)
    return bias.reshape(H // 2, past + chunk, 2 * chunk)


def _attn(x, mod, mod_row0, k_prev, k_cur, v_prev, v_cur, table, wq, wo, lng, lnb,
          tt, chunk, mask_first, alpha):
    B, T, D = x.shape
    past = WINDOW
    assert past % tt == 0 or tt % past == 0
    assert tt == past or T == tt
    bias = _pair_bias(table, past, chunk)
    npairs = D // LANES
    head_dim = D // N_HEADS
    blk = pl.BlockSpec((1, tt, D), lambda b_, i: (b_, i, 0))
    prev = pl.BlockSpec((1, past, D), lambda b_, i: (b_, jnp.maximum(i - 1, 0), 0))
    return pl.pallas_call(
        functools.partial(_attn_kernel, alpha=alpha, chunk=chunk, past=past,
                          mask_first=mask_first, q_scale=head_dim ** -0.5),
        grid=(B, T // tt),
        in_specs=[blk,
                  pl.BlockSpec((1, 6, D), lambda b_, i: (b_ + mod_row0, 0, 0)),
                  prev, blk, prev, blk,
                  _const_spec((npairs, past + chunk, 2 * chunk)),
                  _const_spec((D, D)), _const_spec((D, D)),
                  _const_spec((1, D)), _const_spec((1, D))],
        out_specs=blk,
        out_shape=jax.ShapeDtypeStruct((B, T, D), F32),
        scratch_shapes=[pltpu.VMEM((past + tt, D), BF16), pltpu.VMEM((past + tt, D), BF16),
                        pltpu.VMEM((tt, D), BF16), pltpu.VMEM((tt, D), BF16)],
        compiler_params=_params(2),
        name="attn",
    )(x, mod, k_prev, k_cur, v_prev, v_cur, bias, wq, wo, lng.reshape(1, D), lnb.reshape(1, D))


def _trunk(x, mods, mods_kv, mod_row0, conv_state, kv_cache, p, bb, tt, chunk):
    B, T, D = x.shape
    depth = p['w_ada'].shape[0]
    n_a = p['w_pw1'].shape[0]
    alpha = (2.0 * depth) ** 0.25
    assert T >= CONV_WIDTH - 1 and T % tt == 0 and B % bb == 0 and mod_row0 % bb == 0
    new_conv = []
    kb = vb = kf = vf = None
    for l in range(depth):
        mod = mods[l]
        if l < n_a:
            u = _pw1(x, mod, mod_row0, p['w_pw1'][l], p['b_pw1'][l], bb, tt)
            x = _conv(x, u, conv_state[l], mod, mod_row0, p['w_dw'][l], p['b_dw'][l],
                      p['conv_ln_g'][l], p['conv_ln_b'][l], p['w_pw2'][l], p['b_pw2'][l],
                      p['ln_g'][l, 0], p['ln_b'][l, 0], bb, tt, alpha)
            new_conv.append(u[:, T - (CONV_WIDTH - 1):])
        else:
            j = l - n_a
            if kv_cache is None:
                x = _attn(x, mod, mod_row0, kb, kb, vb, vb, p['rel_bias'][j], p['w_q'][j], p['w_o'][j],
                          p['ln_g'][l, 0], p['ln_b'][l, 0], WINDOW, chunk, True, alpha)
            else:
                x = _attn(x, mod, mod_row0, kv_cache[0], kb, kv_cache[1], vb, p['rel_bias'][j],
                          p['w_q'][j], p['w_o'][j], p['ln_g'][l, 0], p['ln_b'][l, 0],
                          T, chunk, False, alpha)
        i = l // 2
        if l % 2 == 0:
            x = _ffn(x, mod, mod_row0, p['w_ff_gate'][i], p['w_ff_up'][i], p['w_ff_down'][i],
                     p['ln_g'][l, 1], p['ln_b'][l, 1], bb, tt, alpha)
        else:
            x = _moe(x, mod, mod_row0, p['w_router'][i], p['b_router'][i], p['w_e_gate'][i],
                     p['w_e_up'][i], p['w_e_down'][i], p['ln_g'][l, 1], p['ln_b'][l, 1],
                     bb, tt, alpha)
        if l == n_a - 1:
            kb, vb, kf, vf = _kv(x, mods_kv, mod_row0, p['w_k'], p['w_v'], bb, tt)
    return x, jnp.stack(new_conv), kf, vf


def kernel(x_prompt, x_sample, c_prompt, c_sample, state_conv, cache_k, cache_v, w_ada, b_ada, ln_g, ln_b, w_pw1, b_pw1, w_dw, b_dw, conv_ln_g, conv_ln_b, w_pw2, b_pw2, w_ada_kv, b_ada_kv, w_k, w_v, w_q, w_o, rel_bias, w_ff_gate, w_ff_up, w_ff_down, w_router, b_router, w_e_gate, w_e_up, w_e_down):
    B, T, D = x_prompt.shape
    Bs, Ts, _ = x_sample.shape
    H = N_HEADS
    assert Bs + B <= MOD_ROWS and D == H * (LANES // 2)
    depth = w_ada.shape[0]
    n_a = w_pw1.shape[0]

    bf = lambda w: w.astype(BF16)
    p = dict(w_ada=w_ada, ln_g=ln_g, ln_b=ln_b, w_pw1=bf(w_pw1), b_pw1=b_pw1, w_dw=w_dw, b_dw=b_dw,
             conv_ln_g=conv_ln_g, conv_ln_b=conv_ln_b, w_pw2=bf(w_pw2), b_pw2=b_pw2,
             w_k=bf(w_k), w_v=bf(w_v), w_q=bf(w_q), w_o=bf(w_o), rel_bias=rel_bias,
             w_ff_gate=bf(w_ff_gate), w_ff_up=bf(w_ff_up), w_ff_down=bf(w_ff_down),
             w_router=w_router, b_router=b_router,
             w_e_gate=bf(w_e_gate), w_e_up=bf(w_e_up), w_e_down=bf(w_e_down))

    c_all = jnp.concatenate([c_sample, c_prompt, jnp.zeros((MOD_ROWS - Bs - B, D), F32)], axis=0)
    mods = _ada(c_all, w_ada, b_ada, 1536).reshape(depth, MOD_ROWS, 6, D)
    mods_kv = _ada(c_all, w_ada_kv[None], b_ada_kv[None], 1024).reshape(MOD_ROWS, 2, D)

    pad_state = lambda s: jnp.pad(s, ((0, 0), (0, 0), (HALO - (CONV_WIDTH - 1), 0), (0, 0)))
    zero_state = jnp.zeros((n_a, B, HALO, D), F32)
    y_p, conv_p, kf_p, vf_p = _trunk(x_prompt, mods, mods_kv, Bs, zero_state, None, p, 1, WINDOW, CHUNK)

    cache = (cache_k.reshape(Bs, -1, D).astype(BF16), cache_v.reshape(Bs, -1, D).astype(BF16))
    y_s, conv_s, kf_s, vf_s = _trunk(x_sample, mods, mods_kv, 0, pad_state(state_conv), cache, p,
                                     Bs, Ts, Ts)

    L = cache_k.shape[1]
    heads = lambda a: a.reshape(a.shape[0], a.shape[1], H, D // H)
    k_s = jnp.concatenate([cache_k, heads(kf_s)], axis=1)[:, -L:]
    v_s = jnp.concatenate([cache_v, heads(vf_s)], axis=1)[:, -L:]
    return (y_p, y_s, conv_p, heads(kf_p), heads(vf_p), conv_s, k_s, v_s)
```

```python
import functools

import jax
import jax.numpy as jnp
from jax import lax
from jax.experimental import pallas as pl
from jax.experimental.pallas import tpu as pltpu

CHUNK = 64
WINDOW = 512
REL_CLIP = 128
CONV_WIDTH = 31
N_HEADS = 16
LN_EPS = 1e-5
NEG_INF = -1e30

LANES = 128
HALO = 32
MOD_ROWS = 16
VMEM_LIMIT = 56 * 1024 * 1024

F32 = jnp.float32
BF16 = jnp.bfloat16


def _layer_norm(v, g, b):
    mu = jnp.mean(v, axis=-1, keepdims=True)
    d = v - mu
    var = jnp.mean(d * d, axis=-1, keepdims=True)
    return d * lax.rsqrt(var + LN_EPS) * g + b


def _silu(v):
    return v * jax.nn.sigmoid(v)


def _const_spec(shape):
    nd = len(shape)
    return pl.BlockSpec(shape, lambda *_: (0,) * nd, pipeline_mode=pl.Buffered(1))


def _params(n_grid):
    return pltpu.CompilerParams(dimension_semantics=("arbitrary",) * n_grid,
                                vmem_limit_bytes=VMEM_LIMIT)


def _ada_kernel(c_ref, w_ref, b_ref, o_ref):
    a = _silu(c_ref[...]).astype(BF16)
    o_ref[0] = jnp.dot(a, w_ref[0].astype(BF16), preferred_element_type=F32) + b_ref[0]


def _ada(c, w, b, tn):
    L, D, N = w.shape
    return pl.pallas_call(
        _ada_kernel,
        grid=(L, N // tn),
        in_specs=[pl.BlockSpec((MOD_ROWS, D), lambda l, j: (0, 0)),
                  pl.BlockSpec((1, D, tn), lambda l, j: (l, 0, j)),
                  pl.BlockSpec((1, 1, tn), lambda l, j: (l, 0, j))],
        out_specs=pl.BlockSpec((1, MOD_ROWS, tn), lambda l, j: (l, 0, j)),
        out_shape=jax.ShapeDtypeStruct((L, MOD_ROWS, N), F32),
        compiler_params=_params(2),
        name="ada",
    )(c, w, b.reshape(L, 1, N))


def _pw1_kernel(x_ref, mod_ref, w_ref, b_ref, u_ref):
    bb, tt, D = x_ref.shape
    shift = mod_ref[:, 0:1, :]
    scale = mod_ref[:, 1:2, :]
    h = (x_ref[...] * (1 + scale) + shift).astype(BF16).reshape(bb * tt, D)
    z = jnp.dot(h, w_ref[...], preferred_element_type=F32) + b_ref[...]
    u = z[:, :D] * jax.nn.sigmoid(z[:, D:])
    u_ref[...] = u.reshape(bb, tt, D)


def _pw1(x, mod, mod_row0, w, b, bb, tt):
    B, T, D = x.shape
    mb = mod_row0 // bb
    return pl.pallas_call(
        _pw1_kernel,
        grid=(B // bb, T // tt),
        in_specs=[pl.BlockSpec((bb, tt, D), lambda b_, i: (b_, i, 0)),
                  pl.BlockSpec((bb, 6, D), lambda b_, i: (b_ + mb, 0, 0)),
                  _const_spec((D, 2 * D)),
                  _const_spec((1, 2 * D))],
        out_specs=pl.BlockSpec((bb, tt, D), lambda b_, i: (b_, i, 0)),
        out_shape=jax.ShapeDtypeStruct((B, T, D), F32),
        compiler_params=_params(2),
        name="pw1",
    )(x, mod, w, b.reshape(1, 2 * D))


CONV_ROWS = 64
CONV_LANES = 256
SUBLANES = 8


def _conv_kernel(x_ref, u_ref, halo_ref, st_ref, mod_ref, wdw_ref, bdw_ref, cg_ref, cb_ref,
                 w2_ref, b2_ref, lng_ref, lnb_ref, o_ref, ext_ref, sh_ref, y_ref, *, alpha):
    bb, tt, D = x_ref.shape
    i = pl.program_id(1)

    @pl.when(i == 0)
    def _():
        ext_ref[:, 0:HALO, :] = st_ref[...]

    @pl.when(i > 0)
    def _():
        ext_ref[:, 0:HALO, :] = halo_ref[...]

    ext_ref[:, HALO:, :] = u_ref[...]

    first = HALO - (CONV_WIDTH - 1)
    rows = min(CONV_ROWS, tt)
    for b in range(bb):
        for lb in range(D // CONV_LANES):
            lanes = slice(lb * CONV_LANES, (lb + 1) * CONV_LANES)
            sh_ref[0] = ext_ref[b, :, lanes]
            for s in range(1, SUBLANES):
                sh_ref[s, 0:tt + HALO - SUBLANES, :] = ext_ref[b, s:s + tt + HALO - SUBLANES, lanes]

            def chunk(c, carry, b=b, lanes=lanes):
                r0 = pl.multiple_of(c * rows, rows)
                acc = jnp.zeros((rows, CONV_LANES), F32)
                for j in range(CONV_WIDTH):
                    a, s = divmod(j + first, SUBLANES)
                    acc = acc + wdw_ref[j:j + 1, lanes] * sh_ref[s, pl.ds(r0 + SUBLANES * a, rows), :]
                y_ref[b, pl.ds(r0, rows), lanes] = acc + bdw_ref[:, lanes]
                return carry
            lax.fori_loop(0, tt // rows, chunk, 0)

    yn = _silu(_layer_norm(y_ref[...], cg_ref[...], cb_ref[...]))
    out = jnp.dot(yn.astype(BF16).reshape(bb * tt, D), w2_ref[...],
                  preferred_element_type=F32) + b2_ref[...]
    gate = mod_ref[:, 2:3, :]
    v = alpha * x_ref[...] + gate * out.reshape(bb, tt, D)
    o_ref[...] = _layer_norm(v, lng_ref[...], lnb_ref[...])


def _conv(x, u, state, mod, mod_row0, w_dw, b_dw, cg, cb, w2, b2, lng, lnb, bb, tt, alpha):
    B, T, D = x.shape
    mb = mod_row0 // bb
    hb = tt // HALO
    wdw = jnp.pad(w_dw, ((0, HALO - CONV_WIDTH), (0, 0)))
    row = lambda a: a.reshape(1, D)
    return pl.pallas_call(
        functools.partial(_conv_kernel, alpha=alpha),
        grid=(B // bb, T // tt),
        in_specs=[pl.BlockSpec((bb, tt, D), lambda b_, i: (b_, i, 0)),
                  pl.BlockSpec((bb, tt, D), lambda b_, i: (b_, i, 0)),
                  pl.BlockSpec((bb, HALO, D), lambda b_, i: (b_, jnp.maximum(i * hb - 1, 0), 0)),
                  pl.BlockSpec((bb, HALO, D), lambda b_, i: (b_, 0, 0)),
                  pl.BlockSpec((bb, 6, D), lambda b_, i: (b_ + mb, 0, 0)),
                  _const_spec((HALO, D)),
                  _const_spec((1, D)), _const_spec((1, D)), _const_spec((1, D)),
                  _const_spec((D, D)),
                  _const_spec((1, D)), _const_spec((1, D)), _const_spec((1, D))],
        out_specs=pl.BlockSpec((bb, tt, D), lambda b_, i: (b_, i, 0)),
        out_shape=jax.ShapeDtypeStruct((B, T, D), F32),
        scratch_shapes=[pltpu.VMEM((bb, HALO + tt, D), F32),
                        pltpu.VMEM((SUBLANES, HALO + tt, CONV_LANES), F32),
                        pltpu.VMEM((bb, tt, D), F32)],
        compiler_params=_params(2),
        name="conv",
    )(x, u, u, state, mod, wdw, row(b_dw), row(cg), row(cb), w2, row(b2), row(lng), row(lnb))


def _ff_chunks(F, width):
    return [(f0, min(width, F - f0)) for f0 in range(0, F, width)]


def _swiglu(h, wg_ref, wu_ref, wd_ref, lead):
    F = wg_ref.shape[-1]
    acc = None
    for f0, fw in _ff_chunks(F, 1024):
        g = jnp.dot(h, wg_ref[lead + (slice(None), slice(f0, f0 + fw))], preferred_element_type=F32)
        u = jnp.dot(h, wu_ref[lead + (slice(None), slice(f0, f0 + fw))], preferred_element_type=F32)
        a = (_silu(g) * u).astype(BF16)
        d = jnp.dot(a, wd_ref[lead + (slice(f0, f0 + fw), slice(None))], preferred_element_type=F32)
        acc = d if acc is None else acc + d
    return acc


def _ffn_kernel(x_ref, mod_ref, wg_ref, wu_ref, wd_ref, lng_ref, lnb_ref, o_ref, *, alpha):
    bb, tt, D = x_ref.shape
    shift = mod_ref[:, 3:4, :]
    scale = mod_ref[:, 4:5, :]
    gate = mod_ref[:, 5:6, :]
    x = x_ref[...]
    h = (x * (1 + scale) + shift).astype(BF16).reshape(bb * tt, D)
    f = _swiglu(h, wg_ref, wu_ref, wd_ref, ())
    v = alpha * x + gate * f.reshape(bb, tt, D)
    o_ref[...] = _layer_norm(v, lng_ref[...], lnb_ref[...])


def _ffn(x, mod, mod_row0, wg, wu, wd, lng, lnb, bb, tt, alpha):
    B, T, D = x.shape
    F = wg.shape[-1]
    mb = mod_row0 // bb
    return pl.pallas_call(
        functools.partial(_ffn_kernel, alpha=alpha),
        grid=(B // bb, T // tt),
        in_specs=[pl.BlockSpec((bb, tt, D), lambda b_, i: (b_, i, 0)),
                  pl.BlockSpec((bb, 6, D), lambda b_, i: (b_ + mb, 0, 0)),
                  _const_spec((D, F)), _const_spec((D, F)), _const_spec((F, D)),
                  _const_spec((1, D)), _const_spec((1, D))],
        out_specs=pl.BlockSpec((bb, tt, D), lambda b_, i: (b_, i, 0)),
        out_shape=jax.ShapeDtypeStruct((B, T, D), F32),
        compiler_params=_params(2),
        name="ffn",
    )(x, mod, wg, wu, wd, lng.reshape(1, D), lnb.reshape(1, D))


def _router_top2(h32, wr_ref, br_ref, n_exp):
    logits = [jnp.sum(h32 * wr_ref[e:e + 1, :], axis=-1, keepdims=True) + br_ref[0:1, e:e + 1]
              for e in range(n_exp)]
    m = functools.reduce(jnp.maximum, logits)
    ex = [jnp.exp(l - m) for l in logits]
    s = functools.reduce(lambda a, b: a + b, ex)
    p = [e_ / s for e_ in ex]

    def argbest(cols):
        best, idx = cols[0], jnp.zeros_like(cols[0], dtype=jnp.int32)
        for e in range(1, n_exp):
            better = cols[e] > best
            best = jnp.where(better, cols[e], best)
            idx = jnp.where(better, e, idx)
        return best, idx

    p1, i1 = argbest(p)
    p2, i2 = argbest([jnp.where(i1 == e, -1.0, p[e]) for e in range(n_exp)])
    tot = p1 + p2
    return i1, i2, p1 / tot, p2 / tot


def _router_gates(h32, wr_ref, br_ref, n_exp):
    i1, i2, g1, g2 = _router_top2(h32, wr_ref, br_ref, n_exp)
    return [jnp.where(i1 == e, g1, 0.0) + jnp.where(i2 == e, g2, 0.0) for e in range(n_exp)]


def _moe_kernel(x_ref, mod_ref, wr_ref, br_ref, wg_ref, wu_ref, wd_ref, lng_ref, lnb_ref, o_ref,
                h_ref, gate_ref, acc_ref, *, alpha, n_exp):
    bb, tt, D = x_ref.shape
    e = pl.program_id(2)

    @pl.when(e == 0)
    def _():
        shift = mod_ref[:, 3:4, :]
        scale = mod_ref[:, 4:5, :]
        h32 = (x_ref[...] * (1 + scale) + shift).reshape(bb * tt, D)
        h_ref[...] = h32.astype(BF16)
        gates = _router_gates(h32, wr_ref, br_ref, n_exp)
        for k in range(n_exp):
            gate_ref[k] = gates[k]
        acc_ref[...] = jnp.zeros_like(acc_ref)

    f = _swiglu(h_ref[...], wg_ref, wu_ref, wd_ref, (0,))
    acc_ref[...] += gate_ref[e] * f

    @pl.when(e == n_exp - 1)
    def _():
        v = alpha * x_ref[...] + mod_ref[:, 5:6, :] * acc_ref[...].reshape(bb, tt, D)
        o_ref[...] = _layer_norm(v, lng_ref[...], lnb_ref[...])


def _moe(x, mod, mod_row0, wr, br, wg, wu, wd, lng, lnb, bb, tt, alpha):
    B, T, D = x.shape
    E, _, Fe = wg.shape
    mb = mod_row0 // bb
    M = bb * tt
    return pl.pallas_call(
        functools.partial(_moe_kernel, alpha=alpha, n_exp=E),
        grid=(B // bb, T // tt, E),
        in_specs=[pl.BlockSpec((bb, tt, D), lambda b_, i, e: (b_, i, 0)),
                  pl.BlockSpec((bb, 6, D), lambda b_, i, e: (b_ + mb, 0, 0)),
                  _const_spec((E, D)), _const_spec((1, E)),
                  pl.BlockSpec((1, D, Fe), lambda b_, i, e: (e, 0, 0)),
                  pl.BlockSpec((1, D, Fe), lambda b_, i, e: (e, 0, 0)),
                  pl.BlockSpec((1, Fe, D), lambda b_, i, e: (e, 0, 0)),
                  _const_spec((1, D)), _const_spec((1, D))],
        out_specs=pl.BlockSpec((bb, tt, D), lambda b_, i, e: (b_, i, 0)),
        out_shape=jax.ShapeDtypeStruct((B, T, D), F32),
        scratch_shapes=[pltpu.VMEM((M, D), BF16), pltpu.VMEM((E, M, 1), F32),
                        pltpu.VMEM((M, D), F32)],
        compiler_params=_params(3),
        name="moe",
    )(x, mod, wr.T, br.reshape(1, E), wg, wu, wd, lng.reshape(1, D), lnb.reshape(1, D))


ROUTE_ROWS = 512
EXPERT_ROWS = 512
META_I1, META_I2, META_G1, META_G2, META_R1, META_R2 = range(6)


def _route_kernel(x_ref, mod_ref, wr_ref, br_ref, h_ref, meta_ref, cnt_ref, *, n_exp):
    _, tt, D = x_ref.shape
    first = jnp.logical_and(pl.program_id(0) == 0, pl.program_id(1) == 0)

    @pl.when(first)
    def _():
        cnt_ref[...] = jnp.zeros_like(cnt_ref)

    h32 = x_ref[0] * (1 + mod_ref[0, 4:5, :]) + mod_ref[0, 3:4, :]
    h_ref[...] = h32.reshape(tt, D // LANES, LANES)
    i1, i2, g1, g2 = _router_top2(h32, wr_ref, br_ref, n_exp)

    lane = lax.broadcasted_iota(jnp.int32, (tt, LANES), 1)
    o1 = (lane == i1).astype(F32)
    o2 = (lane == i2).astype(F32)
    lower = (lax.broadcasted_iota(jnp.int32, (tt, tt), 1) <
             lax.broadcasted_iota(jnp.int32, (tt, tt), 0)).astype(BF16)
    c1 = jnp.dot(lower, o1.astype(BF16), preferred_element_type=F32)
    c2 = jnp.dot(lower, o2.astype(BF16), preferred_element_type=F32)
    base = cnt_ref[0:1, :]
    tot1 = jnp.sum(o1, axis=0, keepdims=True)
    tot2 = jnp.sum(o2, axis=0, keepdims=True)
    r1 = jnp.sum(o1 * (c1 + base), axis=-1, keepdims=True)
    r2 = jnp.sum(o2 * (c2 + (base + tot1)), axis=-1, keepdims=True)
    cnt_ref[0:1, :] = base + tot1 + tot2

    cols = {META_I1: i1.astype(F32), META_I2: i2.astype(F32), META_G1: g1, META_G2: g2,
            META_R1: r1, META_R2: r2}
    meta = jnp.zeros((tt, LANES), F32)
    for k, col in cols.items():
        meta = jnp.where(lane == k, col, meta)
    meta_ref[...] = meta


def _route(x, mod, mod_row0, wr, br, tt):
    B, T, D = x.shape
    E = wr.shape[1]
    nT = T // tt
    return pl.pallas_call(
        functools.partial(_route_kernel, n_exp=E),
        grid=(B, nT),
        in_specs=[pl.BlockSpec((1, tt, D), lambda b_, i: (b_, i, 0)),
                  pl.BlockSpec((1, 6, D), lambda b_, i: (b_ + mod_row0, 0, 0)),
                  _const_spec((E, D)), _const_spec((1, E))],
        out_specs=[pl.BlockSpec((tt, D // LANES, LANES), lambda b_, i: (b_ * nT + i, 0, 0)),
                   pl.BlockSpec((tt, LANES), lambda b_, i: (b_ * nT + i, 0)),
                   pl.BlockSpec((SUBLANES, LANES), lambda b_, i: (0, 0))],
        out_shape=[jax.ShapeDtypeStruct((B * T, D // LANES, LANES), F32),
                   jax.ShapeDtypeStruct((B * T, LANES), F32),
                   jax.ShapeDtypeStruct((SUBLANES, LANES), F32)],
        compiler_params=_params(2),
        name="route",
    )(x, mod, wr.T, br.reshape(1, E))


def _row_copy(src_ref, src_row, dst_ref, dst_row, sem):
    return pltpu.make_async_copy(src_ref.at[pl.ds(src_row, 1)], dst_ref.at[pl.ds(dst_row, 1)], sem)


def _dispatch_kernel(pad_ref, pos_ref, h_ref, xs_ref, zero_ref, sems, *, n_exp):
    tb = pos_ref.shape[2] // 2
    tm = zero_ref.shape[0]
    i = pl.program_id(0)
    last = pl.num_programs(0) - 1
    pad_sem = sems.at[2]

    def tail_copy(t):
        return pltpu.make_async_copy(zero_ref, xs_ref.at[pl.ds(pl.multiple_of(t * tm, tm), tm)], pad_sem)

    @pl.when(i == 0)
    def _():
        zero_ref[...] = jnp.zeros_like(zero_ref)
        for e in range(n_exp):
            def fill(r, c):
                _row_copy(zero_ref, 0, xs_ref, r, pad_sem).start()
                return c
            lax.fori_loop(pad_ref[0, e], pad_ref[1, e], fill, 0)

        def fill_tail(t, c):
            tail_copy(t).start()
            return c
        lax.fori_loop(pad_ref[0, n_exp], pad_ref[1, n_exp], fill_tail, 0)

    def issue(t, c):
        _row_copy(h_ref, i * tb + t, xs_ref, pos_ref[0, 0, t], sems.at[i % 2]).start()
        _row_copy(h_ref, i * tb + t, xs_ref, pos_ref[0, 0, tb + t], sems.at[i % 2]).start()
        return c
    lax.fori_loop(0, tb, issue, 0, unroll=8)

    def wait_step(slot):
        pltpu.make_async_copy(h_ref.at[pl.ds(0, 2 * tb)], xs_ref.at[pl.ds(0, 2 * tb)],
                              sems.at[slot]).wait()

    @pl.when(i > 0)
    def _():
        wait_step((i + 1) % 2)

    @pl.when(i == last)
    def _():
        wait_step(i % 2)
        for e in range(n_exp):
            def drain(r, c):
                _row_copy(zero_ref, 0, xs_ref, r, pad_sem).wait()
                return c
            lax.fori_loop(pad_ref[0, e], pad_ref[1, e], drain, 0)

        def drain_tail(t, c):
            tail_copy(t).wait()
            return c
        lax.fori_loop(pad_ref[0, n_exp], pad_ref[1, n_exp], drain_tail, 0)


def _dispatch(pads, pos, h, n_rows, n_exp, tm):
    nb, _, tb2 = pos.shape
    S, L = h.shape[1:]
    return pl.pallas_call(
        functools.partial(_dispatch_kernel, n_exp=n_exp),
        grid_spec=pltpu.PrefetchScalarGridSpec(
            num_scalar_prefetch=1,
            grid=(nb,),
            in_specs=[pl.BlockSpec((1, 1, tb2), lambda i, pads_: (i, 0, 0), memory_space=pltpu.SMEM),
                      pl.BlockSpec(memory_space=pl.ANY)],
            out_specs=pl.BlockSpec(memory_space=pl.ANY),
            scratch_shapes=[pltpu.VMEM((tm, S, L), h.dtype), pltpu.SemaphoreType.DMA((3,))]),
        out_shape=jax.ShapeDtypeStruct((n_rows, S, L), h.dtype),
        compiler_params=_params(1),
        name="dispatch",
    )(pads, pos, h)


def _experts_kernel(te_ref, nu_ref, xs_ref, wg_ref, wu_ref, wd_ref, y_ref):
    tm, S, L = xs_ref.shape
    i = pl.program_id(0)

    @pl.when(i < nu_ref[0])
    def _():
        h = xs_ref[...].reshape(tm, S * L).astype(BF16)
        y_ref[...] = _swiglu(h, wg_ref, wu_ref, wd_ref, (0,)).reshape(tm, S, L)

    @pl.when(i >= nu_ref[0])
    def _():
        y_ref[...] = jnp.zeros_like(y_ref)


def _experts(tile_expert, n_used, xs, wg, wu, wd, tm):
    n_rows, S, L = xs.shape
    E, D, Fe = wg.shape
    used = lambda i, te, nu: jnp.minimum(i, nu[0] - 1)
    return pl.pallas_call(
        _experts_kernel,
        grid_spec=pltpu.PrefetchScalarGridSpec(
            num_scalar_prefetch=2,
            grid=(n_rows // tm,),
            in_specs=[pl.BlockSpec((tm, S, L), lambda i, te, nu: (used(i, te, nu), 0, 0)),
                      pl.BlockSpec((1, D, Fe), lambda i, te, nu: (te[used(i, te, nu)], 0, 0)),
                      pl.BlockSpec((1, D, Fe), lambda i, te, nu: (te[used(i, te, nu)], 0, 0)),
                      pl.BlockSpec((1, Fe, D), lambda i, te, nu: (te[used(i, te, nu)], 0, 0))],
            out_specs=pl.BlockSpec((tm, S, L), lambda i, te, nu: (i, 0, 0))),
        out_shape=jax.ShapeDtypeStruct((n_rows, S, L), F32),
        compiler_params=_params(1),
        name="experts",
    )(tile_expert, n_used, xs, wg, wu, wd)


def _combine_kernel(pcur_ref, pnext_ref, y_ref, x_ref, mod_ref, meta_ref, lng_ref, lnb_ref, o_ref,
                    buf_ref, sems, *, alpha):
    _, tc, D = x_ref.shape
    step = pl.program_id(0) * pl.num_programs(1) + pl.program_id(1)
    n_steps = pl.num_programs(0) * pl.num_programs(1)
    slot = step % 2

    def issue(p_ref, s):
        def body(t, c):
            _row_copy(y_ref, p_ref[0, 0, t], buf_ref.at[s], t, sems.at[s]).start()
            _row_copy(y_ref, p_ref[0, 0, tc + t], buf_ref.at[s], tc + t, sems.at[s]).start()
            return c
        lax.fori_loop(0, tc, body, 0, unroll=8)

    @pl.when(step == 0)
    def _():
        issue(pcur_ref, 0)

    @pl.when(step + 1 < n_steps)
    def _():
        issue(pnext_ref, 1 - slot)

    pltpu.make_async_copy(y_ref.at[pl.ds(0, 2 * tc)], buf_ref.at[slot], sems.at[slot]).wait()
    y1 = buf_ref[slot, 0:tc].reshape(tc, D)
    y2 = buf_ref[slot, tc:2 * tc].reshape(tc, D)
    f = meta_ref[:, META_G1:META_G1 + 1] * y1 + meta_ref[:, META_G2:META_G2 + 1] * y2
    v = alpha * x_ref[0] + mod_ref[0, 5:6, :] * f
    o_ref[0] = _layer_norm(v, lng_ref[...], lnb_ref[...])


def _combine(pos, y, x, mod, mod_row0, meta, lng, lnb, alpha):
    B, T, D = x.shape
    nb, _, tc2 = pos.shape
    tc = tc2 // 2
    nT = T // tc
    S, L = y.shape[1:]
    blk = pl.BlockSpec((1, tc, D), lambda b_, i: (b_, i, 0))
    return pl.pallas_call(
        functools.partial(_combine_kernel, alpha=alpha),
        grid=(B, nT),
        in_specs=[pl.BlockSpec((1, 1, tc2), lambda b_, i: (b_ * nT + i, 0, 0), memory_space=pltpu.SMEM),
                  pl.BlockSpec((1, 1, tc2), lambda b_, i: (jnp.minimum(b_ * nT + i + 1, nb - 1), 0, 0),
                               memory_space=pltpu.SMEM),
                  pl.BlockSpec(memory_space=pl.ANY),
                  blk,
                  pl.BlockSpec((1, 6, D), lambda b_, i: (b_ + mod_row0, 0, 0)),
                  pl.BlockSpec((tc, LANES), lambda b_, i: (b_ * nT + i, 0)),
                  _const_spec((1, D)), _const_spec((1, D))],
        out_specs=blk,
        out_shape=jax.ShapeDtypeStruct((B, T, D), F32),
        scratch_shapes=[pltpu.VMEM((2, tc2, S, L), F32), pltpu.SemaphoreType.DMA((2,))],
        compiler_params=_params(2),
        name="combine",
    )(pos, pos, y, x, mod, meta, lng.reshape(1, D), lnb.reshape(1, D))


def _moe_sparse(x, mod, mod_row0, wr, br, wg, wu, wd, lng, lnb, alpha):
    B, T, D = x.shape
    E = wg.shape[0]
    N = B * T
    tb, tm = ROUTE_ROWS, EXPERT_ROWS
    assert T % tb == 0
    h, meta, cnt = _route(x, mod, mod_row0, wr, br, tb)

    counts = cnt[0, :E].astype(jnp.int32)
    padded = (counts + tm - 1) // tm * tm
    ends = jnp.cumsum(padded)
    offs = ends - padded
    col = lambda k: meta[:, k].astype(jnp.int32)
    pos1 = offs[col(META_I1)] + col(META_R1)
    pos2 = offs[col(META_I2)] + col(META_R2)
    pos = jnp.concatenate([pos1.reshape(N // tb, tb), pos2.reshape(N // tb, tb)], axis=1)
    pos = pos.reshape(N // tb, 1, 2 * tb)
    n_rows = 2 * N + E * tm
    n_tiles = n_rows // tm
    n_used = (ends[-1:] // tm).astype(jnp.int32)
    tile_expert = jnp.sum(jnp.arange(n_tiles)[:, None] * tm >= ends[None, :], axis=1)
    tile_expert = jnp.minimum(tile_expert, E - 1).astype(jnp.int32)
    pads = jnp.stack([jnp.concatenate([offs + counts, n_used]),
                      jnp.concatenate([ends, jnp.full((1,), n_tiles)])]).astype(jnp.int32)

    xs = _dispatch(pads, pos, h, n_rows, E, tm)
    y = _experts(tile_expert, n_used, xs, wg, wu, wd, tm)
    return _combine(pos, y, x, mod, mod_row0, meta, lng, lnb, alpha)


def _kv_kernel(x_ref, mod_ref, wk_ref, wv_ref, kb_ref, vb_ref, kf_ref, vf_ref):
    bb, tt, D = x_ref.shape
    i = pl.program_id(1)
    shift = mod_ref[:, 0:1, :]
    scale = mod_ref[:, 1:2, :]
    h = (x_ref[...] * (1 + scale) + shift).astype(BF16).reshape(bb * tt, D)
    k = jnp.dot(h, wk_ref[...], preferred_element_type=F32).reshape(bb, tt, D)
    v = jnp.dot(h, wv_ref[...], preferred_element_type=F32).reshape(bb, tt, D)
    kb_ref[...] = k.astype(BF16)
    vb_ref[...] = v.astype(BF16)

    @pl.when(i == pl.num_programs(1) - 1)
    def _():
        kf_ref[...] = k
        vf_ref[...] = v


def _kv(x, mod, mod_row0, wk, wv, bb, tt):
    B, T, D = x.shape
    mb = mod_row0 // bb
    blk = pl.BlockSpec((bb, tt, D), lambda b_, i: (b_, i, 0))
    last = pl.BlockSpec((bb, tt, D), lambda b_, i: (b_, 0, 0))
    return pl.pallas_call(
        _kv_kernel,
        grid=(B // bb, T // tt),
        in_specs=[blk,
                  pl.BlockSpec((bb, 2, D), lambda b_, i: (b_ + mb, 0, 0)),
                  _const_spec((D, D)), _const_spec((D, D))],
        out_specs=[blk, blk, last, last],
        out_shape=[jax.ShapeDtypeStruct((B, T, D), BF16), jax.ShapeDtypeStruct((B, T, D), BF16),
                   jax.ShapeDtypeStruct((B, tt, D), F32), jax.ShapeDtypeStruct((B, tt, D), F32)],
        compiler_params=_params(2),
        name="kv",
    )(x, mod, wk, wv)


ATTN_ROWS = 256


def _attn_kernel(x_ref, mod_ref, kp_ref, kc_ref, vp_ref, vc_ref, bias_ref, wq_ref, wo_ref,
                 lng_ref, lnb_ref, o_ref, kcat_ref, vcat_ref, q_ref, a_ref,
                 *, alpha, chunk, past, pad, mask_first, q_scale):
    _, tt, D = x_ref.shape
    n_chunks = tt // chunk
    win = pad + past + chunk
    head_dim = LANES // 2
    i = pl.program_id(1)

    if pad:
        kcat_ref[0:pad, :] = jnp.zeros((pad, D), BF16)
        vcat_ref[0:pad, :] = jnp.zeros((pad, D), BF16)
    kcat_ref[pad:pad + past, :] = kp_ref[0]
    kcat_ref[pad + past:, :] = kc_ref[0]
    vcat_ref[pad:pad + past, :] = vp_ref[0]
    vcat_ref[pad + past:, :] = vc_ref[0]

    x = x_ref[0]
    h = (x * (1 + mod_ref[0, 1:2, :]) + mod_ref[0, 0:1, :]).astype(BF16)
    q = jnp.dot(h, wq_ref[...], preferred_element_type=F32) * q_scale
    q_ref[...] = q.astype(BF16)

    row_head = lax.broadcasted_iota(jnp.int32, (2 * chunk, LANES), 0) >= chunk
    lane_head = lax.broadcasted_iota(jnp.int32, (2 * chunk, LANES), 1) >= head_dim
    own = row_head == lane_head
    out_lane_head0 = lax.broadcasted_iota(jnp.int32, (chunk, LANES), 1) < head_dim
    key_col = lax.broadcasted_iota(jnp.int32, (1, win), 1)

    def chunk_loop(masked):
        def one_chunk(g, carry):
            r0 = pl.multiple_of(g * chunk, chunk)
            for pr in range(D // LANES):
                lanes = slice(pr * LANES, (pr + 1) * LANES)
                qp = q_ref[pl.ds(r0, chunk), lanes]
                qbd = jnp.where(own, jnp.concatenate([qp, qp], axis=0), jnp.zeros((), BF16))
                kw = kcat_ref[pl.ds(r0, win), lanes]
                s = lax.dot_general(qbd, kw, (((1,), (1,)), ((), ())), preferred_element_type=F32)
                s = s + bias_ref[pr]
                if masked:
                    s = jnp.where(key_col + r0 >= pad + past, s, NEG_INF)
                m = jnp.max(s, axis=-1, keepdims=True)
                ex = jnp.exp(s - m)
                l = jnp.sum(ex, axis=-1, keepdims=True)
                p = (ex * (1.0 / l)).astype(BF16)
                vw = vcat_ref[pl.ds(r0, win), lanes]
                r = jnp.dot(p, vw, preferred_element_type=F32)
                a_ref[pl.ds(r0, chunk), lanes] = jnp.where(out_lane_head0, r[:chunk], r[chunk:]).astype(BF16)
            return carry
        lax.fori_loop(0, n_chunks, one_chunk, 0)

    if mask_first:
        @pl.when(i == 0)
        def _():
            chunk_loop(True)

        @pl.when(i > 0)
        def _():
            chunk_loop(False)
    else:
        chunk_loop(False)

    out = jnp.dot(a_ref[...], wo_ref[...], preferred_element_type=F32)
    v = alpha * x + mod_ref[0, 2:3, :] * out
    o_ref[0] = _layer_norm(v, lng_ref[...], lnb_ref[...])


def _pair_bias(table, past, chunk, pad):
    H = table.shape[0]
    win = pad + past + chunk
    L = win + chunk
    m = jnp.arange(L) - (chunk - 1)
    d = past + pad - m
    u = table[:, jnp.clip(d, -REL_CLIP, REL_CLIP) + REL_CLIP]
    skew = jnp.broadcast_to(u[:, None, :], (H, chunk, L)).reshape(H, chunk * L)
    skew = skew[:, :chunk * (L - 1)].reshape(H, chunk, L - 1)
    bias = skew[:, :, chunk - 1:chunk - 1 + win]
    key = jnp.arange(win)[None, :] - (pad + past)
    start = jnp.arange(chunk)[:, None] // CHUNK * CHUNK
    visible = jnp.logical_and(key >= start - past, key < start + CHUNK)
    bias = jnp.where(visible, bias, NEG_INF)
    return bias.reshape(H // 2, 2 * chunk, win)


def _attn(x, mod, mod_row0, k_prev, k_cur, v_prev, v_cur, table, wq, wo, lng, lnb,
          tt, chunk, mask_first, alpha):
    B, T, D = x.shape
    past = WINDOW
    assert past % tt == 0 or tt % past == 0
    assert tt == past or T == tt
    pad = -(past + chunk) % LANES
    bias = _pair_bias(table, past, chunk, pad)
    npairs = D // LANES
    head_dim = D // N_HEADS
    blk = pl.BlockSpec((1, tt, D), lambda b_, i: (b_, i, 0))
    prev = pl.BlockSpec((1, past, D), lambda b_, i: (b_, jnp.maximum(i - 1, 0), 0))
    return pl.pallas_call(
        functools.partial(_attn_kernel, alpha=alpha, chunk=chunk, past=past, pad=pad,
                          mask_first=mask_first, q_scale=head_dim ** -0.5),
        grid=(B, T // tt),
        in_specs=[blk,
                  pl.BlockSpec((1, 6, D), lambda b_, i: (b_ + mod_row0, 0, 0)),
                  prev, blk, prev, blk,
                  _const_spec((npairs, 2 * chunk, pad + past + chunk)),
                  _const_spec((D, D)), _const_spec((D, D)),
                  _const_spec((1, D)), _const_spec((1, D))],
        out_specs=blk,
        out_shape=jax.ShapeDtypeStruct((B, T, D), F32),
        scratch_shapes=[pltpu.VMEM((pad + past + tt, D), BF16), pltpu.VMEM((pad + past + tt, D), BF16),
                        pltpu.VMEM((tt, D), BF16), pltpu.VMEM((tt, D), BF16)],
        compiler_params=_params(2),
        name="attn",
    )(x, mod, k_prev, k_cur, v_prev, v_cur, bias, wq, wo, lng.reshape(1, D), lnb.reshape(1, D))


def _trunk(x, mods, mods_kv, mod_row0, conv_state, kv_cache, p, bb, tt, chunk):
    B, T, D = x.shape
    depth = p['w_ada'].shape[0]
    n_a = p['w_pw1'].shape[0]
    alpha = (2.0 * depth) ** 0.25
    assert T >= CONV_WIDTH - 1 and T % tt == 0 and B % bb == 0 and mod_row0 % bb == 0
    new_conv = []
    kb = vb = kf = vf = None
    for l in range(depth):
        mod = mods[l]
        if l < n_a:
            u = _pw1(x, mod, mod_row0, p['w_pw1'][l], p['b_pw1'][l], bb, tt)
            x = _conv(x, u, conv_state[l], mod, mod_row0, p['w_dw'][l], p['b_dw'][l],
                      p['conv_ln_g'][l], p['conv_ln_b'][l], p['w_pw2'][l], p['b_pw2'][l],
                      p['ln_g'][l, 0], p['ln_b'][l, 0], bb, tt, alpha)
            new_conv.append(u[:, T - (CONV_WIDTH - 1):])
        else:
            j = l - n_a
            if kv_cache is None:
                x = _attn(x, mod, mod_row0, kb, kb, vb, vb, p['rel_bias'][j], p['w_q'][j], p['w_o'][j],
                          p['ln_g'][l, 0], p['ln_b'][l, 0], WINDOW, chunk, True, alpha)
            else:
                x = _attn(x, mod, mod_row0, kv_cache[0], kb, kv_cache[1], vb, p['rel_bias'][j],
                          p['w_q'][j], p['w_o'][j], p['ln_g'][l, 0], p['ln_b'][l, 0],
                          T, chunk, False, alpha)
        i = l // 2
        if l % 2 == 0:
            x = _ffn(x, mod, mod_row0, p['w_ff_gate'][i], p['w_ff_up'][i], p['w_ff_down'][i],
                     p['ln_g'][l, 1], p['ln_b'][l, 1], bb, tt, alpha)
        elif bb == 1 and T % ROUTE_ROWS == 0:
            x = _moe_sparse(x, mod, mod_row0, p['w_router'][i], p['b_router'][i], p['w_e_gate'][i],
                            p['w_e_up'][i], p['w_e_down'][i], p['ln_g'][l, 1], p['ln_b'][l, 1], alpha)
        else:
            x = _moe(x, mod, mod_row0, p['w_router'][i], p['b_router'][i], p['w_e_gate'][i],
                     p['w_e_up'][i], p['w_e_down'][i], p['ln_g'][l, 1], p['ln_b'][l, 1],
                     bb, tt, alpha)
        if l == n_a - 1:
            kb, vb, kf, vf = _kv(x, mods_kv, mod_row0, p['w_k'], p['w_v'], bb, tt)
    return x, jnp.stack(new_conv), kf, vf


def kernel(x_prompt, x_sample, c_prompt, c_sample, state_conv, cache_k, cache_v, w_ada, b_ada, ln_g, ln_b, w_pw1, b_pw1, w_dw, b_dw, conv_ln_g, conv_ln_b, w_pw2, b_pw2, w_ada_kv, b_ada_kv, w_k, w_v, w_q, w_o, rel_bias, w_ff_gate, w_ff_up, w_ff_down, w_router, b_router, w_e_gate, w_e_up, w_e_down):
    B, T, D = x_prompt.shape
    Bs, Ts, _ = x_sample.shape
    H = N_HEADS
    assert Bs + B <= MOD_ROWS and D == H * (LANES // 2)
    depth = w_ada.shape[0]
    n_a = w_pw1.shape[0]

    bf = lambda w: w.astype(BF16)
    p = dict(w_ada=w_ada, ln_g=ln_g, ln_b=ln_b, w_pw1=bf(w_pw1), b_pw1=b_pw1, w_dw=w_dw, b_dw=b_dw,
             conv_ln_g=conv_ln_g, conv_ln_b=conv_ln_b, w_pw2=bf(w_pw2), b_pw2=b_pw2,
             w_k=bf(w_k), w_v=bf(w_v), w_q=bf(w_q), w_o=bf(w_o), rel_bias=rel_bias,
             w_ff_gate=bf(w_ff_gate), w_ff_up=bf(w_ff_up), w_ff_down=bf(w_ff_down),
             w_router=w_router, b_router=b_router,
             w_e_gate=bf(w_e_gate), w_e_up=bf(w_e_up), w_e_down=bf(w_e_down))

    c_all = jnp.concatenate([c_sample, c_prompt, jnp.zeros((MOD_ROWS - Bs - B, D), F32)], axis=0)
    mods = _ada(c_all, w_ada, b_ada, 1536).reshape(depth, MOD_ROWS, 6, D)
    mods_kv = _ada(c_all, w_ada_kv[None], b_ada_kv[None], 1024).reshape(MOD_ROWS, 2, D)

    pad_state = lambda s: jnp.pad(s, ((0, 0), (0, 0), (HALO - (CONV_WIDTH - 1), 0), (0, 0)))
    zero_state = jnp.zeros((n_a, B, HALO, D), F32)
    y_p, conv_p, kf_p, vf_p = _trunk(x_prompt, mods, mods_kv, Bs, zero_state, None, p, 1, WINDOW,
                                     ATTN_ROWS)

    cache = (cache_k.reshape(Bs, -1, D).astype(BF16), cache_v.reshape(Bs, -1, D).astype(BF16))
    y_s, conv_s, kf_s, vf_s = _trunk(x_sample, mods, mods_kv, 0, pad_state(state_conv), cache, p,
                                     Bs, Ts, Ts)

    L = cache_k.shape[1]
    heads = lambda a: a.reshape(a.shape[0], a.shape[1], H, D // H)
    k_s = jnp.concatenate([cache_k, heads(kf_s)], axis=1)[:, -L:]
    v_s = jnp.concatenate([cache_v, heads(vf_s)], axis=1)[:, -L:]
    return (y_p, y_s, conv_p, heads(kf_p), heads(vf_p), conv_s, k_s, v_s)
```

```python
import functools

import jax
import jax.numpy as jnp
from jax import lax
from jax.experimental import pallas as pl
from jax.experimental.pallas import tpu as pltpu

CHUNK = 64
WINDOW = 512
REL_CLIP = 128
CONV_WIDTH = 31
N_HEADS = 16
LN_EPS = 1e-5
NEG_INF = -1e30

LANES = 128
HALO = 32
MOD_ROWS = 16
VMEM_LIMIT = 56 * 1024 * 1024

F32 = jnp.float32
BF16 = jnp.bfloat16


def _layer_norm(v, g, b):
    mu = jnp.mean(v, axis=-1, keepdims=True)
    d = v - mu
    var = jnp.mean(d * d, axis=-1, keepdims=True)
    return d * lax.rsqrt(var + LN_EPS) * g + b


def _silu(v):
    return v * jax.nn.sigmoid(v)


def _const_spec(shape):
    nd = len(shape)
    return pl.BlockSpec(shape, lambda *_: (0,) * nd, pipeline_mode=pl.Buffered(1))


def _params(n_grid):
    return pltpu.CompilerParams(dimension_semantics=("arbitrary",) * n_grid,
                                vmem_limit_bytes=VMEM_LIMIT)


def _ada_kernel(c_ref, w_ref, b_ref, o_ref):
    a = _silu(c_ref[...]).astype(BF16)
    o_ref[0] = jnp.dot(a, w_ref[0].astype(BF16), preferred_element_type=F32) + b_ref[0]


def _ada(c, w, b, tn):
    L, D, N = w.shape
    return pl.pallas_call(
        _ada_kernel,
        grid=(L, N // tn),
        in_specs=[pl.BlockSpec((MOD_ROWS, D), lambda l, j: (0, 0)),
                  pl.BlockSpec((1, D, tn), lambda l, j: (l, 0, j)),
                  pl.BlockSpec((1, 1, tn), lambda l, j: (l, 0, j))],
        out_specs=pl.BlockSpec((1, MOD_ROWS, tn), lambda l, j: (l, 0, j)),
        out_shape=jax.ShapeDtypeStruct((L, MOD_ROWS, N), F32),
        compiler_params=_params(2),
        name="ada",
    )(c, w, b.reshape(L, 1, N))


def _pw1_kernel(x_ref, mod_ref, w_ref, b_ref, u_ref):
    bb, tt, D = x_ref.shape
    shift = mod_ref[:, 0:1, :]
    scale = mod_ref[:, 1:2, :]
    h = (x_ref[...] * (1 + scale) + shift).astype(BF16).reshape(bb * tt, D)
    z = jnp.dot(h, w_ref[...], preferred_element_type=F32) + b_ref[...]
    u = z[:, :D] * jax.nn.sigmoid(z[:, D:])
    u_ref[...] = u.reshape(bb, tt, D)


def _pw1(x, mod, mod_row0, w, b, bb, tt):
    B, T, D = x.shape
    mb = mod_row0 // bb
    return pl.pallas_call(
        _pw1_kernel,
        grid=(B // bb, T // tt),
        in_specs=[pl.BlockSpec((bb, tt, D), lambda b_, i: (b_, i, 0)),
                  pl.BlockSpec((bb, 6, D), lambda b_, i: (b_ + mb, 0, 0)),
                  _const_spec((D, 2 * D)),
                  _const_spec((1, 2 * D))],
        out_specs=pl.BlockSpec((bb, tt, D), lambda b_, i: (b_, i, 0)),
        out_shape=jax.ShapeDtypeStruct((B, T, D), F32),
        compiler_params=_params(2),
        name="pw1",
    )(x, mod, w, b.reshape(1, 2 * D))


CONV_ROWS = 64
CONV_LANES = 256
SUBLANES = 8


def _conv_kernel(x_ref, u_ref, halo_ref, st_ref, mod_ref, wdw_ref, bdw_ref, cg_ref, cb_ref,
                 w2_ref, b2_ref, lng_ref, lnb_ref, o_ref, ext_ref, sh_ref, y_ref, *, alpha):
    bb, tt, D = x_ref.shape
    i = pl.program_id(1)

    @pl.when(i == 0)
    def _():
        ext_ref[:, 0:HALO, :] = st_ref[...]

    @pl.when(i > 0)
    def _():
        ext_ref[:, 0:HALO, :] = halo_ref[...]

    ext_ref[:, HALO:, :] = u_ref[...]

    first = HALO - (CONV_WIDTH - 1)
    rows = min(CONV_ROWS, tt)
    for b in range(bb):
        for lb in range(D // CONV_LANES):
            lanes = slice(lb * CONV_LANES, (lb + 1) * CONV_LANES)
            sh_ref[0] = ext_ref[b, :, lanes]
            for s in range(1, SUBLANES):
                sh_ref[s, 0:tt + HALO - SUBLANES, :] = ext_ref[b, s:s + tt + HALO - SUBLANES, lanes]

            def chunk(c, carry, b=b, lanes=lanes):
                r0 = pl.multiple_of(c * rows, rows)
                acc = jnp.zeros((rows, CONV_LANES), F32)
                for j in range(CONV_WIDTH):
                    a, s = divmod(j + first, SUBLANES)
                    acc = acc + wdw_ref[j:j + 1, lanes] * sh_ref[s, pl.ds(r0 + SUBLANES * a, rows), :]
                y_ref[b, pl.ds(r0, rows), lanes] = acc + bdw_ref[:, lanes]
                return carry
            lax.fori_loop(0, tt // rows, chunk, 0)

    yn = _silu(_layer_norm(y_ref[...], cg_ref[...], cb_ref[...]))
    out = jnp.dot(yn.astype(BF16).reshape(bb * tt, D), w2_ref[...],
                  preferred_element_type=F32) + b2_ref[...]
    gate = mod_ref[:, 2:3, :]
    v = alpha * x_ref[...] + gate * out.reshape(bb, tt, D)
    o_ref[...] = _layer_norm(v, lng_ref[...], lnb_ref[...])


def _conv(x, u, state, mod, mod_row0, w_dw, b_dw, cg, cb, w2, b2, lng, lnb, bb, tt, alpha):
    B, T, D = x.shape
    mb = mod_row0 // bb
    hb = tt // HALO
    wdw = jnp.pad(w_dw, ((0, HALO - CONV_WIDTH), (0, 0)))
    row = lambda a: a.reshape(1, D)
    return pl.pallas_call(
        functools.partial(_conv_kernel, alpha=alpha),
        grid=(B // bb, T // tt),
        in_specs=[pl.BlockSpec((bb, tt, D), lambda b_, i: (b_, i, 0)),
                  pl.BlockSpec((bb, tt, D), lambda b_, i: (b_, i, 0)),
                  pl.BlockSpec((bb, HALO, D), lambda b_, i: (b_, jnp.maximum(i * hb - 1, 0), 0)),
                  pl.BlockSpec((bb, HALO, D), lambda b_, i: (b_, 0, 0)),
                  pl.BlockSpec((bb, 6, D), lambda b_, i: (b_ + mb, 0, 0)),
                  _const_spec((HALO, D)),
                  _const_spec((1, D)), _const_spec((1, D)), _const_spec((1, D)),
                  _const_spec((D, D)),
                  _const_spec((1, D)), _const_spec((1, D)), _const_spec((1, D))],
        out_specs=pl.BlockSpec((bb, tt, D), lambda b_, i: (b_, i, 0)),
        out_shape=jax.ShapeDtypeStruct((B, T, D), F32),
        scratch_shapes=[pltpu.VMEM((bb, HALO + tt, D), F32),
                        pltpu.VMEM((SUBLANES, HALO + tt, CONV_LANES), F32),
                        pltpu.VMEM((bb, tt, D), F32)],
        compiler_params=_params(2),
        name="conv",
    )(x, u, u, state, mod, wdw, row(b_dw), row(cg), row(cb), w2, row(b2), row(lng), row(lnb))


def _ff_chunks(F, width):
    return [(f0, min(width, F - f0)) for f0 in range(0, F, width)]


def _swiglu(h, wg_ref, wu_ref, wd_ref, lead):
    F = wg_ref.shape[-1]
    acc = None
    for f0, fw in _ff_chunks(F, 1024):
        g = jnp.dot(h, wg_ref[lead + (slice(None), slice(f0, f0 + fw))], preferred_element_type=F32)
        u = jnp.dot(h, wu_ref[lead + (slice(None), slice(f0, f0 + fw))], preferred_element_type=F32)
        a = (_silu(g) * u).astype(BF16)
        d = jnp.dot(a, wd_ref[lead + (slice(f0, f0 + fw), slice(None))], preferred_element_type=F32)
        acc = d if acc is None else acc + d
    return acc


def _ffn_kernel(x_ref, mod_ref, wg_ref, wu_ref, wd_ref, lng_ref, lnb_ref, o_ref, *, alpha):
    bb, tt, D = x_ref.shape
    shift = mod_ref[:, 3:4, :]
    scale = mod_ref[:, 4:5, :]
    gate = mod_ref[:, 5:6, :]
    x = x_ref[...]
    h = (x * (1 + scale) + shift).astype(BF16).reshape(bb * tt, D)
    f = _swiglu(h, wg_ref, wu_ref, wd_ref, ())
    v = alpha * x + gate * f.reshape(bb, tt, D)
    o_ref[...] = _layer_norm(v, lng_ref[...], lnb_ref[...])


def _ffn(x, mod, mod_row0, wg, wu, wd, lng, lnb, bb, tt, alpha):
    B, T, D = x.shape
    F = wg.shape[-1]
    mb = mod_row0 // bb
    return pl.pallas_call(
        functools.partial(_ffn_kernel, alpha=alpha),
        grid=(B // bb, T // tt),
        in_specs=[pl.BlockSpec((bb, tt, D), lambda b_, i: (b_, i, 0)),
                  pl.BlockSpec((bb, 6, D), lambda b_, i: (b_ + mb, 0, 0)),
                  _const_spec((D, F)), _const_spec((D, F)), _const_spec((F, D)),
                  _const_spec((1, D)), _const_spec((1, D))],
        out_specs=pl.BlockSpec((bb, tt, D), lambda b_, i: (b_, i, 0)),
        out_shape=jax.ShapeDtypeStruct((B, T, D), F32),
        compiler_params=_params(2),
        name="ffn",
    )(x, mod, wg, wu, wd, lng.reshape(1, D), lnb.reshape(1, D))


def _router_top2(h32, wr_ref, br_ref, n_exp):
    M, D = h32.shape
    rows = []
    for e in range(n_exp):
        prod = h32 * wr_ref[e:e + 1, :]
        part = prod[:, 0:LANES]
        for c in range(1, D // LANES):
            part = part + prod[:, c * LANES:(c + 1) * LANES]
        rows.append(jnp.sum(part.T, axis=0, keepdims=True))
    logits = jnp.concatenate(rows, axis=0) + br_ref[...]
    ex = jnp.exp(logits - jnp.max(logits, axis=0, keepdims=True))
    p = ex / jnp.sum(ex, axis=0, keepdims=True)
    eid = lax.broadcasted_iota(jnp.int32, p.shape, 0)

    def best(v):
        top = jnp.max(v, axis=0, keepdims=True)
        return top, jnp.min(jnp.where(v == top, eid, n_exp), axis=0, keepdims=True)

    p1, i1 = best(p)
    p2, i2 = best(jnp.where(eid == i1, -1.0, p))
    tot = p1 + p2
    return i1, i2, p1 / tot, p2 / tot


def _to_columns(rows, M):
    pad = jnp.zeros((LANES - len(rows), M), F32)
    return jnp.concatenate(list(rows) + [pad], axis=0).T


def _router_gates(h32, wr_ref, br_ref, n_exp):
    i1, i2, g1, g2 = _router_top2(h32, wr_ref, br_ref, n_exp)
    rows = [jnp.where(i1 == e, g1, 0.0) + jnp.where(i2 == e, g2, 0.0) for e in range(n_exp)]
    return _to_columns(rows, h32.shape[0])


def _moe_kernel(x_ref, mod_ref, wr_ref, br_ref, wg_ref, wu_ref, wd_ref, lng_ref, lnb_ref, o_ref,
                h_ref, gate_ref, acc_ref, *, alpha, n_exp):
    bb, tt, D = x_ref.shape
    e = pl.program_id(2)

    @pl.when(e == 0)
    def _():
        shift = mod_ref[:, 3:4, :]
        scale = mod_ref[:, 4:5, :]
        h32 = (x_ref[...] * (1 + scale) + shift).reshape(bb * tt, D)
        h_ref[...] = h32.astype(BF16)
        gates = _router_gates(h32, wr_ref, br_ref, n_exp)
        for k in range(n_exp):
            gate_ref[k] = gates[:, k:k + 1]
        acc_ref[...] = jnp.zeros_like(acc_ref)

    f = _swiglu(h_ref[...], wg_ref, wu_ref, wd_ref, (0,))
    acc_ref[...] += gate_ref[e] * f

    @pl.when(e == n_exp - 1)
    def _():
        v = alpha * x_ref[...] + mod_ref[:, 5:6, :] * acc_ref[...].reshape(bb, tt, D)
        o_ref[...] = _layer_norm(v, lng_ref[...], lnb_ref[...])


def _moe(x, mod, mod_row0, wr, br, wg, wu, wd, lng, lnb, bb, tt, alpha):
    B, T, D = x.shape
    E, _, Fe = wg.shape
    mb = mod_row0 // bb
    M = bb * tt
    return pl.pallas_call(
        functools.partial(_moe_kernel, alpha=alpha, n_exp=E),
        grid=(B // bb, T // tt, E),
        in_specs=[pl.BlockSpec((bb, tt, D), lambda b_, i, e: (b_, i, 0)),
                  pl.BlockSpec((bb, 6, D), lambda b_, i, e: (b_ + mb, 0, 0)),
                  _const_spec((E, D)), _const_spec((E, 1)),
                  pl.BlockSpec((1, D, Fe), lambda b_, i, e: (e, 0, 0)),
                  pl.BlockSpec((1, D, Fe), lambda b_, i, e: (e, 0, 0)),
                  pl.BlockSpec((1, Fe, D), lambda b_, i, e: (e, 0, 0)),
                  _const_spec((1, D)), _const_spec((1, D))],
        out_specs=pl.BlockSpec((bb, tt, D), lambda b_, i, e: (b_, i, 0)),
        out_shape=jax.ShapeDtypeStruct((B, T, D), F32),
        scratch_shapes=[pltpu.VMEM((M, D), BF16), pltpu.VMEM((E, M, 1), F32),
                        pltpu.VMEM((M, D), F32)],
        compiler_params=_params(3),
        name="moe",
    )(x, mod, wr.T, br.reshape(E, 1), wg, wu, wd, lng.reshape(1, D), lnb.reshape(1, D))


ROUTE_ROWS = 512
EXPERT_ROWS = 512
META_I1, META_I2, META_G1, META_G2, META_R1, META_R2 = range(6)


def _route_kernel(x_ref, mod_ref, wr_ref, br_ref, meta_ref, cnt_ref, *, n_exp):
    _, tt, D = x_ref.shape
    first = jnp.logical_and(pl.program_id(0) == 0, pl.program_id(1) == 0)

    @pl.when(first)
    def _():
        cnt_ref[...] = jnp.zeros_like(cnt_ref)

    h32 = x_ref[0] * (1 + mod_ref[0, 4:5, :]) + mod_ref[0, 3:4, :]
    i1, i2, g1, g2 = _router_top2(h32, wr_ref, br_ref, n_exp)
    rec = _to_columns([i1.astype(F32), i2.astype(F32), g1, g2], tt)

    lane = lax.broadcasted_iota(jnp.int32, (tt, LANES), 1)
    o1 = (lane == rec[:, META_I1:META_I1 + 1].astype(jnp.int32)).astype(F32)
    o2 = (lane == rec[:, META_I2:META_I2 + 1].astype(jnp.int32)).astype(F32)
    lower = (lax.broadcasted_iota(jnp.int32, (tt, tt), 1) <
             lax.broadcasted_iota(jnp.int32, (tt, tt), 0)).astype(BF16)
    c1 = jnp.dot(lower, o1.astype(BF16), preferred_element_type=F32)
    c2 = jnp.dot(lower, o2.astype(BF16), preferred_element_type=F32)
    base = cnt_ref[0:1, :]
    tot1 = jnp.sum(o1, axis=0, keepdims=True)
    tot2 = jnp.sum(o2, axis=0, keepdims=True)
    r1 = jnp.sum(o1 * (c1 + base), axis=-1, keepdims=True)
    r2 = jnp.sum(o2 * (c2 + (base + tot1)), axis=-1, keepdims=True)
    cnt_ref[0:1, :] = base + tot1 + tot2

    meta_ref[...] = jnp.where(lane == META_R1, r1, jnp.where(lane == META_R2, r2, rec))


def _route(x, mod, mod_row0, wr, br, tt):
    B, T, D = x.shape
    E = wr.shape[1]
    nT = T // tt
    return pl.pallas_call(
        functools.partial(_route_kernel, n_exp=E),
        grid=(B, nT),
        in_specs=[pl.BlockSpec((1, tt, D), lambda b_, i: (b_, i, 0)),
                  pl.BlockSpec((1, 6, D), lambda b_, i: (b_ + mod_row0, 0, 0)),
                  _const_spec((E, D)), _const_spec((E, 1))],
        out_specs=[pl.BlockSpec((tt, LANES), lambda b_, i: (b_ * nT + i, 0)),
                   pl.BlockSpec((SUBLANES, LANES), lambda b_, i: (0, 0))],
        out_shape=[jax.ShapeDtypeStruct((B * T, LANES), F32),
                   jax.ShapeDtypeStruct((SUBLANES, LANES), F32)],
        compiler_params=_params(2),
        name="route",
    )(x, mod, wr.T, br.reshape(E, 1))


def _row_copy(src_ref, src_row, dst_ref, dst_row, sem):
    return pltpu.make_async_copy(src_ref.at[pl.ds(src_row, 1)], dst_ref.at[pl.ds(dst_row, 1)], sem)


def _dispatch_kernel(pad_ref, pos_ref, x_ref, mod_ref, xs_ref, rows_ref, zero_ref, sems, *, n_exp):
    _, tb, D = x_ref.shape
    tm = zero_ref.shape[0]
    i = pl.program_id(0) * pl.num_programs(1) + pl.program_id(1)
    last = pl.num_programs(0) * pl.num_programs(1) - 1
    slot = i % 2
    pad_sem = sems.at[2]

    def tail_copy(t):
        return pltpu.make_async_copy(zero_ref, xs_ref.at[pl.ds(pl.multiple_of(t * tm, tm), tm)], pad_sem)

    @pl.when(i == 0)
    def _():
        zero_ref[...] = jnp.zeros_like(zero_ref)
        for e in range(n_exp):
            def fill(r, c):
                _row_copy(zero_ref, 0, xs_ref, r, pad_sem).start()
                return c
            lax.fori_loop(pad_ref[0, e], pad_ref[1, e], fill, 0)

        def fill_tail(t, c):
            tail_copy(t).start()
            return c
        lax.fori_loop(pad_ref[0, n_exp], pad_ref[1, n_exp], fill_tail, 0)

    h32 = x_ref[0] * (1 + mod_ref[0, 4:5, :]) + mod_ref[0, 3:4, :]
    rows_ref[slot] = h32.reshape(tb, D // LANES, LANES)

    def issue(t, c):
        _row_copy(rows_ref.at[slot], t, xs_ref, pos_ref[0, 0, t], sems.at[slot]).start()
        _row_copy(rows_ref.at[slot], t, xs_ref, pos_ref[0, 0, tb + t], sems.at[slot]).start()
        return c
    lax.fori_loop(0, tb, issue, 0, unroll=8)

    def wait_step(s):
        for _ in range(2):
            pltpu.make_async_copy(rows_ref.at[s], xs_ref.at[pl.ds(0, tb)], sems.at[s]).wait()

    @pl.when(i > 0)
    def _():
        wait_step(1 - slot)

    @pl.when(i == last)
    def _():
        wait_step(slot)
        for e in range(n_exp):
            def drain(r, c):
                _row_copy(zero_ref, 0, xs_ref, r, pad_sem).wait()
                return c
            lax.fori_loop(pad_ref[0, e], pad_ref[1, e], drain, 0)

        def drain_tail(t, c):
            tail_copy(t).wait()
            return c
        lax.fori_loop(pad_ref[0, n_exp], pad_ref[1, n_exp], drain_tail, 0)


def _dispatch(pads, pos, x, mod, mod_row0, n_rows, n_exp, tm):
    B, T, D = x.shape
    tb = pos.shape[2] // 2
    nT = T // tb
    S, L = D // LANES, LANES
    return pl.pallas_call(
        functools.partial(_dispatch_kernel, n_exp=n_exp),
        grid_spec=pltpu.PrefetchScalarGridSpec(
            num_scalar_prefetch=1,
            grid=(B, nT),
            in_specs=[pl.BlockSpec((1, 1, 2 * tb), lambda b_, i, pads_: (b_ * nT + i, 0, 0),
                                   memory_space=pltpu.SMEM),
                      pl.BlockSpec((1, tb, D), lambda b_, i, pads_: (b_, i, 0)),
                      pl.BlockSpec((1, 6, D), lambda b_, i, pads_: (b_ + mod_row0, 0, 0))],
            out_specs=pl.BlockSpec(memory_space=pl.ANY),
            scratch_shapes=[pltpu.VMEM((2, tb, S, L), F32), pltpu.VMEM((tm, S, L), F32),
                            pltpu.SemaphoreType.DMA((3,))]),
        out_shape=jax.ShapeDtypeStruct((n_rows, S, L), F32),
        compiler_params=_params(2),
        name="dispatch",
    )(pads, pos, x, mod)


def _experts_kernel(te_ref, nu_ref, xs_ref, wg_ref, wu_ref, wd_ref, y_ref):
    tm, S, L = xs_ref.shape
    i = pl.program_id(0)

    @pl.when(i < nu_ref[0])
    def _():
        h = xs_ref[...].reshape(tm, S * L).astype(BF16)
        y_ref[...] = _swiglu(h, wg_ref, wu_ref, wd_ref, (0,)).reshape(tm, S, L)

    @pl.when(i >= nu_ref[0])
    def _():
        y_ref[...] = jnp.zeros_like(y_ref)


def _experts(tile_expert, n_used, xs, wg, wu, wd, tm):
    n_rows, S, L = xs.shape
    E, D, Fe = wg.shape
    used = lambda i, te, nu: jnp.minimum(i, nu[0] - 1)
    return pl.pallas_call(
        _experts_kernel,
        grid_spec=pltpu.PrefetchScalarGridSpec(
            num_scalar_prefetch=2,
            grid=(n_rows // tm,),
            in_specs=[pl.BlockSpec((tm, S, L), lambda i, te, nu: (used(i, te, nu), 0, 0)),
                      pl.BlockSpec((1, D, Fe), lambda i, te, nu: (te[used(i, te, nu)], 0, 0)),
                      pl.BlockSpec((1, D, Fe), lambda i, te, nu: (te[used(i, te, nu)], 0, 0)),
                      pl.BlockSpec((1, Fe, D), lambda i, te, nu: (te[used(i, te, nu)], 0, 0))],
            out_specs=pl.BlockSpec((tm, S, L), lambda i, te, nu: (i, 0, 0))),
        out_shape=jax.ShapeDtypeStruct((n_rows, S, L), F32),
        compiler_params=_params(1),
        name="experts",
    )(tile_expert, n_used, xs, wg, wu, wd)


def _combine_kernel(pcur_ref, pnext_ref, y_ref, x_ref, mod_ref, meta_ref, lng_ref, lnb_ref, o_ref,
                    buf_ref, sems, *, alpha):
    _, tc, D = x_ref.shape
    step = pl.program_id(0) * pl.num_programs(1) + pl.program_id(1)
    n_steps = pl.num_programs(0) * pl.num_programs(1)
    slot = step % 2

    def issue(p_ref, s):
        def body(t, c):
            _row_copy(y_ref, p_ref[0, 0, t], buf_ref.at[s], t, sems.at[s]).start()
            _row_copy(y_ref, p_ref[0, 0, tc + t], buf_ref.at[s], tc + t, sems.at[s]).start()
            return c
        lax.fori_loop(0, tc, body, 0, unroll=8)

    @pl.when(step == 0)
    def _():
        issue(pcur_ref, 0)

    @pl.when(step + 1 < n_steps)
    def _():
        issue(pnext_ref, 1 - slot)

    pltpu.make_async_copy(y_ref.at[pl.ds(0, 2 * tc)], buf_ref.at[slot], sems.at[slot]).wait()
    y1 = buf_ref[slot, 0:tc].reshape(tc, D)
    y2 = buf_ref[slot, tc:2 * tc].reshape(tc, D)
    f = meta_ref[:, META_G1:META_G1 + 1] * y1 + meta_ref[:, META_G2:META_G2 + 1] * y2
    v = alpha * x_ref[0] + mod_ref[0, 5:6, :] * f
    o_ref[0] = _layer_norm(v, lng_ref[...], lnb_ref[...])


def _combine(pos, y, x, mod, mod_row0, meta, lng, lnb, alpha):
    B, T, D = x.shape
    nb, _, tc2 = pos.shape
    tc = tc2 // 2
    nT = T // tc
    S, L = y.shape[1:]
    blk = pl.BlockSpec((1, tc, D), lambda b_, i: (b_, i, 0))
    return pl.pallas_call(
        functools.partial(_combine_kernel, alpha=alpha),
        grid=(B, nT),
        in_specs=[pl.BlockSpec((1, 1, tc2), lambda b_, i: (b_ * nT + i, 0, 0), memory_space=pltpu.SMEM),
                  pl.BlockSpec((1, 1, tc2), lambda b_, i: (jnp.minimum(b_ * nT + i + 1, nb - 1), 0, 0),
                               memory_space=pltpu.SMEM),
                  pl.BlockSpec(memory_space=pl.ANY),
                  blk,
                  pl.BlockSpec((1, 6, D), lambda b_, i: (b_ + mod_row0, 0, 0)),
                  pl.BlockSpec((tc, LANES), lambda b_, i: (b_ * nT + i, 0)),
                  _const_spec((1, D)), _const_spec((1, D))],
        out_specs=blk,
        out_shape=jax.ShapeDtypeStruct((B, T, D), F32),
        scratch_shapes=[pltpu.VMEM((2, tc2, S, L), F32), pltpu.SemaphoreType.DMA((2,))],
        compiler_params=_params(2),
        name="combine",
    )(pos, pos, y, x, mod, meta, lng.reshape(1, D), lnb.reshape(1, D))


def _moe_sparse(x, mod, mod_row0, wr, br, wg, wu, wd, lng, lnb, alpha):
    B, T, D = x.shape
    E = wg.shape[0]
    N = B * T
    tb, tm = ROUTE_ROWS, EXPERT_ROWS
    assert T % tb == 0
    meta, cnt = _route(x, mod, mod_row0, wr, br, tb)

    counts = cnt[0, :E].astype(jnp.int32)
    padded = (counts + tm - 1) // tm * tm
    ends = jnp.cumsum(padded)
    offs = ends - padded
    col = lambda k: meta[:, k].astype(jnp.int32)
    pos1 = offs[col(META_I1)] + col(META_R1)
    pos2 = offs[col(META_I2)] + col(META_R2)
    pos = jnp.concatenate([pos1.reshape(N // tb, tb), pos2.reshape(N // tb, tb)], axis=1)
    pos = pos.reshape(N // tb, 1, 2 * tb)
    n_rows = 2 * N + E * tm
    n_tiles = n_rows // tm
    n_used = (ends[-1:] // tm).astype(jnp.int32)
    tile_expert = jnp.sum(jnp.arange(n_tiles)[:, None] * tm >= ends[None, :], axis=1)
    tile_expert = jnp.minimum(tile_expert, E - 1).astype(jnp.int32)
    pads = jnp.stack([jnp.concatenate([offs + counts, n_used]),
                      jnp.concatenate([ends, jnp.full((1,), n_tiles)])]).astype(jnp.int32)

    xs = _dispatch(pads, pos, x, mod, mod_row0, n_rows, E, tm)
    y = _experts(tile_expert, n_used, xs, wg, wu, wd, tm)
    return _combine(pos, y, x, mod, mod_row0, meta, lng, lnb, alpha)


def _kv_kernel(x_ref, mod_ref, wk_ref, wv_ref, kb_ref, vb_ref, kf_ref, vf_ref):
    bb, tt, D = x_ref.shape
    i = pl.program_id(1)
    shift = mod_ref[:, 0:1, :]
    scale = mod_ref[:, 1:2, :]
    h = (x_ref[...] * (1 + scale) + shift).astype(BF16).reshape(bb * tt, D)
    k = jnp.dot(h, wk_ref[...], preferred_element_type=F32).reshape(bb, tt, D)
    v = jnp.dot(h, wv_ref[...], preferred_element_type=F32).reshape(bb, tt, D)
    kb_ref[...] = k.astype(BF16)
    vb_ref[...] = v.astype(BF16)

    @pl.when(i == pl.num_programs(1) - 1)
    def _():
        kf_ref[...] = k
        vf_ref[...] = v


def _kv(x, mod, mod_row0, wk, wv, bb, tt):
    B, T, D = x.shape
    mb = mod_row0 // bb
    blk = pl.BlockSpec((bb, tt, D), lambda b_, i: (b_, i, 0))
    last = pl.BlockSpec((bb, tt, D), lambda b_, i: (b_, 0, 0))
    return pl.pallas_call(
        _kv_kernel,
        grid=(B // bb, T // tt),
        in_specs=[blk,
                  pl.BlockSpec((bb, 2, D), lambda b_, i: (b_ + mb, 0, 0)),
                  _const_spec((D, D)), _const_spec((D, D))],
        out_specs=[blk, blk, last, last],
        out_shape=[jax.ShapeDtypeStruct((B, T, D), BF16), jax.ShapeDtypeStruct((B, T, D), BF16),
                   jax.ShapeDtypeStruct((B, tt, D), F32), jax.ShapeDtypeStruct((B, tt, D), F32)],
        compiler_params=_params(2),
        name="kv",
    )(x, mod, wk, wv)


ATTN_ROWS = 256


def _attn_kernel(x_ref, mod_ref, kp_ref, kc_ref, vp_ref, vc_ref, bias_ref, wq_ref, wo_ref,
                 lng_ref, lnb_ref, o_ref, kcat_ref, vcat_ref, q_ref, a_ref,
                 *, alpha, chunk, past, pad, mask_first, q_scale):
    _, tt, D = x_ref.shape
    n_chunks = tt // chunk
    win = pad + past + chunk
    head_dim = LANES // 2
    i = pl.program_id(1)

    if pad:
        kcat_ref[0:pad, :] = jnp.zeros((pad, D), BF16)
        vcat_ref[0:pad, :] = jnp.zeros((pad, D), BF16)
    kcat_ref[pad:pad + past, :] = kp_ref[0]
    kcat_ref[pad + past:, :] = kc_ref[0]
    vcat_ref[pad:pad + past, :] = vp_ref[0]
    vcat_ref[pad + past:, :] = vc_ref[0]

    x = x_ref[0]
    h = (x * (1 + mod_ref[0, 1:2, :]) + mod_ref[0, 0:1, :]).astype(BF16)
    q = jnp.dot(h, wq_ref[...], preferred_element_type=F32) * q_scale
    q_ref[...] = q.astype(BF16)

    row_head = lax.broadcasted_iota(jnp.int32, (2 * chunk, LANES), 0) >= chunk
    lane_head = lax.broadcasted_iota(jnp.int32, (2 * chunk, LANES), 1) >= head_dim
    own = row_head == lane_head
    out_lane_head0 = lax.broadcasted_iota(jnp.int32, (chunk, LANES), 1) < head_dim
    key_col = lax.broadcasted_iota(jnp.int32, (1, win), 1)

    def chunk_loop(masked):
        def one_chunk(g, carry):
            r0 = pl.multiple_of(g * chunk, chunk)
            for pr in range(D // LANES):
                lanes = slice(pr * LANES, (pr + 1) * LANES)
                qp = q_ref[pl.ds(r0, chunk), lanes]
                qbd = jnp.where(own, jnp.concatenate([qp, qp], axis=0), jnp.zeros((), BF16))
                kw = kcat_ref[pl.ds(r0, win), lanes]
                s = lax.dot_general(qbd, kw, (((1,), (1,)), ((), ())), preferred_element_type=F32)
                s = s + bias_ref[pr]
                if masked:
                    s = jnp.where(key_col + r0 >= pad + past, s, NEG_INF)
                m = jnp.max(s, axis=-1, keepdims=True)
                ex = jnp.exp(s - m)
                l = jnp.sum(ex, axis=-1, keepdims=True)
                p = (ex * (1.0 / l)).astype(BF16)
                vw = vcat_ref[pl.ds(r0, win), lanes]
                r = jnp.dot(p, vw, preferred_element_type=F32)
                a_ref[pl.ds(r0, chunk), lanes] = jnp.where(out_lane_head0, r[:chunk], r[chunk:]).astype(BF16)
            return carry
        lax.fori_loop(0, n_chunks, one_chunk, 0)

    if mask_first:
        @pl.when(i == 0)
        def _():
            chunk_loop(True)

        @pl.when(i > 0)
        def _():
            chunk_loop(False)
    else:
        chunk_loop(False)

    out = jnp.dot(a_ref[...], wo_ref[...], preferred_element_type=F32)
    v = alpha * x + mod_ref[0, 2:3, :] * out
    o_ref[0] = _layer_norm(v, lng_ref[...], lnb_ref[...])


def _pair_bias(table, past, chunk, pad):
    H = table.shape[0]
    win = pad + past + chunk
    L = win + chunk
    m = jnp.arange(L) - (chunk - 1)
    d = past + pad - m
    u = table[:, jnp.clip(d, -REL_CLIP, REL_CLIP) + REL_CLIP]
    skew = jnp.broadcast_to(u[:, None, :], (H, chunk, L)).reshape(H, chunk * L)
    skew = skew[:, :chunk * (L - 1)].reshape(H, chunk, L - 1)
    bias = skew[:, :, chunk - 1:chunk - 1 + win]
    key = jnp.arange(win)[None, :] - (pad + past)
    start = jnp.arange(chunk)[:, None] // CHUNK * CHUNK
    visible = jnp.logical_and(key >= start - past, key < start + CHUNK)
    bias = jnp.where(visible, bias, NEG_INF)
    return bias.reshape(H // 2, 2 * chunk, win)


def _attn(x, mod, mod_row0, k_prev, k_cur, v_prev, v_cur, table, wq, wo, lng, lnb,
          tt, chunk, mask_first, alpha):
    B, T, D = x.shape
    past = WINDOW
    assert past % tt == 0 or tt % past == 0
    assert tt == past or T == tt
    pad = -(past + chunk) % LANES
    bias = _pair_bias(table, past, chunk, pad)
    npairs = D // LANES
    head_dim = D // N_HEADS
    blk = pl.BlockSpec((1, tt, D), lambda b_, i: (b_, i, 0))
    prev = pl.BlockSpec((1, past, D), lambda b_, i: (b_, jnp.maximum(i - 1, 0), 0))
    return pl.pallas_call(
        functools.partial(_attn_kernel, alpha=alpha, chunk=chunk, past=past, pad=pad,
                          mask_first=mask_first, q_scale=head_dim ** -0.5),
        grid=(B, T // tt),
        in_specs=[blk,
                  pl.BlockSpec((1, 6, D), lambda b_, i: (b_ + mod_row0, 0, 0)),
                  prev, blk, prev, blk,
                  _const_spec((npairs, 2 * chunk, pad + past + chunk)),
                  _const_spec((D, D)), _const_spec((D, D)),
                  _const_spec((1, D)), _const_spec((1, D))],
        out_specs=blk,
        out_shape=jax.ShapeDtypeStruct((B, T, D), F32),
        scratch_shapes=[pltpu.VMEM((pad + past + tt, D), BF16), pltpu.VMEM((pad + past + tt, D), BF16),
                        pltpu.VMEM((tt, D), BF16), pltpu.VMEM((tt, D), BF16)],
        compiler_params=_params(2),
        name="attn",
    )(x, mod, k_prev, k_cur, v_prev, v_cur, bias, wq, wo, lng.reshape(1, D), lnb.reshape(1, D))


def _trunk(x, mods, mods_kv, mod_row0, conv_state, kv_cache, p, bb, tt, chunk):
    B, T, D = x.shape
    depth = p['w_ada'].shape[0]
    n_a = p['w_pw1'].shape[0]
    alpha = (2.0 * depth) ** 0.25
    assert T >= CONV_WIDTH - 1 and T % tt == 0 and B % bb == 0 and mod_row0 % bb == 0
    new_conv = []
    kb = vb = kf = vf = None
    for l in range(depth):
        mod = mods[l]
        if l < n_a:
            u = _pw1(x, mod, mod_row0, p['w_pw1'][l], p['b_pw1'][l], bb, tt)
            x = _conv(x, u, conv_state[l], mod, mod_row0, p['w_dw'][l], p['b_dw'][l],
                      p['conv_ln_g'][l], p['conv_ln_b'][l], p['w_pw2'][l], p['b_pw2'][l],
                      p['ln_g'][l, 0], p['ln_b'][l, 0], bb, tt, alpha)
            new_conv.append(u[:, T - (CONV_WIDTH - 1):])
        else:
            j = l - n_a
            if kv_cache is None:
                x = _attn(x, mod, mod_row0, kb, kb, vb, vb, p['rel_bias'][j], p['w_q'][j], p['w_o'][j],
                          p['ln_g'][l, 0], p['ln_b'][l, 0], WINDOW, chunk, True, alpha)
            else:
                x = _attn(x, mod, mod_row0, kv_cache[0], kb, kv_cache[1], vb, p['rel_bias'][j],
                          p['w_q'][j], p['w_o'][j], p['ln_g'][l, 0], p['ln_b'][l, 0],
                          T, chunk, False, alpha)
        i = l // 2
        if l % 2 == 0:
            x = _ffn(x, mod, mod_row0, p['w_ff_gate'][i], p['w_ff_up'][i], p['w_ff_down'][i],
                     p['ln_g'][l, 1], p['ln_b'][l, 1], bb, tt, alpha)
        elif bb == 1 and T % ROUTE_ROWS == 0:
            x = _moe_sparse(x, mod, mod_row0, p['w_router'][i], p['b_router'][i], p['w_e_gate'][i],
                            p['w_e_up'][i], p['w_e_down'][i], p['ln_g'][l, 1], p['ln_b'][l, 1], alpha)
        else:
            x = _moe(x, mod, mod_row0, p['w_router'][i], p['b_router'][i], p['w_e_gate'][i],
                     p['w_e_up'][i], p['w_e_down'][i], p['ln_g'][l, 1], p['ln_b'][l, 1],
                     bb, tt, alpha)
        if l == n_a - 1:
            kb, vb, kf, vf = _kv(x, mods_kv, mod_row0, p['w_k'], p['w_v'], bb, tt)
    return x, jnp.stack(new_conv), kf, vf


def kernel(x_prompt, x_sample, c_prompt, c_sample, state_conv, cache_k, cache_v, w_ada, b_ada, ln_g, ln_b, w_pw1, b_pw1, w_dw, b_dw, conv_ln_g, conv_ln_b, w_pw2, b_pw2, w_ada_kv, b_ada_kv, w_k, w_v, w_q, w_o, rel_bias, w_ff_gate, w_ff_up, w_ff_down, w_router, b_router, w_e_gate, w_e_up, w_e_down):
    B, T, D = x_prompt.shape
    Bs, Ts, _ = x_sample.shape
    H = N_HEADS
    assert Bs + B <= MOD_ROWS and D == H * (LANES // 2)
    depth = w_ada.shape[0]
    n_a = w_pw1.shape[0]

    bf = lambda w: w.astype(BF16)
    p = dict(w_ada=w_ada, ln_g=ln_g, ln_b=ln_b, w_pw1=bf(w_pw1), b_pw1=b_pw1, w_dw=w_dw, b_dw=b_dw,
             conv_ln_g=conv_ln_g, conv_ln_b=conv_ln_b, w_pw2=bf(w_pw2), b_pw2=b_pw2,
             w_k=bf(w_k), w_v=bf(w_v), w_q=bf(w_q), w_o=bf(w_o), rel_bias=rel_bias,
             w_ff_gate=bf(w_ff_gate), w_ff_up=bf(w_ff_up), w_ff_down=bf(w_ff_down),
             w_router=w_router, b_router=b_router,
             w_e_gate=bf(w_e_gate), w_e_up=bf(w_e_up), w_e_down=bf(w_e_down))

    c_all = jnp.concatenate([c_sample, c_prompt, jnp.zeros((MOD_ROWS - Bs - B, D), F32)], axis=0)
    mods = _ada(c_all, w_ada, b_ada, 1536).reshape(depth, MOD_ROWS, 6, D)
    mods_kv = _ada(c_all, w_ada_kv[None], b_ada_kv[None], 1024).reshape(MOD_ROWS, 2, D)

    pad_state = lambda s: jnp.pad(s, ((0, 0), (0, 0), (HALO - (CONV_WIDTH - 1), 0), (0, 0)))
    zero_state = jnp.zeros((n_a, B, HALO, D), F32)
    y_p, conv_p, kf_p, vf_p = _trunk(x_prompt, mods, mods_kv, Bs, zero_state, None, p, 1, WINDOW,
                                     ATTN_ROWS)

    cache = (cache_k.reshape(Bs, -1, D).astype(BF16), cache_v.reshape(Bs, -1, D).astype(BF16))
    y_s, conv_s, kf_s, vf_s = _trunk(x_sample, mods, mods_kv, 0, pad_state(state_conv), cache, p,
                                     Bs, Ts, Ts)

    L = cache_k.shape[1]
    heads = lambda a: a.reshape(a.shape[0], a.shape[1], H, D // H)
    k_s = jnp.concatenate([cache_k, heads(kf_s)], axis=1)[:, -L:]
    v_s = jnp.concatenate([cache_v, heads(vf_s)], axis=1)[:, -L:]
    return (y_p, y_s, conv_p, heads(kf_p), heads(vf_p), conv_s, k_s, v_s)
```

```python
import functools

import jax
import jax.numpy as jnp
from jax import lax
from jax.experimental import pallas as pl
from jax.experimental.pallas import tpu as pltpu

CHUNK = 64
WINDOW = 512
REL_CLIP = 128
CONV_WIDTH = 31
N_HEADS = 16
LN_EPS = 1e-5
NEG_INF = -1e30

LANES = 128
HALO = 32
MOD_ROWS = 16
VMEM_LIMIT = 56 * 1024 * 1024

F32 = jnp.float32
BF16 = jnp.bfloat16


def _layer_norm(v, g, b):
    mu = jnp.mean(v, axis=-1, keepdims=True)
    d = v - mu
    var = jnp.mean(d * d, axis=-1, keepdims=True)
    return d * lax.rsqrt(var + LN_EPS) * g + b


def _silu(v):
    return v * jax.nn.sigmoid(v)


def _const_spec(shape):
    nd = len(shape)
    return pl.BlockSpec(shape, lambda *_: (0,) * nd, pipeline_mode=pl.Buffered(1))


def _params(n_grid):
    return pltpu.CompilerParams(dimension_semantics=("arbitrary",) * n_grid,
                                vmem_limit_bytes=VMEM_LIMIT)


def _ada_kernel(c_ref, w_ref, b_ref, o_ref):
    a = _silu(c_ref[...]).astype(BF16)
    o_ref[0] = jnp.dot(a, w_ref[0].astype(BF16), preferred_element_type=F32) + b_ref[0]


def _ada(c, w, b, tn):
    L, D, N = w.shape
    return pl.pallas_call(
        _ada_kernel,
        grid=(L, N // tn),
        in_specs=[pl.BlockSpec((MOD_ROWS, D), lambda l, j: (0, 0)),
                  pl.BlockSpec((1, D, tn), lambda l, j: (l, 0, j)),
                  pl.BlockSpec((1, 1, tn), lambda l, j: (l, 0, j))],
        out_specs=pl.BlockSpec((1, MOD_ROWS, tn), lambda l, j: (l, 0, j)),
        out_shape=jax.ShapeDtypeStruct((L, MOD_ROWS, N), F32),
        compiler_params=_params(2),
        name="ada",
    )(c, w, b.reshape(L, 1, N))


CONV_ROWS = 64
CONV_LANES = 256
SUBLANES = 8


def _conv_kernel(x_ref, st_ref, mod_ref, w1_ref, b1_ref, wdw_ref, bdw_ref, cg_ref, cb_ref,
                 w2_ref, b2_ref, lng_ref, lnb_ref, o_ref, tail_ref, ext_ref, sh_ref, y_ref, *, alpha):
    bb, tt, D = x_ref.shape
    i = pl.program_id(1)

    h = (x_ref[...] * (1 + mod_ref[:, 1:2, :]) + mod_ref[:, 0:1, :]).astype(BF16).reshape(bb * tt, D)
    z = jnp.dot(h, w1_ref[...], preferred_element_type=F32) + b1_ref[...]
    u = (z[:, :D] * jax.nn.sigmoid(z[:, D:])).reshape(bb, tt, D)

    @pl.when(i == 0)
    def _():
        ext_ref[:, 0:HALO, :] = st_ref[...]

    @pl.when(i > 0)
    def _():
        ext_ref[:, 0:HALO, :] = ext_ref[:, tt:tt + HALO, :]

    ext_ref[:, HALO:, :] = u

    @pl.when(i == pl.num_programs(1) - 1)
    def _():
        tail_ref[...] = u[:, tt - HALO:, :]

    first = HALO - (CONV_WIDTH - 1)
    rows = min(CONV_ROWS, tt)
    for b in range(bb):
        for lb in range(D // CONV_LANES):
            lanes = slice(lb * CONV_LANES, (lb + 1) * CONV_LANES)
            sh_ref[0] = ext_ref[b, :, lanes]
            for s in range(1, SUBLANES):
                sh_ref[s, 0:tt + HALO - SUBLANES, :] = ext_ref[b, s:s + tt + HALO - SUBLANES, lanes]

            def chunk(c, carry, b=b, lanes=lanes):
                r0 = pl.multiple_of(c * rows, rows)
                acc = jnp.zeros((rows, CONV_LANES), F32)
                for j in range(CONV_WIDTH):
                    a, s = divmod(j + first, SUBLANES)
                    acc = acc + wdw_ref[j:j + 1, lanes] * sh_ref[s, pl.ds(r0 + SUBLANES * a, rows), :]
                y_ref[b, pl.ds(r0, rows), lanes] = acc + bdw_ref[:, lanes]
                return carry
            lax.fori_loop(0, tt // rows, chunk, 0)

    yn = _silu(_layer_norm(y_ref[...], cg_ref[...], cb_ref[...]))
    out = jnp.dot(yn.astype(BF16).reshape(bb * tt, D), w2_ref[...],
                  preferred_element_type=F32) + b2_ref[...]
    gate = mod_ref[:, 2:3, :]
    v = alpha * x_ref[...] + gate * out.reshape(bb, tt, D)
    o_ref[...] = _layer_norm(v, lng_ref[...], lnb_ref[...])


def _conv(x, state, mod, mod_row0, w1, b1, w_dw, b_dw, cg, cb, w2, b2, lng, lnb, bb, tt, alpha):
    B, T, D = x.shape
    mb = mod_row0 // bb
    wdw = jnp.pad(w_dw, ((0, HALO - CONV_WIDTH), (0, 0)))
    row = lambda a: a.reshape(1, -1)
    return pl.pallas_call(
        functools.partial(_conv_kernel, alpha=alpha),
        grid=(B // bb, T // tt),
        in_specs=[pl.BlockSpec((bb, tt, D), lambda b_, i: (b_, i, 0)),
                  pl.BlockSpec((bb, HALO, D), lambda b_, i: (b_, 0, 0)),
                  pl.BlockSpec((bb, 6, D), lambda b_, i: (b_ + mb, 0, 0)),
                  _const_spec((D, 2 * D)), _const_spec((1, 2 * D)),
                  _const_spec((HALO, D)),
                  _const_spec((1, D)), _const_spec((1, D)), _const_spec((1, D)),
                  _const_spec((D, D)),
                  _const_spec((1, D)), _const_spec((1, D)), _const_spec((1, D))],
        out_specs=[pl.BlockSpec((bb, tt, D), lambda b_, i: (b_, i, 0)),
                   pl.BlockSpec((bb, HALO, D), lambda b_, i: (b_, 0, 0))],
        out_shape=[jax.ShapeDtypeStruct((B, T, D), F32), jax.ShapeDtypeStruct((B, HALO, D), F32)],
        scratch_shapes=[pltpu.VMEM((bb, HALO + tt, D), F32),
                        pltpu.VMEM((SUBLANES, HALO + tt, CONV_LANES), F32),
                        pltpu.VMEM((bb, tt, D), F32)],
        compiler_params=_params(2),
        name="conv",
    )(x, state, mod, w1, row(b1), wdw, row(b_dw), row(cg), row(cb), w2, row(b2), row(lng), row(lnb))


def _ff_chunks(F, width):
    return [(f0, min(width, F - f0)) for f0 in range(0, F, width)]


def _swiglu(h, wg_ref, wu_ref, wd_ref, lead):
    F = wg_ref.shape[-1]
    acc = None
    for f0, fw in _ff_chunks(F, 1024):
        g = jnp.dot(h, wg_ref[lead + (slice(None), slice(f0, f0 + fw))], preferred_element_type=F32)
        u = jnp.dot(h, wu_ref[lead + (slice(None), slice(f0, f0 + fw))], preferred_element_type=F32)
        a = (_silu(g) * u).astype(BF16)
        d = jnp.dot(a, wd_ref[lead + (slice(f0, f0 + fw), slice(None))], preferred_element_type=F32)
        acc = d if acc is None else acc + d
    return acc


def _ffn_kernel(x_ref, mod_ref, wg_ref, wu_ref, wd_ref, lng_ref, lnb_ref, o_ref, *, alpha):
    bb, tt, D = x_ref.shape
    shift = mod_ref[:, 3:4, :]
    scale = mod_ref[:, 4:5, :]
    gate = mod_ref[:, 5:6, :]
    x = x_ref[...]
    h = (x * (1 + scale) + shift).astype(BF16).reshape(bb * tt, D)
    f = _swiglu(h, wg_ref, wu_ref, wd_ref, (0,))
    v = alpha * x + gate * f.reshape(bb, tt, D)
    o_ref[...] = _layer_norm(v, lng_ref[...], lnb_ref[...])


def _layer_spec(shape, layer):
    nd = len(shape)
    return pl.BlockSpec((1,) + shape, lambda *_: (layer,) + (0,) * nd, pipeline_mode=pl.Buffered(1))


def _ffn(x, mod, mod_row0, wg, wu, wd, layer, lng, lnb, bb, tt, alpha):
    B, T, D = x.shape
    F = wg.shape[-1]
    mb = mod_row0 // bb
    return pl.pallas_call(
        functools.partial(_ffn_kernel, alpha=alpha),
        grid=(B // bb, T // tt),
        in_specs=[pl.BlockSpec((bb, tt, D), lambda b_, i: (b_, i, 0)),
                  pl.BlockSpec((bb, 6, D), lambda b_, i: (b_ + mb, 0, 0)),
                  _layer_spec((D, F), layer), _layer_spec((D, F), layer), _layer_spec((F, D), layer),
                  _const_spec((1, D)), _const_spec((1, D))],
        out_specs=pl.BlockSpec((bb, tt, D), lambda b_, i: (b_, i, 0)),
        out_shape=jax.ShapeDtypeStruct((B, T, D), F32),
        compiler_params=_params(2),
        name="ffn",
    )(x, mod, wg, wu, wd, lng.reshape(1, D), lnb.reshape(1, D))


def _router_top2(h32, wr_ref, br_ref, n_exp):
    M, D = h32.shape
    rows = []
    for e in range(n_exp):
        prod = h32 * wr_ref[e:e + 1, :]
        part = prod[:, 0:LANES]
        for c in range(1, D // LANES):
            part = part + prod[:, c * LANES:(c + 1) * LANES]
        rows.append(jnp.sum(part.T, axis=0, keepdims=True))
    logits = jnp.concatenate(rows, axis=0) + br_ref[...]
    ex = jnp.exp(logits - jnp.max(logits, axis=0, keepdims=True))
    p = ex / jnp.sum(ex, axis=0, keepdims=True)
    eid = lax.broadcasted_iota(jnp.int32, p.shape, 0)

    def best(v):
        top = jnp.max(v, axis=0, keepdims=True)
        return top, jnp.min(jnp.where(v == top, eid, n_exp), axis=0, keepdims=True)

    p1, i1 = best(p)
    p2, i2 = best(jnp.where(eid == i1, -1.0, p))
    tot = p1 + p2
    return i1, i2, p1 / tot, p2 / tot


def _to_columns(rows, M):
    pad = jnp.zeros((LANES - len(rows), M), F32)
    return jnp.concatenate(list(rows) + [pad], axis=0).T


def _router_gates(h32, wr_ref, br_ref, n_exp):
    i1, i2, g1, g2 = _router_top2(h32, wr_ref, br_ref, n_exp)
    rows = [jnp.where(i1 == e, g1, 0.0) + jnp.where(i2 == e, g2, 0.0) for e in range(n_exp)]
    return _to_columns(rows, h32.shape[0])


def _moe_kernel(x_ref, mod_ref, wr_ref, br_ref, wg_ref, wu_ref, wd_ref, lng_ref, lnb_ref, o_ref,
                h_ref, gate_ref, acc_ref, *, alpha, n_exp):
    bb, tt, D = x_ref.shape
    e = pl.program_id(2)

    @pl.when(e == 0)
    def _():
        shift = mod_ref[:, 3:4, :]
        scale = mod_ref[:, 4:5, :]
        h32 = (x_ref[...] * (1 + scale) + shift).reshape(bb * tt, D)
        h_ref[...] = h32.astype(BF16)
        gates = _router_gates(h32, wr_ref, br_ref, n_exp)
        for k in range(n_exp):
            gate_ref[k] = gates[:, k:k + 1]
        acc_ref[...] = jnp.zeros_like(acc_ref)

    f = _swiglu(h_ref[...], wg_ref, wu_ref, wd_ref, (0, 0))
    acc_ref[...] += gate_ref[e] * f

    @pl.when(e == n_exp - 1)
    def _():
        v = alpha * x_ref[...] + mod_ref[:, 5:6, :] * acc_ref[...].reshape(bb, tt, D)
        o_ref[...] = _layer_norm(v, lng_ref[...], lnb_ref[...])


def _moe(x, mod, mod_row0, wr, br, wg, wu, wd, layer, lng, lnb, bb, tt, alpha):
    B, T, D = x.shape
    _, E, _, Fe = wg.shape
    mb = mod_row0 // bb
    M = bb * tt
    return pl.pallas_call(
        functools.partial(_moe_kernel, alpha=alpha, n_exp=E),
        grid=(B // bb, T // tt, E),
        in_specs=[pl.BlockSpec((bb, tt, D), lambda b_, i, e: (b_, i, 0)),
                  pl.BlockSpec((bb, 6, D), lambda b_, i, e: (b_ + mb, 0, 0)),
                  _const_spec((E, D)), _const_spec((E, 1)),
                  pl.BlockSpec((1, 1, D, Fe), lambda b_, i, e: (layer, e, 0, 0)),
                  pl.BlockSpec((1, 1, D, Fe), lambda b_, i, e: (layer, e, 0, 0)),
                  pl.BlockSpec((1, 1, Fe, D), lambda b_, i, e: (layer, e, 0, 0)),
                  _const_spec((1, D)), _const_spec((1, D))],
        out_specs=pl.BlockSpec((bb, tt, D), lambda b_, i, e: (b_, i, 0)),
        out_shape=jax.ShapeDtypeStruct((B, T, D), F32),
        scratch_shapes=[pltpu.VMEM((M, D), BF16), pltpu.VMEM((E, M, 1), F32),
                        pltpu.VMEM((M, D), F32)],
        compiler_params=_params(3),
        name="moe",
    )(x, mod, wr.T, br.reshape(E, 1), wg, wu, wd, lng.reshape(1, D), lnb.reshape(1, D))


ROUTE_ROWS = 512
EXPERT_ROWS = 512
META_I1, META_I2, META_G1, META_G2, META_R1, META_R2 = range(6)


def _route_kernel(x_ref, mod_ref, wr_ref, br_ref, meta_ref, cnt_ref, *, n_exp):
    _, tt, D = x_ref.shape
    first = jnp.logical_and(pl.program_id(0) == 0, pl.program_id(1) == 0)

    @pl.when(first)
    def _():
        cnt_ref[...] = jnp.zeros_like(cnt_ref)

    h32 = x_ref[0] * (1 + mod_ref[0, 4:5, :]) + mod_ref[0, 3:4, :]
    i1, i2, g1, g2 = _router_top2(h32, wr_ref, br_ref, n_exp)

    eid = lax.broadcasted_iota(jnp.int32, (n_exp, tt), 0)
    o1 = (eid == i1).astype(F32)
    o2 = (eid == i2).astype(F32)
    upper = (lax.broadcasted_iota(jnp.int32, (tt, tt), 0) <
             lax.broadcasted_iota(jnp.int32, (tt, tt), 1)).astype(BF16)
    c1 = jnp.dot(o1.astype(BF16), upper, preferred_element_type=F32)
    c2 = jnp.dot(o2.astype(BF16), upper, preferred_element_type=F32)
    base = cnt_ref[:, 0:1]
    tot1 = jnp.sum(o1, axis=1, keepdims=True)
    tot2 = jnp.sum(o2, axis=1, keepdims=True)
    r1 = jnp.sum(o1 * (c1 + base), axis=0, keepdims=True)
    r2 = jnp.sum(o2 * (c2 + (base + tot1)), axis=0, keepdims=True)
    cnt_ref[...] = jnp.broadcast_to(base + tot1 + tot2, cnt_ref.shape)

    rows = [i1.astype(F32), i2.astype(F32), g1, g2, r1, r2]
    meta_ref[0] = jnp.concatenate(rows + [jnp.zeros((SUBLANES - len(rows), tt), F32)], axis=0)


def _route(x, mod, mod_row0, wr, br, tt):
    B, T, D = x.shape
    E = wr.shape[1]
    nT = T // tt
    return pl.pallas_call(
        functools.partial(_route_kernel, n_exp=E),
        grid=(B, nT),
        in_specs=[pl.BlockSpec((1, tt, D), lambda b_, i: (b_, i, 0)),
                  pl.BlockSpec((1, 6, D), lambda b_, i: (b_ + mod_row0, 0, 0)),
                  _const_spec((E, D)), _const_spec((E, 1))],
        out_specs=[pl.BlockSpec((1, SUBLANES, tt), lambda b_, i: (b_ * nT + i, 0, 0)),
                   pl.BlockSpec((E, LANES), lambda b_, i: (0, 0))],
        out_shape=[jax.ShapeDtypeStruct((B * nT, SUBLANES, tt), F32),
                   jax.ShapeDtypeStruct((E, LANES), F32)],
        compiler_params=_params(2),
        name="route",
    )(x, mod, wr.T, br.reshape(E, 1))


def _row_copy(src_ref, src_row, dst_ref, dst_row, sem):
    return pltpu.make_async_copy(src_ref.at[pl.ds(src_row, 1)], dst_ref.at[pl.ds(dst_row, 1)], sem)


def _dispatch_kernel(pad_ref, pos_ref, x_ref, mod_ref, xs_ref, rows_ref, zero_ref, sems, *, n_exp):
    _, tb, D = x_ref.shape
    tm = zero_ref.shape[0]
    i = pl.program_id(0) * pl.num_programs(1) + pl.program_id(1)
    last = pl.num_programs(0) * pl.num_programs(1) - 1
    slot = i % 2
    pad_sem = sems.at[2]

    def tail_copy(t):
        return pltpu.make_async_copy(zero_ref, xs_ref.at[pl.ds(pl.multiple_of(t * tm, tm), tm)], pad_sem)

    @pl.when(i == 0)
    def _():
        zero_ref[...] = jnp.zeros_like(zero_ref)
        for e in range(n_exp):
            def fill(r, c):
                _row_copy(zero_ref, 0, xs_ref, r, pad_sem).start()
                return c
            lax.fori_loop(pad_ref[0, e], pad_ref[1, e], fill, 0)

        def fill_tail(t, c):
            tail_copy(t).start()
            return c
        lax.fori_loop(pad_ref[0, n_exp], pad_ref[1, n_exp], fill_tail, 0)

    h32 = x_ref[0] * (1 + mod_ref[0, 4:5, :]) + mod_ref[0, 3:4, :]
    rows_ref[slot] = h32.reshape(tb, D // LANES, LANES)

    def issue(t, c):
        _row_copy(rows_ref.at[slot], t, xs_ref, pos_ref[0, 0, t], sems.at[slot]).start()
        _row_copy(rows_ref.at[slot], t, xs_ref, pos_ref[0, 0, tb + t], sems.at[slot]).start()
        return c
    lax.fori_loop(0, tb, issue, 0, unroll=8)

    def wait_step(s):
        for _ in range(2):
            pltpu.make_async_copy(rows_ref.at[s], xs_ref.at[pl.ds(0, tb)], sems.at[s]).wait()

    @pl.when(i > 0)
    def _():
        wait_step(1 - slot)

    @pl.when(i == last)
    def _():
        wait_step(slot)
        for e in range(n_exp):
            def drain(r, c):
                _row_copy(zero_ref, 0, xs_ref, r, pad_sem).wait()
                return c
            lax.fori_loop(pad_ref[0, e], pad_ref[1, e], drain, 0)

        def drain_tail(t, c):
            tail_copy(t).wait()
            return c
        lax.fori_loop(pad_ref[0, n_exp], pad_ref[1, n_exp], drain_tail, 0)


def _dispatch(pads, pos, x, mod, mod_row0, n_rows, n_exp, tm):
    B, T, D = x.shape
    tb = pos.shape[2] // 2
    nT = T // tb
    S, L = D // LANES, LANES
    return pl.pallas_call(
        functools.partial(_dispatch_kernel, n_exp=n_exp),
        grid_spec=pltpu.PrefetchScalarGridSpec(
            num_scalar_prefetch=1,
            grid=(B, nT),
            in_specs=[pl.BlockSpec((1, 1, 2 * tb), lambda b_, i, pads_: (b_ * nT + i, 0, 0),
                                   memory_space=pltpu.SMEM),
                      pl.BlockSpec((1, tb, D), lambda b_, i, pads_: (b_, i, 0)),
                      pl.BlockSpec((1, 6, D), lambda b_, i, pads_: (b_ + mod_row0, 0, 0))],
            out_specs=pl.BlockSpec(memory_space=pl.ANY),
            scratch_shapes=[pltpu.VMEM((2, tb, S, L), F32), pltpu.VMEM((tm, S, L), F32),
                            pltpu.SemaphoreType.DMA((3,))]),
        out_shape=jax.ShapeDtypeStruct((n_rows, S, L), F32),
        compiler_params=_params(2),
        name="dispatch",
    )(pads, pos, x, mod)


def _experts_kernel(te_ref, nu_ref, xs_ref, wg_ref, wu_ref, wd_ref, y_ref):
    tm, S, L = xs_ref.shape
    i = pl.program_id(0)

    @pl.when(i < nu_ref[0])
    def _():
        h = xs_ref[...].reshape(tm, S * L).astype(BF16)
        y_ref[...] = _swiglu(h, wg_ref, wu_ref, wd_ref, (0, 0)).reshape(tm, S, L)

    @pl.when(i >= nu_ref[0])
    def _():
        y_ref[...] = jnp.zeros_like(y_ref)


def _experts(tile_expert, n_used, xs, wg, wu, wd, layer, tm):
    n_rows, S, L = xs.shape
    _, E, D, Fe = wg.shape
    used = lambda i, te, nu: jnp.minimum(i, nu[0] - 1)
    return pl.pallas_call(
        _experts_kernel,
        grid_spec=pltpu.PrefetchScalarGridSpec(
            num_scalar_prefetch=2,
            grid=(n_rows // tm,),
            in_specs=[pl.BlockSpec((tm, S, L), lambda i, te, nu: (used(i, te, nu), 0, 0)),
                      pl.BlockSpec((1, 1, D, Fe), lambda i, te, nu: (layer, te[used(i, te, nu)], 0, 0)),
                      pl.BlockSpec((1, 1, D, Fe), lambda i, te, nu: (layer, te[used(i, te, nu)], 0, 0)),
                      pl.BlockSpec((1, 1, Fe, D), lambda i, te, nu: (layer, te[used(i, te, nu)], 0, 0))],
            out_specs=pl.BlockSpec((tm, S, L), lambda i, te, nu: (i, 0, 0))),
        out_shape=jax.ShapeDtypeStruct((n_rows, S, L), F32),
        compiler_params=_params(1),
        name="experts",
    )(tile_expert, n_used, xs, wg, wu, wd)


def _combine_kernel(pcur_ref, pnext_ref, y_ref, x_ref, mod_ref, meta_ref, lng_ref, lnb_ref, o_ref,
                    buf_ref, sems, *, alpha):
    _, tc, D = x_ref.shape
    step = pl.program_id(0) * pl.num_programs(1) + pl.program_id(1)
    n_steps = pl.num_programs(0) * pl.num_programs(1)
    slot = step % 2

    def issue(p_ref, s):
        def body(t, c):
            _row_copy(y_ref, p_ref[0, 0, t], buf_ref.at[s], t, sems.at[s]).start()
            _row_copy(y_ref, p_ref[0, 0, tc + t], buf_ref.at[s], tc + t, sems.at[s]).start()
            return c
        lax.fori_loop(0, tc, body, 0, unroll=8)

    @pl.when(step == 0)
    def _():
        issue(pcur_ref, 0)

    @pl.when(step + 1 < n_steps)
    def _():
        issue(pnext_ref, 1 - slot)

    pltpu.make_async_copy(y_ref.at[pl.ds(0, 2 * tc)], buf_ref.at[slot], sems.at[slot]).wait()
    y1 = buf_ref[slot, 0:tc].reshape(tc, D)
    y2 = buf_ref[slot, tc:2 * tc].reshape(tc, D)
    gates = _to_columns([meta_ref[0, META_G1:META_G1 + 1, :], meta_ref[0, META_G2:META_G2 + 1, :]], tc)
    f = gates[:, 0:1] * y1 + gates[:, 1:2] * y2
    v = alpha * x_ref[0] + mod_ref[0, 5:6, :] * f
    o_ref[0] = _layer_norm(v, lng_ref[...], lnb_ref[...])


def _combine(pos, y, x, mod, mod_row0, meta, lng, lnb, alpha):
    B, T, D = x.shape
    nb, _, tc2 = pos.shape
    tc = tc2 // 2
    nT = T // tc
    S, L = y.shape[1:]
    blk = pl.BlockSpec((1, tc, D), lambda b_, i: (b_, i, 0))
    return pl.pallas_call(
        functools.partial(_combine_kernel, alpha=alpha),
        grid=(B, nT),
        in_specs=[pl.BlockSpec((1, 1, tc2), lambda b_, i: (b_ * nT + i, 0, 0), memory_space=pltpu.SMEM),
                  pl.BlockSpec((1, 1, tc2), lambda b_, i: (jnp.minimum(b_ * nT + i + 1, nb - 1), 0, 0),
                               memory_space=pltpu.SMEM),
                  pl.BlockSpec(memory_space=pl.ANY),
                  blk,
                  pl.BlockSpec((1, 6, D), lambda b_, i: (b_ + mod_row0, 0, 0)),
                  pl.BlockSpec((1, SUBLANES, tc), lambda b_, i: (b_ * nT + i, 0, 0)),
                  _const_spec((1, D)), _const_spec((1, D))],
        out_specs=blk,
        out_shape=jax.ShapeDtypeStruct((B, T, D), F32),
        scratch_shapes=[pltpu.VMEM((2, tc2, S, L), F32), pltpu.SemaphoreType.DMA((2,))],
        compiler_params=_params(2),
        name="combine",
    )(pos, pos, y, x, mod, meta, lng.reshape(1, D), lnb.reshape(1, D))


def _moe_sparse(x, mod, mod_row0, wr, br, wg, wu, wd, layer, lng, lnb, alpha):
    B, T, D = x.shape
    E = wg.shape[1]
    N = B * T
    tb, tm = ROUTE_ROWS, EXPERT_ROWS
    assert T % tb == 0 and E == SUBLANES
    meta, cnt = _route(x, mod, mod_row0, wr, br, tb)

    counts = cnt[:, 0].astype(jnp.int32)
    padded = (counts + tm - 1) // tm * tm
    ends = jnp.cumsum(padded)
    offs = ends - padded
    field = lambda k: meta[:, k, :].astype(jnp.int32)
    dest = lambda ids, ranks: jnp.sum(jnp.where(ids[..., None] == jnp.arange(E), offs, 0), -1) + ranks
    pos1 = dest(field(META_I1), field(META_R1))
    pos2 = dest(field(META_I2), field(META_R2))
    pos = jnp.concatenate([pos1, pos2], axis=1).reshape(N // tb, 1, 2 * tb)
    n_rows = 2 * N + E * tm
    n_tiles = n_rows // tm
    n_used = (ends[-1:] // tm).astype(jnp.int32)
    tile_expert = jnp.sum(jnp.arange(n_tiles)[:, None] * tm >= ends[None, :], axis=1)
    tile_expert = jnp.minimum(tile_expert, E - 1).astype(jnp.int32)
    pads = jnp.stack([jnp.concatenate([offs + counts, n_used]),
                      jnp.concatenate([ends, jnp.full((1,), n_tiles)])]).astype(jnp.int32)

    xs = _dispatch(pads, pos, x, mod, mod_row0, n_rows, E, tm)
    y = _experts(tile_expert, n_used, xs, wg, wu, wd, layer, tm)
    return _combine(pos, y, x, mod, mod_row0, meta, lng, lnb, alpha)


def _kv_kernel(x_ref, mod_ref, wk_ref, wv_ref, kb_ref, vb_ref, kf_ref, vf_ref):
    bb, tt, D = x_ref.shape
    i = pl.program_id(1)
    shift = mod_ref[:, 0:1, :]
    scale = mod_ref[:, 1:2, :]
    h = (x_ref[...] * (1 + scale) + shift).astype(BF16).reshape(bb * tt, D)
    k = jnp.dot(h, wk_ref[...], preferred_element_type=F32).reshape(bb, tt, D)
    v = jnp.dot(h, wv_ref[...], preferred_element_type=F32).reshape(bb, tt, D)
    kb_ref[...] = k.astype(BF16)
    vb_ref[...] = v.astype(BF16)

    @pl.when(i == pl.num_programs(1) - 1)
    def _():
        kf_ref[...] = k
        vf_ref[...] = v


def _kv(x, mod, mod_row0, wk, wv, bb, tt):
    B, T, D = x.shape
    mb = mod_row0 // bb
    blk = pl.BlockSpec((bb, tt, D), lambda b_, i: (b_, i, 0))
    last = pl.BlockSpec((bb, tt, D), lambda b_, i: (b_, 0, 0))
    return pl.pallas_call(
        _kv_kernel,
        grid=(B // bb, T // tt),
        in_specs=[blk,
                  pl.BlockSpec((bb, 2, D), lambda b_, i: (b_ + mb, 0, 0)),
                  _const_spec((D, D)), _const_spec((D, D))],
        out_specs=[blk, blk, last, last],
        out_shape=[jax.ShapeDtypeStruct((B, T, D), BF16), jax.ShapeDtypeStruct((B, T, D), BF16),
                   jax.ShapeDtypeStruct((B, tt, D), F32), jax.ShapeDtypeStruct((B, tt, D), F32)],
        compiler_params=_params(2),
        name="kv",
    )(x, mod, wk, wv)


ATTN_ROWS = 256


def _attn_kernel(x_ref, mod_ref, kp_ref, kc_ref, vp_ref, vc_ref, bias_ref, wq_ref, wo_ref,
                 lng_ref, lnb_ref, o_ref, kcat_ref, vcat_ref, q_ref, a_ref,
                 *, alpha, chunk, past, pad, mask_first, q_scale):
    _, tt, D = x_ref.shape
    n_chunks = tt // chunk
    win = pad + past + chunk
    head_dim = LANES // 2
    i = pl.program_id(1)

    if pad:
        kcat_ref[0:pad, :] = jnp.zeros((pad, D), BF16)
        vcat_ref[0:pad, :] = jnp.zeros((pad, D), BF16)
    kcat_ref[pad:pad + past, :] = kp_ref[0]
    kcat_ref[pad + past:, :] = kc_ref[0]
    vcat_ref[pad:pad + past, :] = vp_ref[0]
    vcat_ref[pad + past:, :] = vc_ref[0]

    x = x_ref[0]
    h = (x * (1 + mod_ref[0, 1:2, :]) + mod_ref[0, 0:1, :]).astype(BF16)
    q = jnp.dot(h, wq_ref[...], preferred_element_type=F32) * q_scale
    q_ref[...] = q.astype(BF16)

    row_head = lax.broadcasted_iota(jnp.int32, (2 * chunk, LANES), 0) >= chunk
    lane_head = lax.broadcasted_iota(jnp.int32, (2 * chunk, LANES), 1) >= head_dim
    own = row_head == lane_head
    out_lane_head0 = lax.broadcasted_iota(jnp.int32, (chunk, LANES), 1) < head_dim
    key_col = lax.broadcasted_iota(jnp.int32, (1, win), 1)

    def chunk_loop(masked):
        def one_chunk(g, carry):
            r0 = pl.multiple_of(g * chunk, chunk)
            for pr in range(D // LANES):
                lanes = slice(pr * LANES, (pr + 1) * LANES)
                qp = q_ref[pl.ds(r0, chunk), lanes]
                qbd = jnp.where(own, jnp.concatenate([qp, qp], axis=0), jnp.zeros((), BF16))
                kw = kcat_ref[pl.ds(r0, win), lanes]
                s = lax.dot_general(qbd, kw, (((1,), (1,)), ((), ())), preferred_element_type=F32)
                s = s + bias_ref[pr]
                if masked:
                    s = jnp.where(key_col + r0 >= pad + past, s, NEG_INF)
                m = jnp.max(s, axis=-1, keepdims=True)
                ex = jnp.exp(s - m)
                l = jnp.sum(ex, axis=-1, keepdims=True)
                vw = vcat_ref[pl.ds(r0, win), lanes]
                r = jnp.dot(ex.astype(BF16), vw, preferred_element_type=F32) * (1.0 / l)
                a_ref[pl.ds(r0, chunk), lanes] = jnp.where(out_lane_head0, r[:chunk], r[chunk:]).astype(BF16)
            return carry
        lax.fori_loop(0, n_chunks, one_chunk, 0)

    if mask_first:
        @pl.when(i == 0)
        def _():
            chunk_loop(True)

        @pl.when(i > 0)
        def _():
            chunk_loop(False)
    else:
        chunk_loop(False)

    out = jnp.dot(a_ref[...], wo_ref[...], preferred_element_type=F32)
    v = alpha * x + mod_ref[0, 2:3, :] * out
    o_ref[0] = _layer_norm(v, lng_ref[...], lnb_ref[...])


def _pair_bias(table, past, chunk, pad):
    H = table.shape[0]
    win = pad + past + chunk
    L = win + chunk
    m = jnp.arange(L) - (chunk - 1)
    d = past + pad - m
    u = table[:, jnp.clip(d, -REL_CLIP, REL_CLIP) + REL_CLIP]
    skew = jnp.broadcast_to(u[:, None, :], (H, chunk, L)).reshape(H, chunk * L)
    skew = skew[:, :chunk * (L - 1)].reshape(H, chunk, L - 1)
    bias = skew[:, :, chunk - 1:chunk - 1 + win]
    key = jnp.arange(win)[None, :] - (pad + past)
    start = jnp.arange(chunk)[:, None] // CHUNK * CHUNK
    visible = jnp.logical_and(key >= start - past, key < start + CHUNK)
    bias = jnp.where(visible, bias, NEG_INF)
    return bias.reshape(H // 2, 2 * chunk, win)


def _attn(x, mod, mod_row0, k_prev, k_cur, v_prev, v_cur, table, wq, wo, lng, lnb,
          tt, chunk, mask_first, alpha):
    B, T, D = x.shape
    past = WINDOW
    assert past % tt == 0 or tt % past == 0
    assert tt == past or T == tt
    pad = -(past + chunk) % LANES
    bias = _pair_bias(table, past, chunk, pad)
    npairs = D // LANES
    head_dim = D // N_HEADS
    blk = pl.BlockSpec((1, tt, D), lambda b_, i: (b_, i, 0))
    prev = pl.BlockSpec((1, past, D), lambda b_, i: (b_, jnp.maximum(i - 1, 0), 0))
    return pl.pallas_call(
        functools.partial(_attn_kernel, alpha=alpha, chunk=chunk, past=past, pad=pad,
                          mask_first=mask_first, q_scale=head_dim ** -0.5),
        grid=(B, T // tt),
        in_specs=[blk,
                  pl.BlockSpec((1, 6, D), lambda b_, i: (b_ + mod_row0, 0, 0)),
                  prev, blk, prev, blk,
                  _const_spec((npairs, 2 * chunk, pad + past + chunk)),
                  _const_spec((D, D)), _const_spec((D, D)),
                  _const_spec((1, D)), _const_spec((1, D))],
        out_specs=blk,
        out_shape=jax.ShapeDtypeStruct((B, T, D), F32),
        scratch_shapes=[pltpu.VMEM((pad + past + tt, D), BF16), pltpu.VMEM((pad + past + tt, D), BF16),
                        pltpu.VMEM((tt, D), BF16), pltpu.VMEM((tt, D), BF16)],
        compiler_params=_params(2),
        name="attn",
    )(x, mod, k_prev, k_cur, v_prev, v_cur, bias, wq, wo, lng.reshape(1, D), lnb.reshape(1, D))


def _trunk(x, mods, mods_kv, mod_row0, conv_state, kv_cache, p, bb, tt, chunk):
    B, T, D = x.shape
    depth = p['w_ada'].shape[0]
    n_a = p['w_pw1'].shape[0]
    alpha = (2.0 * depth) ** 0.25
    assert T >= CONV_WIDTH - 1 and T % tt == 0 and B % bb == 0 and mod_row0 % bb == 0
    new_conv = []
    kb = vb = kf = vf = None
    for l in range(depth):
        mod = mods[l]
        if l < n_a:
            x, tail = _conv(x, conv_state[l], mod, mod_row0, p['w_pw1'][l], p['b_pw1'][l],
                            p['w_dw'][l], p['b_dw'][l], p['conv_ln_g'][l], p['conv_ln_b'][l],
                            p['w_pw2'][l], p['b_pw2'][l], p['ln_g'][l, 0], p['ln_b'][l, 0],
                            bb, tt, alpha)
            new_conv.append(tail[:, HALO - (CONV_WIDTH - 1):])
        else:
            j = l - n_a
            if kv_cache is None:
                x = _attn(x, mod, mod_row0, kb, kb, vb, vb, p['rel_bias'][j], p['w_q'][j], p['w_o'][j],
                          p['ln_g'][l, 0], p['ln_b'][l, 0], WINDOW, chunk, True, alpha)
            else:
                x = _attn(x, mod, mod_row0, kv_cache[0], kb, kv_cache[1], vb, p['rel_bias'][j],
                          p['w_q'][j], p['w_o'][j], p['ln_g'][l, 0], p['ln_b'][l, 0],
                          T, chunk, False, alpha)
        i = l // 2
        if l % 2 == 0:
            x = _ffn(x, mod, mod_row0, p['w_ff_gate'], p['w_ff_up'], p['w_ff_down'], i,
                     p['ln_g'][l, 1], p['ln_b'][l, 1], bb, tt, alpha)
        elif bb == 1 and T % ROUTE_ROWS == 0:
            x = _moe_sparse(x, mod, mod_row0, p['w_router'][i], p['b_router'][i], p['w_e_gate'],
                            p['w_e_up'], p['w_e_down'], i, p['ln_g'][l, 1], p['ln_b'][l, 1], alpha)
        else:
            x = _moe(x, mod, mod_row0, p['w_router'][i], p['b_router'][i], p['w_e_gate'],
                     p['w_e_up'], p['w_e_down'], i, p['ln_g'][l, 1], p['ln_b'][l, 1],
                     bb, tt, alpha)
        if l == n_a - 1:
            kb, vb, kf, vf = _kv(x, mods_kv, mod_row0, p['w_k'], p['w_v'], bb, tt)
    return x, jnp.stack(new_conv), kf, vf


def kernel(x_prompt, x_sample, c_prompt, c_sample, state_conv, cache_k, cache_v, w_ada, b_ada, ln_g, ln_b, w_pw1, b_pw1, w_dw, b_dw, conv_ln_g, conv_ln_b, w_pw2, b_pw2, w_ada_kv, b_ada_kv, w_k, w_v, w_q, w_o, rel_bias, w_ff_gate, w_ff_up, w_ff_down, w_router, b_router, w_e_gate, w_e_up, w_e_down):
    B, T, D = x_prompt.shape
    Bs, Ts, _ = x_sample.shape
    H = N_HEADS
    assert Bs + B <= MOD_ROWS and D == H * (LANES // 2)
    depth = w_ada.shape[0]
    n_a = w_pw1.shape[0]

    bf = lambda w: w.astype(BF16)
    p = dict(w_ada=w_ada, ln_g=ln_g, ln_b=ln_b, w_pw1=bf(w_pw1), b_pw1=b_pw1, w_dw=w_dw, b_dw=b_dw,
             conv_ln_g=conv_ln_g, conv_ln_b=conv_ln_b, w_pw2=bf(w_pw2), b_pw2=b_pw2,
             w_k=bf(w_k), w_v=bf(w_v), w_q=bf(w_q), w_o=bf(w_o), rel_bias=rel_bias,
             w_ff_gate=bf(w_ff_gate), w_ff_up=bf(w_ff_up), w_ff_down=bf(w_ff_down),
             w_router=w_router, b_router=b_router,
             w_e_gate=bf(w_e_gate), w_e_up=bf(w_e_up), w_e_down=bf(w_e_down))

    c_all = jnp.concatenate([c_sample, c_prompt, jnp.zeros((MOD_ROWS - Bs - B, D), F32)], axis=0)
    mods = _ada(c_all, w_ada, b_ada, 1536).reshape(depth, MOD_ROWS, 6, D)
    mods_kv = _ada(c_all, w_ada_kv[None], b_ada_kv[None], 1024).reshape(MOD_ROWS, 2, D)

    pad_state = lambda s: jnp.pad(s, ((0, 0), (0, 0), (HALO - (CONV_WIDTH - 1), 0), (0, 0)))
    zero_state = jnp.zeros((n_a, B, HALO, D), F32)
    y_p, conv_p, kf_p, vf_p = _trunk(x_prompt, mods, mods_kv, Bs, zero_state, None, p, 1, WINDOW,
                                     ATTN_ROWS)

    cache = (cache_k.reshape(Bs, -1, D).astype(BF16), cache_v.reshape(Bs, -1, D).astype(BF16))
    y_s, conv_s, kf_s, vf_s = _trunk(x_sample, mods, mods_kv, 0, pad_state(state_conv), cache, p,
                                     Bs, Ts, Ts)

    L = cache_k.shape[1]
    heads = lambda a: a.reshape(a.shape[0], a.shape[1], H, D // H)
    k_s = jnp.concatenate([cache_k, heads(kf_s)], axis=1)[:, -L:]
    v_s = jnp.concatenate([cache_v, heads(vf_s)], axis=1)[:, -L:]
    return (y_p, y_s, conv_p, heads(kf_p), heads(vf_p), conv_s, k_s, v_s)
```

```python
import functools

import jax
import jax.numpy as jnp
from jax import lax
from jax.experimental import pallas as pl
from jax.experimental.pallas import tpu as pltpu

CHUNK = 64
WINDOW = 512
REL_CLIP = 128
CONV_WIDTH = 31
N_HEADS = 16
LN_EPS = 1e-5
NEG_INF = -1e30

LANES = 128
HALO = 32
MOD_ROWS = 16
VMEM_LIMIT = 56 * 1024 * 1024

F32 = jnp.float32
BF16 = jnp.bfloat16


def _layer_norm(v, g, b):
    mu = jnp.mean(v, axis=-1, keepdims=True)
    d = v - mu
    var = jnp.mean(d * d, axis=-1, keepdims=True)
    return d * lax.rsqrt(var + LN_EPS) * g + b


def _sigmoid(v):
    return 0.5 * jnp.tanh(0.5 * v) + 0.5


def _silu(v):
    return v * _sigmoid(v)


def _const_spec(shape):
    nd = len(shape)
    return pl.BlockSpec(shape, lambda *_: (0,) * nd, pipeline_mode=pl.Buffered(1))


def _params(n_grid):
    return pltpu.CompilerParams(dimension_semantics=("arbitrary",) * n_grid,
                                vmem_limit_bytes=VMEM_LIMIT)


def _ada_kernel(c_ref, w_ref, b_ref, o_ref):
    a = _silu(c_ref[...]).astype(BF16)
    o_ref[0] = jnp.dot(a, w_ref[0].astype(BF16), preferred_element_type=F32) + b_ref[0]


def _ada(c, w, b, tn):
    L, D, N = w.shape
    return pl.pallas_call(
        _ada_kernel,
        grid=(L, N // tn),
        in_specs=[pl.BlockSpec((MOD_ROWS, D), lambda l, j: (0, 0)),
                  pl.BlockSpec((1, D, tn), lambda l, j: (l, 0, j)),
                  pl.BlockSpec((1, 1, tn), lambda l, j: (l, 0, j))],
        out_specs=pl.BlockSpec((1, MOD_ROWS, tn), lambda l, j: (l, 0, j)),
        out_shape=jax.ShapeDtypeStruct((L, MOD_ROWS, N), F32),
        compiler_params=_params(2),
        name="ada",
    )(c, w, b.reshape(L, 1, N))


CONV_ROWS = 64
CONV_LANES = 256
SUBLANES = 8


def _conv_kernel(x_ref, st_ref, mod_ref, w1_ref, b1_ref, wdw_ref, bdw_ref, cg_ref, cb_ref,
                 w2_ref, b2_ref, lng_ref, lnb_ref, o_ref, tail_ref, ext_ref, sh_ref, y_ref, *, alpha):
    bb, tt, D = x_ref.shape
    i = pl.program_id(1)

    h = (x_ref[...] * (1 + mod_ref[:, 1:2, :]) + mod_ref[:, 0:1, :]).astype(BF16).reshape(bb * tt, D)
    z = jnp.dot(h, w1_ref[...], preferred_element_type=F32) + b1_ref[...]
    u = (z[:, :D] * _sigmoid(z[:, D:])).reshape(bb, tt, D)

    @pl.when(i == 0)
    def _():
        ext_ref[:, 0:HALO, :] = st_ref[...]

    @pl.when(i > 0)
    def _():
        ext_ref[:, 0:HALO, :] = ext_ref[:, tt:tt + HALO, :]

    ext_ref[:, HALO:, :] = u

    @pl.when(i == pl.num_programs(1) - 1)
    def _():
        tail_ref[...] = u[:, tt - HALO:, :]

    first = HALO - (CONV_WIDTH - 1)
    rows = min(CONV_ROWS, tt)
    for b in range(bb):
        for lb in range(D // CONV_LANES):
            lanes = slice(lb * CONV_LANES, (lb + 1) * CONV_LANES)
            sh_ref[0] = ext_ref[b, :, lanes]
            for s in range(1, SUBLANES):
                sh_ref[s, 0:tt + HALO - SUBLANES, :] = ext_ref[b, s:s + tt + HALO - SUBLANES, lanes]

            def chunk(c, carry, b=b, lanes=lanes):
                r0 = pl.multiple_of(c * rows, rows)
                acc = jnp.zeros((rows, CONV_LANES), F32)
                for j in range(CONV_WIDTH):
                    a, s = divmod(j + first, SUBLANES)
                    acc = acc + wdw_ref[j:j + 1, lanes] * sh_ref[s, pl.ds(r0 + SUBLANES * a, rows), :]
                y_ref[b, pl.ds(r0, rows), lanes] = acc + bdw_ref[:, lanes]
                return carry
            lax.fori_loop(0, tt // rows, chunk, 0)

    yn = _silu(_layer_norm(y_ref[...], cg_ref[...], cb_ref[...]))
    out = jnp.dot(yn.astype(BF16).reshape(bb * tt, D), w2_ref[...],
                  preferred_element_type=F32) + b2_ref[...]
    gate = mod_ref[:, 2:3, :]
    v = alpha * x_ref[...] + gate * out.reshape(bb, tt, D)
    o_ref[...] = _layer_norm(v, lng_ref[...], lnb_ref[...])


def _conv(x, state, mod, mod_row0, w1, b1, w_dw, b_dw, cg, cb, w2, b2, lng, lnb, bb, tt, alpha):
    B, T, D = x.shape
    mb = mod_row0 // bb
    wdw = jnp.pad(w_dw, ((0, HALO - CONV_WIDTH), (0, 0)))
    row = lambda a: a.reshape(1, -1)
    return pl.pallas_call(
        functools.partial(_conv_kernel, alpha=alpha),
        grid=(B // bb, T // tt),
        in_specs=[pl.BlockSpec((bb, tt, D), lambda b_, i: (b_, i, 0)),
                  pl.BlockSpec((bb, HALO, D), lambda b_, i: (b_, 0, 0)),
                  pl.BlockSpec((bb, 6, D), lambda b_, i: (b_ + mb, 0, 0)),
                  _const_spec((D, 2 * D)), _const_spec((1, 2 * D)),
                  _const_spec((HALO, D)),
                  _const_spec((1, D)), _const_spec((1, D)), _const_spec((1, D)),
                  _const_spec((D, D)),
                  _const_spec((1, D)), _const_spec((1, D)), _const_spec((1, D))],
        out_specs=[pl.BlockSpec((bb, tt, D), lambda b_, i: (b_, i, 0)),
                   pl.BlockSpec((bb, HALO, D), lambda b_, i: (b_, 0, 0))],
        out_shape=[jax.ShapeDtypeStruct((B, T, D), F32), jax.ShapeDtypeStruct((B, HALO, D), F32)],
        scratch_shapes=[pltpu.VMEM((bb, HALO + tt, D), F32),
                        pltpu.VMEM((SUBLANES, HALO + tt, CONV_LANES), F32),
                        pltpu.VMEM((bb, tt, D), F32)],
        compiler_params=_params(2),
        name="conv",
    )(x, state, mod, w1, row(b1), wdw, row(b_dw), row(cg), row(cb), w2, row(b2), row(lng), row(lnb))


def _ff_chunks(F, width):
    return [(f0, min(width, F - f0)) for f0 in range(0, F, width)]


def _swiglu(h, wg_ref, wu_ref, wd_ref, lead):
    F = wg_ref.shape[-1]
    acc = None
    for f0, fw in _ff_chunks(F, 1024):
        g = jnp.dot(h, wg_ref[lead + (slice(None), slice(f0, f0 + fw))], preferred_element_type=F32)
        u = jnp.dot(h, wu_ref[lead + (slice(None), slice(f0, f0 + fw))], preferred_element_type=F32)
        a = (_silu(g) * u).astype(BF16)
        d = jnp.dot(a, wd_ref[lead + (slice(f0, f0 + fw), slice(None))], preferred_element_type=F32)
        acc = d if acc is None else acc + d
    return acc


def _swiglu_packed(h, wgu_ref, wd_ref, lead):
    F = wd_ref.shape[-2]
    gu = jnp.dot(h, wgu_ref[lead], preferred_element_type=F32)
    a = (_silu(gu[:, :F]) * gu[:, F:]).astype(BF16)
    return jnp.dot(a, wd_ref[lead], preferred_element_type=F32)


def _ffn_kernel(x_ref, mod_ref, wg_ref, wu_ref, wd_ref, lng_ref, lnb_ref, o_ref, *, alpha):
    bb, tt, D = x_ref.shape
    shift = mod_ref[:, 3:4, :]
    scale = mod_ref[:, 4:5, :]
    gate = mod_ref[:, 5:6, :]
    x = x_ref[...]
    h = (x * (1 + scale) + shift).astype(BF16).reshape(bb * tt, D)
    f = _swiglu(h, wg_ref, wu_ref, wd_ref, (0,))
    v = alpha * x + gate * f.reshape(bb, tt, D)
    o_ref[...] = _layer_norm(v, lng_ref[...], lnb_ref[...])


def _layer_spec(shape, layer):
    nd = len(shape)
    return pl.BlockSpec((1,) + shape, lambda *_: (layer,) + (0,) * nd, pipeline_mode=pl.Buffered(1))


def _ffn(x, mod, mod_row0, wg, wu, wd, layer, lng, lnb, bb, tt, alpha):
    B, T, D = x.shape
    F = wg.shape[-1]
    mb = mod_row0 // bb
    return pl.pallas_call(
        functools.partial(_ffn_kernel, alpha=alpha),
        grid=(B // bb, T // tt),
        in_specs=[pl.BlockSpec((bb, tt, D), lambda b_, i: (b_, i, 0)),
                  pl.BlockSpec((bb, 6, D), lambda b_, i: (b_ + mb, 0, 0)),
                  _layer_spec((D, F), layer), _layer_spec((D, F), layer), _layer_spec((F, D), layer),
                  _const_spec((1, D)), _const_spec((1, D))],
        out_specs=pl.BlockSpec((bb, tt, D), lambda b_, i: (b_, i, 0)),
        out_shape=jax.ShapeDtypeStruct((B, T, D), F32),
        compiler_params=_params(2),
        name="ffn",
    )(x, mod, wg, wu, wd, lng.reshape(1, D), lnb.reshape(1, D))


def _router_top2(h32, wr_ref, br_ref, n_exp):
    M, D = h32.shape
    rows = []
    for e in range(n_exp):
        prod = h32 * wr_ref[e:e + 1, :]
        part = prod[:, 0:LANES]
        for c in range(1, D // LANES):
            part = part + prod[:, c * LANES:(c + 1) * LANES]
        rows.append(jnp.sum(part.T, axis=0, keepdims=True))
    logits = jnp.concatenate(rows, axis=0) + br_ref[...]
    ex = jnp.exp(logits - jnp.max(logits, axis=0, keepdims=True))
    p = ex / jnp.sum(ex, axis=0, keepdims=True)
    eid = lax.broadcasted_iota(jnp.int32, p.shape, 0)

    def best(v):
        top = jnp.max(v, axis=0, keepdims=True)
        return top, jnp.min(jnp.where(v == top, eid, n_exp), axis=0, keepdims=True)

    p1, i1 = best(p)
    p2, i2 = best(jnp.where(eid == i1, -1.0, p))
    tot = p1 + p2
    return i1, i2, p1 / tot, p2 / tot


def _to_columns(rows, M):
    pad = jnp.zeros((LANES - len(rows), M), F32)
    return jnp.concatenate(list(rows) + [pad], axis=0).T


def _router_gates(h32, wr_ref, br_ref, n_exp):
    i1, i2, g1, g2 = _router_top2(h32, wr_ref, br_ref, n_exp)
    rows = [jnp.where(i1 == e, g1, 0.0) + jnp.where(i2 == e, g2, 0.0) for e in range(n_exp)]
    return _to_columns(rows, h32.shape[0])


def _moe_kernel(x_ref, mod_ref, wr_ref, br_ref, wgu_ref, wd_ref, lng_ref, lnb_ref, o_ref,
                h_ref, gate_ref, acc_ref, *, alpha, n_exp):
    bb, tt, D = x_ref.shape
    e = pl.program_id(2)

    @pl.when(e == 0)
    def _():
        shift = mod_ref[:, 3:4, :]
        scale = mod_ref[:, 4:5, :]
        h32 = (x_ref[...] * (1 + scale) + shift).reshape(bb * tt, D)
        h_ref[...] = h32.astype(BF16)
        gates = _router_gates(h32, wr_ref, br_ref, n_exp)
        for k in range(n_exp):
            gate_ref[k] = gates[:, k:k + 1]
        acc_ref[...] = jnp.zeros_like(acc_ref)

    f = _swiglu_packed(h_ref[...], wgu_ref, wd_ref, (0, 0))
    acc_ref[...] += gate_ref[e] * f

    @pl.when(e == n_exp - 1)
    def _():
        v = alpha * x_ref[...] + mod_ref[:, 5:6, :] * acc_ref[...].reshape(bb, tt, D)
        o_ref[...] = _layer_norm(v, lng_ref[...], lnb_ref[...])


def _moe(x, mod, mod_row0, wr, br, wgu, wd, layer, lng, lnb, bb, tt, alpha):
    B, T, D = x.shape
    _, E, Fe, _ = wd.shape
    mb = mod_row0 // bb
    M = bb * tt
    return pl.pallas_call(
        functools.partial(_moe_kernel, alpha=alpha, n_exp=E),
        grid=(B // bb, T // tt, E),
        in_specs=[pl.BlockSpec((bb, tt, D), lambda b_, i, e: (b_, i, 0)),
                  pl.BlockSpec((bb, 6, D), lambda b_, i, e: (b_ + mb, 0, 0)),
                  _const_spec((E, D)), _const_spec((E, 1)),
                  pl.BlockSpec((1, 1, D, 2 * Fe), lambda b_, i, e: (layer, e, 0, 0)),
                  pl.BlockSpec((1, 1, Fe, D), lambda b_, i, e: (layer, e, 0, 0)),
                  _const_spec((1, D)), _const_spec((1, D))],
        out_specs=pl.BlockSpec((bb, tt, D), lambda b_, i, e: (b_, i, 0)),
        out_shape=jax.ShapeDtypeStruct((B, T, D), F32),
        scratch_shapes=[pltpu.VMEM((M, D), BF16), pltpu.VMEM((E, M, 1), F32),
                        pltpu.VMEM((M, D), F32)],
        compiler_params=_params(3),
        name="moe",
    )(x, mod, wr.T, br.reshape(E, 1), wgu, wd, lng.reshape(1, D), lnb.reshape(1, D))


ROUTE_ROWS = 512
EXPERT_ROWS = 512
META_I1, META_I2, META_G1, META_G2, META_R1, META_R2 = range(6)


def _route_kernel(x_ref, mod_ref, wr_ref, br_ref, meta_ref, cnt_ref, *, n_exp):
    _, tt, D = x_ref.shape
    first = jnp.logical_and(pl.program_id(0) == 0, pl.program_id(1) == 0)

    @pl.when(first)
    def _():
        cnt_ref[...] = jnp.zeros_like(cnt_ref)

    h32 = x_ref[0] * (1 + mod_ref[0, 4:5, :]) + mod_ref[0, 3:4, :]
    i1, i2, g1, g2 = _router_top2(h32, wr_ref, br_ref, n_exp)

    eid = lax.broadcasted_iota(jnp.int32, (n_exp, tt), 0)
    o1 = (eid == i1).astype(F32)
    o2 = (eid == i2).astype(F32)
    upper = (lax.broadcasted_iota(jnp.int32, (tt, tt), 0) <
             lax.broadcasted_iota(jnp.int32, (tt, tt), 1)).astype(BF16)
    c1 = jnp.dot(o1.astype(BF16), upper, preferred_element_type=F32)
    c2 = jnp.dot(o2.astype(BF16), upper, preferred_element_type=F32)
    base = cnt_ref[:, 0:1]
    tot1 = jnp.sum(o1, axis=1, keepdims=True)
    tot2 = jnp.sum(o2, axis=1, keepdims=True)
    r1 = jnp.sum(o1 * (c1 + base), axis=0, keepdims=True)
    r2 = jnp.sum(o2 * (c2 + (base + tot1)), axis=0, keepdims=True)
    cnt_ref[...] = jnp.broadcast_to(base + tot1 + tot2, cnt_ref.shape)

    rows = [i1.astype(F32), i2.astype(F32), g1, g2, r1, r2]
    meta_ref[0] = jnp.concatenate(rows + [jnp.zeros((SUBLANES - len(rows), tt), F32)], axis=0)


def _route(x, mod, mod_row0, wr, br, tt):
    B, T, D = x.shape
    E = wr.shape[1]
    nT = T // tt
    return pl.pallas_call(
        functools.partial(_route_kernel, n_exp=E),
        grid=(B, nT),
        in_specs=[pl.BlockSpec((1, tt, D), lambda b_, i: (b_, i, 0)),
                  pl.BlockSpec((1, 6, D), lambda b_, i: (b_ + mod_row0, 0, 0)),
                  _const_spec((E, D)), _const_spec((E, 1))],
        out_specs=[pl.BlockSpec((1, SUBLANES, tt), lambda b_, i: (b_ * nT + i, 0, 0)),
                   pl.BlockSpec((E, LANES), lambda b_, i: (0, 0))],
        out_shape=[jax.ShapeDtypeStruct((B * nT, SUBLANES, tt), F32),
                   jax.ShapeDtypeStruct((E, LANES), F32)],
        compiler_params=_params(2),
        name="route",
    )(x, mod, wr.T, br.reshape(E, 1))


def _row_copy(src_ref, src_row, dst_ref, dst_row, sem):
    return pltpu.make_async_copy(src_ref.at[pl.ds(src_row, 1)], dst_ref.at[pl.ds(dst_row, 1)], sem)


def _dispatch_kernel(pad_ref, pos_ref, x_ref, mod_ref, xs_ref, rows_ref, zero_ref, sems, *, n_exp):
    _, tb, D = x_ref.shape
    tm = zero_ref.shape[0]
    i = pl.program_id(0) * pl.num_programs(1) + pl.program_id(1)
    last = pl.num_programs(0) * pl.num_programs(1) - 1
    slot = i % 2
    pad_sem = sems.at[2]

    def tail_copy(t):
        return pltpu.make_async_copy(zero_ref, xs_ref.at[pl.ds(pl.multiple_of(t * tm, tm), tm)], pad_sem)

    @pl.when(i == 0)
    def _():
        zero_ref[...] = jnp.zeros_like(zero_ref)
        for e in range(n_exp):
            def fill(r, c):
                _row_copy(zero_ref, 0, xs_ref, r, pad_sem).start()
                return c
            lax.fori_loop(pad_ref[0, e], pad_ref[1, e], fill, 0)

        def fill_tail(t, c):
            tail_copy(t).start()
            return c
        lax.fori_loop(pad_ref[0, n_exp], pad_ref[1, n_exp], fill_tail, 0)

    h32 = x_ref[0] * (1 + mod_ref[0, 4:5, :]) + mod_ref[0, 3:4, :]
    rows_ref[slot] = h32.reshape(tb, D // LANES, LANES)

    def issue(t, c):
        _row_copy(rows_ref.at[slot], t, xs_ref, pos_ref[0, 0, t], sems.at[slot]).start()
        _row_copy(rows_ref.at[slot], t, xs_ref, pos_ref[0, 0, tb + t], sems.at[slot]).start()
        return c
    lax.fori_loop(0, tb, issue, 0, unroll=8)

    def wait_step(s):
        for _ in range(2):
            pltpu.make_async_copy(rows_ref.at[s], xs_ref.at[pl.ds(0, tb)], sems.at[s]).wait()

    @pl.when(i > 0)
    def _():
        wait_step(1 - slot)

    @pl.when(i == last)
    def _():
        wait_step(slot)
        for e in range(n_exp):
            def drain(r, c):
                _row_copy(zero_ref, 0, xs_ref, r, pad_sem).wait()
                return c
            lax.fori_loop(pad_ref[0, e], pad_ref[1, e], drain, 0)

        def drain_tail(t, c):
            tail_copy(t).wait()
            return c
        lax.fori_loop(pad_ref[0, n_exp], pad_ref[1, n_exp], drain_tail, 0)


def _dispatch(pads, pos, x, mod, mod_row0, n_rows, n_exp, tm):
    B, T, D = x.shape
    tb = pos.shape[2] // 2
    nT = T // tb
    S, L = D // LANES, LANES
    return pl.pallas_call(
        functools.partial(_dispatch_kernel, n_exp=n_exp),
        grid_spec=pltpu.PrefetchScalarGridSpec(
            num_scalar_prefetch=1,
            grid=(B, nT),
            in_specs=[pl.BlockSpec((1, 1, 2 * tb), lambda b_, i, pads_: (b_ * nT + i, 0, 0),
                                   memory_space=pltpu.SMEM),
                      pl.BlockSpec((1, tb, D), lambda b_, i, pads_: (b_, i, 0)),
                      pl.BlockSpec((1, 6, D), lambda b_, i, pads_: (b_ + mod_row0, 0, 0))],
            out_specs=pl.BlockSpec(memory_space=pl.ANY),
            scratch_shapes=[pltpu.VMEM((2, tb, S, L), F32), pltpu.VMEM((tm, S, L), F32),
                            pltpu.SemaphoreType.DMA((3,))]),
        out_shape=jax.ShapeDtypeStruct((n_rows, S, L), F32),
        compiler_params=_params(2),
        name="dispatch",
    )(pads, pos, x, mod)


def _experts_kernel(te_ref, nu_ref, xs_ref, wgu_ref, wd_ref, y_ref):
    tm, S, L = xs_ref.shape
    i = pl.program_id(0)

    @pl.when(i < nu_ref[0])
    def _():
        h = xs_ref[...].reshape(tm, S * L).astype(BF16)
        y_ref[...] = _swiglu_packed(h, wgu_ref, wd_ref, (0, 0)).reshape(tm, S, L)

    @pl.when(i >= nu_ref[0])
    def _():
        y_ref[...] = jnp.zeros_like(y_ref)


def _experts(tile_expert, n_used, xs, wgu, wd, layer, tm):
    n_rows, S, L = xs.shape
    _, E, Fe, D = wd.shape
    used = lambda i, te, nu: jnp.minimum(i, nu[0] - 1)
    return pl.pallas_call(
        _experts_kernel,
        grid_spec=pltpu.PrefetchScalarGridSpec(
            num_scalar_prefetch=2,
            grid=(n_rows // tm,),
            in_specs=[pl.BlockSpec((tm, S, L), lambda i, te, nu: (used(i, te, nu), 0, 0)),
                      pl.BlockSpec((1, 1, D, 2 * Fe), lambda i, te, nu: (layer, te[used(i, te, nu)], 0, 0)),
                      pl.BlockSpec((1, 1, Fe, D), lambda i, te, nu: (layer, te[used(i, te, nu)], 0, 0))],
            out_specs=pl.BlockSpec((tm, S, L), lambda i, te, nu: (i, 0, 0))),
        out_shape=jax.ShapeDtypeStruct((n_rows, S, L), F32),
        compiler_params=_params(1),
        name="experts",
    )(tile_expert, n_used, xs, wgu, wd)


def _combine_kernel(pcur_ref, pnext_ref, y_ref, x_ref, mod_ref, meta_ref, lng_ref, lnb_ref, o_ref,
                    buf_ref, sems, *, alpha):
    _, tc, D = x_ref.shape
    step = pl.program_id(0) * pl.num_programs(1) + pl.program_id(1)
    n_steps = pl.num_programs(0) * pl.num_programs(1)
    slot = step % 2

    def issue(p_ref, s):
        def body(t, c):
            _row_copy(y_ref, p_ref[0, 0, t], buf_ref.at[s], t, sems.at[s]).start()
            _row_copy(y_ref, p_ref[0, 0, tc + t], buf_ref.at[s], tc + t, sems.at[s]).start()
            return c
        lax.fori_loop(0, tc, body, 0, unroll=8)

    @pl.when(step == 0)
    def _():
        issue(pcur_ref, 0)

    @pl.when(step + 1 < n_steps)
    def _():
        issue(pnext_ref, 1 - slot)

    pltpu.make_async_copy(y_ref.at[pl.ds(0, 2 * tc)], buf_ref.at[slot], sems.at[slot]).wait()
    y1 = buf_ref[slot, 0:tc].reshape(tc, D)
    y2 = buf_ref[slot, tc:2 * tc].reshape(tc, D)
    gates = _to_columns([meta_ref[0, META_G1:META_G1 + 1, :], meta_ref[0, META_G2:META_G2 + 1, :]], tc)
    f = gates[:, 0:1] * y1 + gates[:, 1:2] * y2
    v = alpha * x_ref[0] + mod_ref[0, 5:6, :] * f
    o_ref[0] = _layer_norm(v, lng_ref[...], lnb_ref[...])


def _combine(pos, y, x, mod, mod_row0, meta, lng, lnb, alpha):
    B, T, D = x.shape
    nb, _, tc2 = pos.shape
    tc = tc2 // 2
    nT = T // tc
    S, L = y.shape[1:]
    blk = pl.BlockSpec((1, tc, D), lambda b_, i: (b_, i, 0))
    return pl.pallas_call(
        functools.partial(_combine_kernel, alpha=alpha),
        grid=(B, nT),
        in_specs=[pl.BlockSpec((1, 1, tc2), lambda b_, i: (b_ * nT + i, 0, 0), memory_space=pltpu.SMEM),
                  pl.BlockSpec((1, 1, tc2), lambda b_, i: (jnp.minimum(b_ * nT + i + 1, nb - 1), 0, 0),
                               memory_space=pltpu.SMEM),
                  pl.BlockSpec(memory_space=pl.ANY),
                  blk,
                  pl.BlockSpec((1, 6, D), lambda b_, i: (b_ + mod_row0, 0, 0)),
                  pl.BlockSpec((1, SUBLANES, tc), lambda b_, i: (b_ * nT + i, 0, 0)),
                  _const_spec((1, D)), _const_spec((1, D))],
        out_specs=blk,
        out_shape=jax.ShapeDtypeStruct((B, T, D), F32),
        scratch_shapes=[pltpu.VMEM((2, tc2, S, L), F32), pltpu.SemaphoreType.DMA((2,))],
        compiler_params=_params(2),
        name="combine",
    )(pos, pos, y, x, mod, meta, lng.reshape(1, D), lnb.reshape(1, D))


def _moe_sparse(x, mod, mod_row0, wr, br, wgu, wd, layer, lng, lnb, alpha):
    B, T, D = x.shape
    E = wd.shape[1]
    N = B * T
    tb, tm = ROUTE_ROWS, EXPERT_ROWS
    assert T % tb == 0 and E == SUBLANES
    meta, cnt = _route(x, mod, mod_row0, wr, br, tb)

    counts = cnt[:, 0].astype(jnp.int32)
    padded = (counts + tm - 1) // tm * tm
    ends = jnp.cumsum(padded)
    offs = ends - padded
    field = lambda k: meta[:, k, :].astype(jnp.int32)
    dest = lambda ids, ranks: jnp.sum(jnp.where(ids[..., None] == jnp.arange(E), offs, 0), -1) + ranks
    pos1 = dest(field(META_I1), field(META_R1))
    pos2 = dest(field(META_I2), field(META_R2))
    pos = jnp.concatenate([pos1, pos2], axis=1).reshape(N // tb, 1, 2 * tb)
    n_rows = 2 * N + E * tm
    n_tiles = n_rows // tm
    n_used = (ends[-1:] // tm).astype(jnp.int32)
    tile_expert = jnp.sum(jnp.arange(n_tiles)[:, None] * tm >= ends[None, :], axis=1)
    tile_expert = jnp.minimum(tile_expert, E - 1).astype(jnp.int32)
    pads = jnp.stack([jnp.concatenate([offs + counts, n_used]),
                      jnp.concatenate([ends, jnp.full((1,), n_tiles)])]).astype(jnp.int32)

    xs = _dispatch(pads, pos, x, mod, mod_row0, n_rows, E, tm)
    y = _experts(tile_expert, n_used, xs, wgu, wd, layer, tm)
    return _combine(pos, y, x, mod, mod_row0, meta, lng, lnb, alpha)


def _kv_kernel(x_ref, mod_ref, wk_ref, wv_ref, kb_ref, vb_ref, kf_ref, vf_ref):
    bb, tt, D = x_ref.shape
    i = pl.program_id(1)
    shift = mod_ref[:, 0:1, :]
    scale = mod_ref[:, 1:2, :]
    h = (x_ref[...] * (1 + scale) + shift).astype(BF16).reshape(bb * tt, D)
    k = jnp.dot(h, wk_ref[...], preferred_element_type=F32).reshape(bb, tt, D)
    v = jnp.dot(h, wv_ref[...], preferred_element_type=F32).reshape(bb, tt, D)
    kb_ref[...] = k.astype(BF16)
    vb_ref[...] = v.astype(BF16)

    @pl.when(i == pl.num_programs(1) - 1)
    def _():
        kf_ref[...] = k
        vf_ref[...] = v


def _kv(x, mod, mod_row0, wk, wv, bb, tt):
    B, T, D = x.shape
    mb = mod_row0 // bb
    blk = pl.BlockSpec((bb, tt, D), lambda b_, i: (b_, i, 0))
    last = pl.BlockSpec((bb, tt, D), lambda b_, i: (b_, 0, 0))
    return pl.pallas_call(
        _kv_kernel,
        grid=(B // bb, T // tt),
        in_specs=[blk,
                  pl.BlockSpec((bb, 2, D), lambda b_, i: (b_ + mb, 0, 0)),
                  _const_spec((D, D)), _const_spec((D, D))],
        out_specs=[blk, blk, last, last],
        out_shape=[jax.ShapeDtypeStruct((B, T, D), BF16), jax.ShapeDtypeStruct((B, T, D), BF16),
                   jax.ShapeDtypeStruct((B, tt, D), F32), jax.ShapeDtypeStruct((B, tt, D), F32)],
        compiler_params=_params(2),
        name="kv",
    )(x, mod, wk, wv)


ATTN_ROWS = 256


def _attn_kernel(x_ref, mod_ref, kp_ref, kc_ref, vp_ref, vc_ref, rel_ref, wq_ref, wo_ref,
                 lng_ref, lnb_ref, o_ref, bias_ref, kcat_ref, vcat_ref, q_ref, a_ref,
                 *, alpha, chunk, past, pad, mask_first, q_scale):
    _, tt, D = x_ref.shape
    n_chunks = tt // chunk
    win = pad + past + chunk
    head_dim = LANES // 2
    i = pl.program_id(1)

    @pl.when(jnp.logical_and(pl.program_id(0) == 0, i == 0))
    def _():
        key = lax.broadcasted_iota(jnp.int32, (chunk, win), 1) - (pad + past)
        start = lax.broadcasted_iota(jnp.int32, (chunk, win), 0) // CHUNK * CHUNK
        visible = jnp.logical_and(key >= start - past, key < start + CHUNK)
        for hd in range(rel_ref.shape[0]):
            vec = jnp.broadcast_to(rel_ref[hd:hd + 1, :], (chunk, rel_ref.shape[1]))
            table = pltpu.roll(vec, 0, 1, stride=1, stride_axis=0)[:, :win]
            bias_ref[hd // 2, hd % 2 * chunk:(hd % 2 + 1) * chunk, :] = jnp.where(visible, table, NEG_INF)

    if pad:
        kcat_ref[0:pad, :] = jnp.zeros((pad, D), BF16)
        vcat_ref[0:pad, :] = jnp.zeros((pad, D), BF16)
    kcat_ref[pad:pad + past, :] = kp_ref[0]
    kcat_ref[pad + past:, :] = kc_ref[0]
    vcat_ref[pad:pad + past, :] = vp_ref[0]
    vcat_ref[pad + past:, :] = vc_ref[0]

    x = x_ref[0]
    h = (x * (1 + mod_ref[0, 1:2, :]) + mod_ref[0, 0:1, :]).astype(BF16)
    q = jnp.dot(h, wq_ref[...], preferred_element_type=F32) * q_scale
    q_ref[...] = q.astype(BF16)

    row_head = lax.broadcasted_iota(jnp.int32, (2 * chunk, LANES), 0) >= chunk
    lane_head = lax.broadcasted_iota(jnp.int32, (2 * chunk, LANES), 1) >= head_dim
    own = row_head == lane_head
    out_lane_head0 = lax.broadcasted_iota(jnp.int32, (chunk, LANES), 1) < head_dim
    key_col = lax.broadcasted_iota(jnp.int32, (1, win), 1)

    def chunk_loop(masked):
        def one_chunk(g, carry):
            r0 = pl.multiple_of(g * chunk, chunk)
            for pr in range(D // LANES):
                lanes = slice(pr * LANES, (pr + 1) * LANES)
                qp = q_ref[pl.ds(r0, chunk), lanes]
                qbd = jnp.where(own, jnp.concatenate([qp, qp], axis=0), jnp.zeros((), BF16))
                kw = kcat_ref[pl.ds(r0, win), lanes]
                s = lax.dot_general(qbd, kw, (((1,), (1,)), ((), ())), preferred_element_type=F32)
                s = s + bias_ref[pr]
                if masked:
                    s = jnp.where(key_col + r0 >= pad + past, s, NEG_INF)
                m = jnp.max(s, axis=-1, keepdims=True)
                ex = jnp.exp(s - m)
                l = jnp.sum(ex, axis=-1, keepdims=True)
                vw = vcat_ref[pl.ds(r0, win), lanes]
                r = jnp.dot(ex.astype(BF16), vw, preferred_element_type=F32) * (1.0 / l)
                a_ref[pl.ds(r0, chunk), lanes] = jnp.where(out_lane_head0, r[:chunk], r[chunk:]).astype(BF16)
            return carry
        lax.fori_loop(0, n_chunks, one_chunk, 0)

    if mask_first:
        @pl.when(i == 0)
        def _():
            chunk_loop(True)

        @pl.when(i > 0)
        def _():
            chunk_loop(False)
    else:
        chunk_loop(False)

    out = jnp.dot(a_ref[...], wo_ref[...], preferred_element_type=F32)
    v = alpha * x + mod_ref[0, 2:3, :] * out
    o_ref[0] = _layer_norm(v, lng_ref[...], lnb_ref[...])


def _rel_vectors(table, past, chunk, pad):
    win = pad + past + chunk
    width = -(-(win + chunk - 1) // LANES) * LANES
    m = jnp.arange(width)
    m = jnp.where(m < win, m, m - width)
    d = past + pad - m
    return table[:, jnp.clip(d, -REL_CLIP, REL_CLIP) + REL_CLIP]


def _attn(x, mod, mod_row0, k_prev, k_cur, v_prev, v_cur, table, wq, wo, lng, lnb,
          tt, chunk, mask_first, alpha):
    B, T, D = x.shape
    past = WINDOW
    assert past % tt == 0 or tt % past == 0
    assert tt == past or T == tt
    pad = -(past + chunk) % LANES
    rel = _rel_vectors(table, past, chunk, pad)
    npairs = D // LANES
    head_dim = D // N_HEADS
    blk = pl.BlockSpec((1, tt, D), lambda b_, i: (b_, i, 0))
    prev = pl.BlockSpec((1, past, D), lambda b_, i: (b_, jnp.maximum(i - 1, 0), 0))
    return pl.pallas_call(
        functools.partial(_attn_kernel, alpha=alpha, chunk=chunk, past=past, pad=pad,
                          mask_first=mask_first, q_scale=head_dim ** -0.5),
        grid=(B, T // tt),
        in_specs=[blk,
                  pl.BlockSpec((1, 6, D), lambda b_, i: (b_ + mod_row0, 0, 0)),
                  prev, blk, prev, blk,
                  _const_spec(rel.shape),
                  _const_spec((D, D)), _const_spec((D, D)),
                  _const_spec((1, D)), _const_spec((1, D))],
        out_specs=blk,
        out_shape=jax.ShapeDtypeStruct((B, T, D), F32),
        scratch_shapes=[pltpu.VMEM((npairs, 2 * chunk, pad + past + chunk), F32),
                        pltpu.VMEM((pad + past + tt, D), BF16), pltpu.VMEM((pad + past + tt, D), BF16),
                        pltpu.VMEM((tt, D), BF16), pltpu.VMEM((tt, D), BF16)],
        compiler_params=_params(2),
        name="attn",
    )(x, mod, k_prev, k_cur, v_prev, v_cur, rel, wq, wo, lng.reshape(1, D), lnb.reshape(1, D))


def _trunk(x, mods, mods_kv, mod_row0, conv_state, kv_cache, p, bb, tt, chunk):
    B, T, D = x.shape
    depth = p['w_ada'].shape[0]
    n_a = p['w_pw1'].shape[0]
    alpha = (2.0 * depth) ** 0.25
    assert T >= CONV_WIDTH - 1 and T % tt == 0 and B % bb == 0 and mod_row0 % bb == 0
    new_conv = []
    kb = vb = kf = vf = None
    for l in range(depth):
        mod = mods[l]
        if l < n_a:
            x, tail = _conv(x, conv_state[l], mod, mod_row0, p['w_pw1'][l], p['b_pw1'][l],
                            p['w_dw'][l], p['b_dw'][l], p['conv_ln_g'][l], p['conv_ln_b'][l],
                            p['w_pw2'][l], p['b_pw2'][l], p['ln_g'][l, 0], p['ln_b'][l, 0],
                            bb, tt, alpha)
            new_conv.append(tail[:, HALO - (CONV_WIDTH - 1):])
        else:
            j = l - n_a
            if kv_cache is None:
                x = _attn(x, mod, mod_row0, kb, kb, vb, vb, p['rel_bias'][j], p['w_q'][j], p['w_o'][j],
                          p['ln_g'][l, 0], p['ln_b'][l, 0], WINDOW, chunk, True, alpha)
            else:
                x = _attn(x, mod, mod_row0, kv_cache[0], kb, kv_cache[1], vb, p['rel_bias'][j],
                          p['w_q'][j], p['w_o'][j], p['ln_g'][l, 0], p['ln_b'][l, 0],
                          T, chunk, False, alpha)
        i = l // 2
        if l % 2 == 0:
            x = _ffn(x, mod, mod_row0, p['w_ff_gate'], p['w_ff_up'], p['w_ff_down'], i,
                     p['ln_g'][l, 1], p['ln_b'][l, 1], bb, tt, alpha)
        elif bb == 1 and T % ROUTE_ROWS == 0:
            x = _moe_sparse(x, mod, mod_row0, p['w_router'][i], p['b_router'][i], p['w_e_gu'],
                            p['w_e_down'], i, p['ln_g'][l, 1], p['ln_b'][l, 1], alpha)
        else:
            x = _moe(x, mod, mod_row0, p['w_router'][i], p['b_router'][i], p['w_e_gu'],
                     p['w_e_down'], i, p['ln_g'][l, 1], p['ln_b'][l, 1],
                     bb, tt, alpha)
        if l == n_a - 1:
            kb, vb, kf, vf = _kv(x, mods_kv, mod_row0, p['w_k'], p['w_v'], bb, tt)
    return x, jnp.stack(new_conv), kf, vf


def kernel(x_prompt, x_sample, c_prompt, c_sample, state_conv, cache_k, cache_v, w_ada, b_ada, ln_g, ln_b, w_pw1, b_pw1, w_dw, b_dw, conv_ln_g, conv_ln_b, w_pw2, b_pw2, w_ada_kv, b_ada_kv, w_k, w_v, w_q, w_o, rel_bias, w_ff_gate, w_ff_up, w_ff_down, w_router, b_router, w_e_gate, w_e_up, w_e_down):
    B, T, D = x_prompt.shape
    Bs, Ts, _ = x_sample.shape
    H = N_HEADS
    assert Bs + B <= MOD_ROWS and D == H * (LANES // 2)
    depth = w_ada.shape[0]
    n_a = w_pw1.shape[0]

    bf = lambda w: w.astype(BF16)
    p = dict(w_ada=w_ada, ln_g=ln_g, ln_b=ln_b, w_pw1=bf(w_pw1), b_pw1=b_pw1, w_dw=w_dw, b_dw=b_dw,
             conv_ln_g=conv_ln_g, conv_ln_b=conv_ln_b, w_pw2=bf(w_pw2), b_pw2=b_pw2,
             w_k=bf(w_k), w_v=bf(w_v), w_q=bf(w_q), w_o=bf(w_o), rel_bias=rel_bias,
             w_ff_gate=bf(w_ff_gate), w_ff_up=bf(w_ff_up), w_ff_down=bf(w_ff_down),
             w_router=w_router, b_router=b_router,
             w_e_gu=bf(jnp.concatenate([w_e_gate, w_e_up], axis=-1)), w_e_down=bf(w_e_down))

    c_all = jnp.concatenate([c_sample, c_prompt, jnp.zeros((MOD_ROWS - Bs - B, D), F32)], axis=0)
    mods = _ada(c_all, w_ada, b_ada, 1536).reshape(depth, MOD_ROWS, 6, D)
    mods_kv = _ada(c_all, w_ada_kv[None], b_ada_kv[None], 1024).reshape(MOD_ROWS, 2, D)

    pad_state = lambda s: jnp.pad(s, ((0, 0), (0, 0), (HALO - (CONV_WIDTH - 1), 0), (0, 0)))
    zero_state = jnp.zeros((n_a, B, HALO, D), F32)
    y_p, conv_p, kf_p, vf_p = _trunk(x_prompt, mods, mods_kv, Bs, zero_state, None, p, 1, WINDOW,
                                     ATTN_ROWS)

    cache = (cache_k.reshape(Bs, -1, D).astype(BF16), cache_v.reshape(Bs, -1, D).astype(BF16))
    y_s, conv_s, kf_s, vf_s = _trunk(x_sample, mods, mods_kv, 0, pad_state(state_conv), cache, p,
                                     Bs, Ts, Ts)

    L = cache_k.shape[1]
    heads = lambda a: a.reshape(a.shape[0], a.shape[1], H, D // H)
    k_s = jnp.concatenate([cache_k, heads(kf_s)], axis=1)[:, -L:]
    v_s = jnp.concatenate([cache_v, heads(vf_s)], axis=1)[:, -L:]
    return (y_p, y_s, conv_p, heads(kf_p), heads(vf_p), conv_s, k_s, v_s)
```

```python
import functools

import jax
import jax.numpy as jnp
from jax import lax
from jax.experimental import pallas as pl
from jax.experimental.pallas import tpu as pltpu

CHUNK = 64
WINDOW = 512
REL_CLIP = 128
CONV_WIDTH = 31
N_HEADS = 16
LN_EPS = 1e-5
NEG_INF = -1e30

LANES = 128
HALO = 32
MOD_ROWS = 16
VMEM_LIMIT = 56 * 1024 * 1024

F32 = jnp.float32
BF16 = jnp.bfloat16


def _layer_norm(v, g, b):
    mu = jnp.mean(v, axis=-1, keepdims=True)
    d = v - mu
    var = jnp.mean(d * d, axis=-1, keepdims=True)
    return d * lax.rsqrt(var + LN_EPS) * g + b


def _sigmoid(v):
    return 0.5 * jnp.tanh(0.5 * v) + 0.5


def _silu(v):
    return v * _sigmoid(v)


def _const_spec(shape):
    nd = len(shape)
    return pl.BlockSpec(shape, lambda *_: (0,) * nd, pipeline_mode=pl.Buffered(1))


def _params(n_grid):
    return pltpu.CompilerParams(dimension_semantics=("arbitrary",) * n_grid,
                                vmem_limit_bytes=VMEM_LIMIT)


def _ada_kernel(c_ref, w_ref, b_ref, o_ref):
    a = _silu(c_ref[...]).astype(BF16)
    o_ref[0] = jnp.dot(a, w_ref[0].astype(BF16), preferred_element_type=F32) + b_ref[0]


def _ada(c, w, b, tn):
    L, D, N = w.shape
    return pl.pallas_call(
        _ada_kernel,
        grid=(L, N // tn),
        in_specs=[pl.BlockSpec((MOD_ROWS, D), lambda l, j: (0, 0)),
                  pl.BlockSpec((1, D, tn), lambda l, j: (l, 0, j)),
                  pl.BlockSpec((1, 1, tn), lambda l, j: (l, 0, j))],
        out_specs=pl.BlockSpec((1, MOD_ROWS, tn), lambda l, j: (l, 0, j)),
        out_shape=jax.ShapeDtypeStruct((L, MOD_ROWS, N), F32),
        compiler_params=_params(2),
        name="ada",
    )(c, w, b.reshape(L, 1, N))


CONV_ROWS = 64
CONV_LANES = 256
SUBLANES = 8


def _conv_kernel(x_ref, st_ref, mod_ref, w1_ref, b1_ref, wdw_ref, bdw_ref, cg_ref, cb_ref,
                 w2_ref, b2_ref, lng_ref, lnb_ref, o_ref, tail_ref, ext_ref, sh_ref, y_ref, *, alpha):
    bb, tt, D = x_ref.shape
    i = pl.program_id(1)

    h = (x_ref[...] * (1 + mod_ref[:, 1:2, :]) + mod_ref[:, 0:1, :]).astype(BF16).reshape(bb * tt, D)
    z = jnp.dot(h, w1_ref[...], preferred_element_type=F32) + b1_ref[...]
    u = (z[:, :D] * _sigmoid(z[:, D:])).reshape(bb, tt, D)

    @pl.when(i == 0)
    def _():
        ext_ref[:, 0:HALO, :] = st_ref[...]

    @pl.when(i > 0)
    def _():
        ext_ref[:, 0:HALO, :] = ext_ref[:, tt:tt + HALO, :]

    ext_ref[:, HALO:, :] = u

    @pl.when(i == pl.num_programs(1) - 1)
    def _():
        tail_ref[...] = u[:, tt - HALO:, :]

    first = HALO - (CONV_WIDTH - 1)
    rows = min(CONV_ROWS, tt)
    for b in range(bb):
        for lb in range(D // CONV_LANES):
            lanes = slice(lb * CONV_LANES, (lb + 1) * CONV_LANES)
            sh_ref[0] = ext_ref[b, :, lanes]
            for s in range(1, SUBLANES):
                sh_ref[s, 0:tt + HALO - SUBLANES, :] = ext_ref[b, s:s + tt + HALO - SUBLANES, lanes]

            def chunk(c, carry, b=b, lanes=lanes):
                r0 = pl.multiple_of(c * rows, rows)
                acc = jnp.zeros((rows, CONV_LANES), F32)
                for j in range(CONV_WIDTH):
                    a, s = divmod(j + first, SUBLANES)
                    acc = acc + wdw_ref[j:j + 1, lanes] * sh_ref[s, pl.ds(r0 + SUBLANES * a, rows), :]
                y_ref[b, pl.ds(r0, rows), lanes] = acc + bdw_ref[:, lanes]
                return carry
            lax.fori_loop(0, tt // rows, chunk, 0)

    yn = _silu(_layer_norm(y_ref[...], cg_ref[...], cb_ref[...]))
    out = jnp.dot(yn.astype(BF16).reshape(bb * tt, D), w2_ref[...],
                  preferred_element_type=F32) + b2_ref[...]
    gate = mod_ref[:, 2:3, :]
    v = alpha * x_ref[...] + gate * out.reshape(bb, tt, D)
    o_ref[...] = _layer_norm(v, lng_ref[...], lnb_ref[...])


def _conv(x, state, mod, mod_row0, w1, b1, w_dw, b_dw, cg, cb, w2, b2, lng, lnb, bb, tt, alpha):
    B, T, D = x.shape
    mb = mod_row0 // bb
    wdw = jnp.pad(w_dw, ((0, HALO - CONV_WIDTH), (0, 0)))
    row = lambda a: a.reshape(1, -1)
    return pl.pallas_call(
        functools.partial(_conv_kernel, alpha=alpha),
        grid=(B // bb, T // tt),
        in_specs=[pl.BlockSpec((bb, tt, D), lambda b_, i: (b_, i, 0)),
                  pl.BlockSpec((bb, HALO, D), lambda b_, i: (b_, 0, 0)),
                  pl.BlockSpec((bb, 6, D), lambda b_, i: (b_ + mb, 0, 0)),
                  _const_spec((D, 2 * D)), _const_spec((1, 2 * D)),
                  _const_spec((HALO, D)),
                  _const_spec((1, D)), _const_spec((1, D)), _const_spec((1, D)),
                  _const_spec((D, D)),
                  _const_spec((1, D)), _const_spec((1, D)), _const_spec((1, D))],
        out_specs=[pl.BlockSpec((bb, tt, D), lambda b_, i: (b_, i, 0)),
                   pl.BlockSpec((bb, HALO, D), lambda b_, i: (b_, 0, 0))],
        out_shape=[jax.ShapeDtypeStruct((B, T, D), F32), jax.ShapeDtypeStruct((B, HALO, D), F32)],
        scratch_shapes=[pltpu.VMEM((bb, HALO + tt, D), F32),
                        pltpu.VMEM((SUBLANES, HALO + tt, CONV_LANES), F32),
                        pltpu.VMEM((bb, tt, D), F32)],
        compiler_params=_params(2),
        name="conv",
    )(x, state, mod, w1, row(b1), wdw, row(b_dw), row(cg), row(cb), w2, row(b2), row(lng), row(lnb))


def _ff_chunks(F, width):
    return [(f0, min(width, F - f0)) for f0 in range(0, F, width)]


def _swiglu(h, wg_ref, wu_ref, wd_ref, lead):
    F = wg_ref.shape[-1]
    acc = None
    for f0, fw in _ff_chunks(F, 1024):
        g = jnp.dot(h, wg_ref[lead + (slice(None), slice(f0, f0 + fw))], preferred_element_type=F32)
        u = jnp.dot(h, wu_ref[lead + (slice(None), slice(f0, f0 + fw))], preferred_element_type=F32)
        a = (_silu(g) * u).astype(BF16)
        d = jnp.dot(a, wd_ref[lead + (slice(f0, f0 + fw), slice(None))], preferred_element_type=F32)
        acc = d if acc is None else acc + d
    return acc


def _swiglu_packed(h, wgu_ref, wd_ref, lead):
    F = wd_ref.shape[-2]
    gu = jnp.dot(h, wgu_ref[lead], preferred_element_type=F32)
    a = (_silu(gu[:, :F]) * gu[:, F:]).astype(BF16)
    return jnp.dot(a, wd_ref[lead], preferred_element_type=F32)


def _ffn_kernel(x_ref, mod_ref, wg_ref, wu_ref, wd_ref, lng_ref, lnb_ref, o_ref, *, alpha):
    bb, tt, D = x_ref.shape
    shift = mod_ref[:, 3:4, :]
    scale = mod_ref[:, 4:5, :]
    gate = mod_ref[:, 5:6, :]
    x = x_ref[...]
    h = (x * (1 + scale) + shift).astype(BF16).reshape(bb * tt, D)
    f = _swiglu(h, wg_ref, wu_ref, wd_ref, (0,))
    v = alpha * x + gate * f.reshape(bb, tt, D)
    o_ref[...] = _layer_norm(v, lng_ref[...], lnb_ref[...])


def _layer_spec(shape, layer):
    nd = len(shape)
    return pl.BlockSpec((1,) + shape, lambda *_: (layer,) + (0,) * nd, pipeline_mode=pl.Buffered(1))


def _ffn(x, mod, mod_row0, wg, wu, wd, layer, lng, lnb, bb, tt, alpha):
    B, T, D = x.shape
    F = wg.shape[-1]
    mb = mod_row0 // bb
    return pl.pallas_call(
        functools.partial(_ffn_kernel, alpha=alpha),
        grid=(B // bb, T // tt),
        in_specs=[pl.BlockSpec((bb, tt, D), lambda b_, i: (b_, i, 0)),
                  pl.BlockSpec((bb, 6, D), lambda b_, i: (b_ + mb, 0, 0)),
                  _layer_spec((D, F), layer), _layer_spec((D, F), layer), _layer_spec((F, D), layer),
                  _const_spec((1, D)), _const_spec((1, D))],
        out_specs=pl.BlockSpec((bb, tt, D), lambda b_, i: (b_, i, 0)),
        out_shape=jax.ShapeDtypeStruct((B, T, D), F32),
        compiler_params=_params(2),
        name="ffn",
    )(x, mod, wg, wu, wd, lng.reshape(1, D), lnb.reshape(1, D))


def _router_top2(h32, wr_ref, br_ref, n_exp):
    M, D = h32.shape
    rows = []
    for e in range(n_exp):
        prod = h32 * wr_ref[e:e + 1, :]
        part = prod[:, 0:LANES]
        for c in range(1, D // LANES):
            part = part + prod[:, c * LANES:(c + 1) * LANES]
        rows.append(jnp.sum(part.T, axis=0, keepdims=True))
    logits = jnp.concatenate(rows, axis=0) + br_ref[...]
    ex = jnp.exp(logits - jnp.max(logits, axis=0, keepdims=True))
    p = ex / jnp.sum(ex, axis=0, keepdims=True)
    eid = lax.broadcasted_iota(jnp.int32, p.shape, 0)

    def best(v):
        top = jnp.max(v, axis=0, keepdims=True)
        return top, jnp.min(jnp.where(v == top, eid, n_exp), axis=0, keepdims=True)

    p1, i1 = best(p)
    p2, i2 = best(jnp.where(eid == i1, -1.0, p))
    tot = p1 + p2
    return i1, i2, p1 / tot, p2 / tot


def _to_columns(rows, M):
    pad = jnp.zeros((LANES - len(rows), M), F32)
    return jnp.concatenate(list(rows) + [pad], axis=0).T


def _router_gates(h32, wr_ref, br_ref, n_exp):
    i1, i2, g1, g2 = _router_top2(h32, wr_ref, br_ref, n_exp)
    rows = [jnp.where(i1 == e, g1, 0.0) + jnp.where(i2 == e, g2, 0.0) for e in range(n_exp)]
    return _to_columns(rows, h32.shape[0])


def _moe_kernel(x_ref, mod_ref, wr_ref, br_ref, wgu_ref, wd_ref, lng_ref, lnb_ref, o_ref,
                h_ref, gate_ref, acc_ref, *, alpha, n_exp):
    bb, tt, D = x_ref.shape
    e = pl.program_id(2)

    @pl.when(e == 0)
    def _():
        shift = mod_ref[:, 3:4, :]
        scale = mod_ref[:, 4:5, :]
        h32 = (x_ref[...] * (1 + scale) + shift).reshape(bb * tt, D)
        h_ref[...] = h32.astype(BF16)
        gates = _router_gates(h32, wr_ref, br_ref, n_exp)
        for k in range(n_exp):
            gate_ref[k] = gates[:, k:k + 1]
        acc_ref[...] = jnp.zeros_like(acc_ref)

    f = _swiglu_packed(h_ref[...], wgu_ref, wd_ref, (0, 0))
    acc_ref[...] += gate_ref[e] * f

    @pl.when(e == n_exp - 1)
    def _():
        v = alpha * x_ref[...] + mod_ref[:, 5:6, :] * acc_ref[...].reshape(bb, tt, D)
        o_ref[...] = _layer_norm(v, lng_ref[...], lnb_ref[...])


def _moe(x, mod, mod_row0, wr, br, wgu, wd, layer, lng, lnb, bb, tt, alpha):
    B, T, D = x.shape
    _, E, Fe, _ = wd.shape
    mb = mod_row0 // bb
    M = bb * tt
    return pl.pallas_call(
        functools.partial(_moe_kernel, alpha=alpha, n_exp=E),
        grid=(B // bb, T // tt, E),
        in_specs=[pl.BlockSpec((bb, tt, D), lambda b_, i, e: (b_, i, 0)),
                  pl.BlockSpec((bb, 6, D), lambda b_, i, e: (b_ + mb, 0, 0)),
                  _const_spec((E, D)), _const_spec((E, 1)),
                  pl.BlockSpec((1, 1, D, 2 * Fe), lambda b_, i, e: (layer, e, 0, 0)),
                  pl.BlockSpec((1, 1, Fe, D), lambda b_, i, e: (layer, e, 0, 0)),
                  _const_spec((1, D)), _const_spec((1, D))],
        out_specs=pl.BlockSpec((bb, tt, D), lambda b_, i, e: (b_, i, 0)),
        out_shape=jax.ShapeDtypeStruct((B, T, D), F32),
        scratch_shapes=[pltpu.VMEM((M, D), BF16), pltpu.VMEM((E, M, 1), F32),
                        pltpu.VMEM((M, D), F32)],
        compiler_params=_params(3),
        name="moe",
    )(x, mod, wr.T, br.reshape(E, 1), wgu, wd, lng.reshape(1, D), lnb.reshape(1, D))


ROUTE_ROWS = 512
EXPERT_ROWS = 512
META_I1, META_I2, META_G1, META_G2, META_R1, META_R2 = range(6)


def _route_kernel(x_ref, mod_ref, wr_ref, br_ref, h_ref, meta_ref, cnt_ref, *, n_exp):
    _, tt, D = x_ref.shape
    first = jnp.logical_and(pl.program_id(0) == 0, pl.program_id(1) == 0)

    @pl.when(first)
    def _():
        cnt_ref[...] = jnp.zeros_like(cnt_ref)

    h32 = x_ref[0] * (1 + mod_ref[0, 4:5, :]) + mod_ref[0, 3:4, :]
    h_ref[...] = h32.reshape(tt, D // LANES, LANES)
    i1, i2, g1, g2 = _router_top2(h32, wr_ref, br_ref, n_exp)

    eid = lax.broadcasted_iota(jnp.int32, (n_exp, tt), 0)
    o1 = (eid == i1).astype(F32)
    o2 = (eid == i2).astype(F32)
    upper = (lax.broadcasted_iota(jnp.int32, (tt, tt), 0) <
             lax.broadcasted_iota(jnp.int32, (tt, tt), 1)).astype(BF16)
    c1 = jnp.dot(o1.astype(BF16), upper, preferred_element_type=F32)
    c2 = jnp.dot(o2.astype(BF16), upper, preferred_element_type=F32)
    base = cnt_ref[:, 0:1]
    tot1 = jnp.sum(o1, axis=1, keepdims=True)
    tot2 = jnp.sum(o2, axis=1, keepdims=True)
    r1 = jnp.sum(o1 * (c1 + base), axis=0, keepdims=True)
    r2 = jnp.sum(o2 * (c2 + (base + tot1)), axis=0, keepdims=True)
    cnt_ref[...] = jnp.broadcast_to(base + tot1 + tot2, cnt_ref.shape)

    rows = [i1.astype(F32), i2.astype(F32), g1, g2, r1, r2]
    meta_ref[0] = jnp.concatenate(rows + [jnp.zeros((SUBLANES - len(rows), tt), F32)], axis=0)


def _route(x, mod, mod_row0, wr, br, tt):
    B, T, D = x.shape
    E = wr.shape[1]
    nT = T // tt
    return pl.pallas_call(
        functools.partial(_route_kernel, n_exp=E),
        grid=(B, nT),
        in_specs=[pl.BlockSpec((1, tt, D), lambda b_, i: (b_, i, 0)),
                  pl.BlockSpec((1, 6, D), lambda b_, i: (b_ + mod_row0, 0, 0)),
                  _const_spec((E, D)), _const_spec((E, 1))],
        out_specs=[pl.BlockSpec((tt, D // LANES, LANES), lambda b_, i: (b_ * nT + i, 0, 0)),
                   pl.BlockSpec((1, SUBLANES, tt), lambda b_, i: (b_ * nT + i, 0, 0)),
                   pl.BlockSpec((E, LANES), lambda b_, i: (0, 0))],
        out_shape=[jax.ShapeDtypeStruct((B * T, D // LANES, LANES), F32),
                   jax.ShapeDtypeStruct((B * nT, SUBLANES, tt), F32),
                   jax.ShapeDtypeStruct((E, LANES), F32)],
        compiler_params=_params(2),
        name="route",
    )(x, mod, wr.T, br.reshape(E, 1))


def _row_copy(src_ref, src_row, dst_ref, dst_row, sem):
    return pltpu.make_async_copy(src_ref.at[pl.ds(src_row, 1)], dst_ref.at[pl.ds(dst_row, 1)], sem)


def _dispatch_kernel(pad_ref, pos_ref, x_ref, mod_ref, xs_ref, rows_ref, zero_ref, sems, *, n_exp):
    _, tb, D = x_ref.shape
    tm = zero_ref.shape[0]
    i = pl.program_id(0) * pl.num_programs(1) + pl.program_id(1)
    last = pl.num_programs(0) * pl.num_programs(1) - 1
    slot = i % 2
    pad_sem = sems.at[2]

    def tail_copy(t):
        return pltpu.make_async_copy(zero_ref, xs_ref.at[pl.ds(pl.multiple_of(t * tm, tm), tm)], pad_sem)

    @pl.when(i == 0)
    def _():
        zero_ref[...] = jnp.zeros_like(zero_ref)
        for e in range(n_exp):
            def fill(r, c):
                _row_copy(zero_ref, 0, xs_ref, r, pad_sem).start()
                return c
            lax.fori_loop(pad_ref[0, e], pad_ref[1, e], fill, 0)

        def fill_tail(t, c):
            tail_copy(t).start()
            return c
        lax.fori_loop(pad_ref[0, n_exp], pad_ref[1, n_exp], fill_tail, 0)

    h32 = x_ref[0] * (1 + mod_ref[0, 4:5, :]) + mod_ref[0, 3:4, :]
    rows_ref[slot] = h32.reshape(tb, D // LANES, LANES)

    def issue(t, c):
        _row_copy(rows_ref.at[slot], t, xs_ref, pos_ref[0, 0, t], sems.at[slot]).start()
        _row_copy(rows_ref.at[slot], t, xs_ref, pos_ref[0, 0, tb + t], sems.at[slot]).start()
        return c
    lax.fori_loop(0, tb, issue, 0, unroll=8)

    def wait_step(s):
        for _ in range(2):
            pltpu.make_async_copy(rows_ref.at[s], xs_ref.at[pl.ds(0, tb)], sems.at[s]).wait()

    @pl.when(i > 0)
    def _():
        wait_step(1 - slot)

    @pl.when(i == last)
    def _():
        wait_step(slot)
        for e in range(n_exp):
            def drain(r, c):
                _row_copy(zero_ref, 0, xs_ref, r, pad_sem).wait()
                return c
            lax.fori_loop(pad_ref[0, e], pad_ref[1, e], drain, 0)

        def drain_tail(t, c):
            tail_copy(t).wait()
            return c
        lax.fori_loop(pad_ref[0, n_exp], pad_ref[1, n_exp], drain_tail, 0)


def _dispatch(pads, pos, x, mod, mod_row0, n_rows, n_exp, tm):
    B, T, D = x.shape
    tb = pos.shape[2] // 2
    nT = T // tb
    S, L = D // LANES, LANES
    return pl.pallas_call(
        functools.partial(_dispatch_kernel, n_exp=n_exp),
        grid_spec=pltpu.PrefetchScalarGridSpec(
            num_scalar_prefetch=1,
            grid=(B, nT),
            in_specs=[pl.BlockSpec((1, 1, 2 * tb), lambda b_, i, pads_: (b_ * nT + i, 0, 0),
                                   memory_space=pltpu.SMEM),
                      pl.BlockSpec((1, tb, D), lambda b_, i, pads_: (b_, i, 0)),
                      pl.BlockSpec((1, 6, D), lambda b_, i, pads_: (b_ + mod_row0, 0, 0))],
            out_specs=pl.BlockSpec(memory_space=pl.ANY),
            scratch_shapes=[pltpu.VMEM((2, tb, S, L), F32), pltpu.VMEM((tm, S, L), F32),
                            pltpu.SemaphoreType.DMA((3,))]),
        out_shape=jax.ShapeDtypeStruct((n_rows, S, L), F32),
        compiler_params=_params(2),
        name="dispatch",
    )(pads, pos, x, mod)


def _experts_kernel(te_ref, nu_ref, xs_ref, wgu_ref, wd_ref, y_ref):
    tm, S, L = xs_ref.shape
    i = pl.program_id(0)

    @pl.when(i < nu_ref[0])
    def _():
        h = xs_ref[...].reshape(tm, S * L).astype(BF16)
        y_ref[...] = _swiglu_packed(h, wgu_ref, wd_ref, (0, 0)).reshape(tm, S, L)

    @pl.when(i >= nu_ref[0])
    def _():
        y_ref[...] = jnp.zeros_like(y_ref)


def _experts(tile_expert, n_used, xs, wgu, wd, layer, tm):
    n_rows, S, L = xs.shape
    _, E, Fe, D = wd.shape
    used = lambda i, te, nu: jnp.minimum(i, nu[0] - 1)
    return pl.pallas_call(
        _experts_kernel,
        grid_spec=pltpu.PrefetchScalarGridSpec(
            num_scalar_prefetch=2,
            grid=(n_rows // tm,),
            in_specs=[pl.BlockSpec((tm, S, L), lambda i, te, nu: (used(i, te, nu), 0, 0)),
                      pl.BlockSpec((1, 1, D, 2 * Fe), lambda i, te, nu: (layer, te[used(i, te, nu)], 0, 0)),
                      pl.BlockSpec((1, 1, Fe, D), lambda i, te, nu: (layer, te[used(i, te, nu)], 0, 0))],
            out_specs=pl.BlockSpec((tm, S, L), lambda i, te, nu: (i, 0, 0))),
        out_shape=jax.ShapeDtypeStruct((n_rows, S, L), F32),
        compiler_params=_params(1),
        name="experts",
    )(tile_expert, n_used, xs, wgu, wd)


def _combine_kernel(pcur_ref, pnext_ref, y_ref, x_ref, mod_ref, meta_ref, lng_ref, lnb_ref, o_ref,
                    buf_ref, sems, *, alpha):
    _, tc, D = x_ref.shape
    step = pl.program_id(0) * pl.num_programs(1) + pl.program_id(1)
    n_steps = pl.num_programs(0) * pl.num_programs(1)
    slot = step % 2

    def issue(p_ref, s):
        def body(t, c):
            _row_copy(y_ref, p_ref[0, 0, t], buf_ref.at[s], t, sems.at[s]).start()
            _row_copy(y_ref, p_ref[0, 0, tc + t], buf_ref.at[s], tc + t, sems.at[s]).start()
            return c
        lax.fori_loop(0, tc, body, 0, unroll=8)

    @pl.when(step == 0)
    def _():
        issue(pcur_ref, 0)

    @pl.when(step + 1 < n_steps)
    def _():
        issue(pnext_ref, 1 - slot)

    pltpu.make_async_copy(y_ref.at[pl.ds(0, 2 * tc)], buf_ref.at[slot], sems.at[slot]).wait()
    y1 = buf_ref[slot, 0:tc].reshape(tc, D)
    y2 = buf_ref[slot, tc:2 * tc].reshape(tc, D)
    gates = _to_columns([meta_ref[0, META_G1:META_G1 + 1, :], meta_ref[0, META_G2:META_G2 + 1, :]], tc)
    f = gates[:, 0:1] * y1 + gates[:, 1:2] * y2
    v = alpha * x_ref[0] + mod_ref[0, 5:6, :] * f
    o_ref[0] = _layer_norm(v, lng_ref[...], lnb_ref[...])


def _combine(pos, y, x, mod, mod_row0, meta, lng, lnb, alpha):
    B, T, D = x.shape
    nb, _, tc2 = pos.shape
    tc = tc2 // 2
    nT = T // tc
    S, L = y.shape[1:]
    blk = pl.BlockSpec((1, tc, D), lambda b_, i: (b_, i, 0))
    return pl.pallas_call(
        functools.partial(_combine_kernel, alpha=alpha),
        grid=(B, nT),
        in_specs=[pl.BlockSpec((1, 1, tc2), lambda b_, i: (b_ * nT + i, 0, 0), memory_space=pltpu.SMEM),
                  pl.BlockSpec((1, 1, tc2), lambda b_, i: (jnp.minimum(b_ * nT + i + 1, nb - 1), 0, 0),
                               memory_space=pltpu.SMEM),
                  pl.BlockSpec(memory_space=pl.ANY),
                  blk,
                  pl.BlockSpec((1, 6, D), lambda b_, i: (b_ + mod_row0, 0, 0)),
                  pl.BlockSpec((1, SUBLANES, tc), lambda b_, i: (b_ * nT + i, 0, 0)),
                  _const_spec((1, D)), _const_spec((1, D))],
        out_specs=blk,
        out_shape=jax.ShapeDtypeStruct((B, T, D), F32),
        scratch_shapes=[pltpu.VMEM((2, tc2, S, L), F32), pltpu.SemaphoreType.DMA((2,))],
        compiler_params=_params(2),
        name="combine",
    )(pos, pos, y, x, mod, meta, lng.reshape(1, D), lnb.reshape(1, D))


def _invert_kernel(pad_ref, pos_ref, dst_ref, *, n_exp, n_tokens, tm):
    tb = pos_ref.shape[2] // 2
    i = pl.program_id(0)

    @pl.when(i == 0)
    def _():
        for e in range(n_exp):
            def fill(r, c, e=e):
                dst_ref[r] = 2 * n_tokens + e * tm + (r - pad_ref[0, e])
                return c
            lax.fori_loop(pad_ref[0, e], pad_ref[1, e], fill, 0)

        def fill_tail(r, c):
            dst_ref[r] = 2 * n_tokens
            return c
        lax.fori_loop(pad_ref[0, n_exp], pad_ref[1, n_exp], fill_tail, 0)

    def body(t, c):
        tok = i * tb + t
        dst_ref[pos_ref[0, 0, t]] = tok
        dst_ref[pos_ref[0, 0, tb + t]] = n_tokens + tok
        return c
    lax.fori_loop(0, tb, body, 0, unroll=8)


def _invert(pads, pos, n_rows, n_exp, n_tokens, tm):
    nb, _, tb2 = pos.shape
    return pl.pallas_call(
        functools.partial(_invert_kernel, n_exp=n_exp, n_tokens=n_tokens, tm=tm),
        grid_spec=pltpu.PrefetchScalarGridSpec(
            num_scalar_prefetch=1,
            grid=(nb,),
            in_specs=[pl.BlockSpec((1, 1, tb2), lambda i, pads_: (i, 0, 0), memory_space=pltpu.SMEM)],
            out_specs=pl.BlockSpec(memory_space=pltpu.SMEM)),
        out_shape=jax.ShapeDtypeStruct((n_rows,), jnp.int32),
        compiler_params=_params(1),
        name="invert",
    )(pads, pos)


def _experts_io_kernel(te_ref, nu_ref, dprev_ref, dnext_ref, h_ref, wgu_ref, wd_ref, y_ref,
                       xbuf, ybuf, zero_ref, gsem, ssem, zsem, *, n_tokens, n_spare):
    _, tm, S, L = xbuf.shape
    i = pl.program_id(0)
    nu = nu_ref[0]
    slot = i % 2

    def src_row(dst):
        return jnp.where(dst >= 2 * n_tokens, 0, dst - jnp.where(dst >= n_tokens, n_tokens, 0))

    def for_rows(fn, unrolled):
        if unrolled:
            for t in range(tm):
                fn(t)
        else:
            def body(t, c):
                fn(t)
                return c
            lax.fori_loop(0, tm, body, 0, unroll=8)

    def gather_tile(d_ref, s, unrolled):
        for_rows(lambda t: _row_copy(h_ref, src_row(d_ref[0, 0, t]), xbuf.at[s], t, gsem.at[s]).start(),
                 unrolled)

    def scatter_tile(d_ref, s, unrolled):
        for_rows(lambda t: _row_copy(ybuf.at[s], t, y_ref, d_ref[0, 0, t], ssem.at[s]).start(), unrolled)

    def wait_gather(s):
        pltpu.make_async_copy(h_ref.at[pl.ds(0, tm)], xbuf.at[s], gsem.at[s]).wait()

    def wait_scatter(s):
        pltpu.make_async_copy(ybuf.at[s], y_ref.at[pl.ds(0, tm)], ssem.at[s]).wait()

    def spare_fill(j):
        return pltpu.make_async_copy(zero_ref, y_ref.at[pl.ds(2 * n_tokens + j * tm, tm)], zsem)

    @pl.when(i == 0)
    def _():
        zero_ref[...] = jnp.zeros_like(zero_ref)
        for j in range(n_spare // tm):
            spare_fill(j).start()
        for j in range(n_spare // tm):
            spare_fill(j).wait()
        gather_tile(dprev_ref, 0, False)

    @pl.when(i <= nu)
    def _():
        wait_gather(slot)

    @pl.when(jnp.logical_and(i >= 2, i <= nu))
    def _():
        wait_scatter(slot)

    def compute_tile(with_scatter):
        gather_tile(dnext_ref, 1 - slot, True)
        if with_scatter:
            scatter_tile(dprev_ref, 1 - slot, True)
        h = xbuf[slot].reshape(tm, S * L).astype(BF16)
        ybuf[slot] = _swiglu_packed(h, wgu_ref, wd_ref, (0, 0)).reshape(tm, S, L)

    @pl.when(i == 0)
    def _():
        compute_tile(False)

    @pl.when(jnp.logical_and(i >= 1, i < nu))
    def _():
        compute_tile(True)

    @pl.when(i == nu)
    def _():
        scatter_tile(dprev_ref, 1 - slot, False)
        wait_scatter(1 - slot)


def _experts_io(tile_expert, n_used, dst, h, wgu, wd, layer, tm, n_spare):
    n_tokens, S, L = h.shape
    n_tiles = dst.shape[0]
    _, E, Fe, D = wd.shape
    assert n_spare % tm == 0
    last = lambda nu: nu[0] - 1
    wmap = lambda i, te, nu: (layer, te[jnp.minimum(i, last(nu))], 0, 0)
    return pl.pallas_call(
        functools.partial(_experts_io_kernel, n_tokens=n_tokens, n_spare=n_spare),
        grid_spec=pltpu.PrefetchScalarGridSpec(
            num_scalar_prefetch=2,
            grid=(n_tiles,),
            in_specs=[pl.BlockSpec((1, 1, tm), lambda i, te, nu: (jnp.maximum(jnp.minimum(i, nu[0]) - 1, 0), 0, 0),
                                   memory_space=pltpu.SMEM),
                      pl.BlockSpec((1, 1, tm), lambda i, te, nu: (jnp.minimum(i + 1, last(nu)), 0, 0),
                                   memory_space=pltpu.SMEM),
                      pl.BlockSpec(memory_space=pl.ANY),
                      pl.BlockSpec((1, 1, D, 2 * Fe), wmap),
                      pl.BlockSpec((1, 1, Fe, D), wmap)],
            out_specs=pl.BlockSpec(memory_space=pl.ANY),
            scratch_shapes=[pltpu.VMEM((2, tm, S, L), F32), pltpu.VMEM((2, tm, S, L), F32),
                            pltpu.VMEM((tm, S, L), F32),
                            pltpu.SemaphoreType.DMA((2,)), pltpu.SemaphoreType.DMA((2,)),
                            pltpu.SemaphoreType.DMA(())]),
        out_shape=jax.ShapeDtypeStruct((2 * n_tokens + n_spare, S, L), F32),
        compiler_params=_params(1),
        name="experts",
    )(tile_expert, n_used, dst, dst, h, wgu, wd)


def _mix_kernel(x_ref, y1_ref, y2_ref, mod_ref, meta_ref, lng_ref, lnb_ref, o_ref, *, alpha):
    _, tc, D = x_ref.shape
    gates = _to_columns([meta_ref[0, META_G1:META_G1 + 1, :], meta_ref[0, META_G2:META_G2 + 1, :]], tc)
    f = gates[:, 0:1] * y1_ref[...].reshape(tc, D) + gates[:, 1:2] * y2_ref[...].reshape(tc, D)
    v = alpha * x_ref[0] + mod_ref[0, 5:6, :] * f
    o_ref[0] = _layer_norm(v, lng_ref[...], lnb_ref[...])


def _mix(y, x, mod, mod_row0, meta, lng, lnb, tc, alpha):
    B, T, D = x.shape
    nT = T // tc
    S, L = y.shape[1:]
    blk = pl.BlockSpec((1, tc, D), lambda b_, i: (b_, i, 0))
    return pl.pallas_call(
        functools.partial(_mix_kernel, alpha=alpha),
        grid=(B, nT),
        in_specs=[blk,
                  pl.BlockSpec((tc, S, L), lambda b_, i: (b_ * nT + i, 0, 0)),
                  pl.BlockSpec((tc, S, L), lambda b_, i: (B * nT + b_ * nT + i, 0, 0)),
                  pl.BlockSpec((1, 6, D), lambda b_, i: (b_ + mod_row0, 0, 0)),
                  pl.BlockSpec((1, SUBLANES, tc), lambda b_, i: (b_ * nT + i, 0, 0)),
                  _const_spec((1, D)), _const_spec((1, D))],
        out_specs=blk,
        out_shape=jax.ShapeDtypeStruct((B, T, D), F32),
        compiler_params=_params(2),
        name="mix",
    )(x, y, y, mod, meta, lng.reshape(1, D), lnb.reshape(1, D))


def _moe_sparse(x, mod, mod_row0, wr, br, wgu, wd, layer, lng, lnb, alpha):
    B, T, D = x.shape
    E = wd.shape[1]
    N = B * T
    tb, tm = ROUTE_ROWS, EXPERT_ROWS
    assert T % tb == 0 and E == SUBLANES
    h, meta, cnt = _route(x, mod, mod_row0, wr, br, tb)

    counts = cnt[:, 0].astype(jnp.int32)
    padded = (counts + tm - 1) // tm * tm
    ends = jnp.cumsum(padded)
    offs = ends - padded
    field = lambda k: meta[:, k, :].astype(jnp.int32)
    dest = lambda ids, ranks: jnp.sum(jnp.where(ids[..., None] == jnp.arange(E), offs, 0), -1) + ranks
    pos1 = dest(field(META_I1), field(META_R1))
    pos2 = dest(field(META_I2), field(META_R2))
    pos = jnp.concatenate([pos1, pos2], axis=1).reshape(N // tb, 1, 2 * tb)
    n_rows = 2 * N + E * tm
    n_tiles = n_rows // tm
    n_used = (ends[-1:] // tm).astype(jnp.int32)
    tile_expert = jnp.sum(jnp.arange(n_tiles)[:, None] * tm >= ends[None, :], axis=1)
    tile_expert = jnp.minimum(tile_expert, E - 1).astype(jnp.int32)
    pads = jnp.stack([jnp.concatenate([offs + counts, ends[-1:]]),
                      jnp.concatenate([ends, jnp.full((1,), n_rows)])]).astype(jnp.int32)

    dst = _invert(pads, pos, n_rows, E, N, tm).reshape(n_tiles, 1, tm)
    y = _experts_io(tile_expert, n_used, dst, h, wgu, wd, layer, tm, E * tm)
    return _mix(y, x, mod, mod_row0, meta, lng, lnb, tb, alpha)


def _kv_kernel(x_ref, mod_ref, wk_ref, wv_ref, kb_ref, vb_ref, kf_ref, vf_ref):
    bb, tt, D = x_ref.shape
    i = pl.program_id(1)
    shift = mod_ref[:, 0:1, :]
    scale = mod_ref[:, 1:2, :]
    h = (x_ref[...] * (1 + scale) + shift).astype(BF16).reshape(bb * tt, D)
    k = jnp.dot(h, wk_ref[...], preferred_element_type=F32).reshape(bb, tt, D)
    v = jnp.dot(h, wv_ref[...], preferred_element_type=F32).reshape(bb, tt, D)
    kb_ref[...] = k.astype(BF16)
    vb_ref[...] = v.astype(BF16)

    @pl.when(i == pl.num_programs(1) - 1)
    def _():
        kf_ref[...] = k
        vf_ref[...] = v


def _kv(x, mod, mod_row0, wk, wv, bb, tt):
    B, T, D = x.shape
    mb = mod_row0 // bb
    blk = pl.BlockSpec((bb, tt, D), lambda b_, i: (b_, i, 0))
    last = pl.BlockSpec((bb, tt, D), lambda b_, i: (b_, 0, 0))
    return pl.pallas_call(
        _kv_kernel,
        grid=(B // bb, T // tt),
        in_specs=[blk,
                  pl.BlockSpec((bb, 2, D), lambda b_, i: (b_ + mb, 0, 0)),
                  _const_spec((D, D)), _const_spec((D, D))],
        out_specs=[blk, blk, last, last],
        out_shape=[jax.ShapeDtypeStruct((B, T, D), BF16), jax.ShapeDtypeStruct((B, T, D), BF16),
                   jax.ShapeDtypeStruct((B, tt, D), F32), jax.ShapeDtypeStruct((B, tt, D), F32)],
        compiler_params=_params(2),
        name="kv",
    )(x, mod, wk, wv)


ATTN_ROWS = 256


def _attn_kernel(x_ref, mod_ref, kp_ref, kc_ref, vp_ref, vc_ref, rel_ref, wq_ref, wo_ref,
                 lng_ref, lnb_ref, o_ref, bias_ref, kcat_ref, vcat_ref, q_ref, a_ref,
                 *, alpha, chunk, past, pad, mask_first, q_scale):
    _, tt, D = x_ref.shape
    n_chunks = tt // chunk
    win = pad + past + chunk
    head_dim = LANES // 2
    i = pl.program_id(1)

    @pl.when(jnp.logical_and(pl.program_id(0) == 0, i == 0))
    def _():
        key = lax.broadcasted_iota(jnp.int32, (chunk, win), 1) - (pad + past)
        start = lax.broadcasted_iota(jnp.int32, (chunk, win), 0) // CHUNK * CHUNK
        visible = jnp.logical_and(key >= start - past, key < start + CHUNK)
        for hd in range(rel_ref.shape[0]):
            vec = jnp.broadcast_to(rel_ref[hd:hd + 1, :], (chunk, rel_ref.shape[1]))
            table = pltpu.roll(vec, 0, 1, stride=1, stride_axis=0)[:, :win]
            bias_ref[hd // 2, hd % 2 * chunk:(hd % 2 + 1) * chunk, :] = jnp.where(visible, table, NEG_INF)

    if pad:
        kcat_ref[0:pad, :] = jnp.zeros((pad, D), BF16)
        vcat_ref[0:pad, :] = jnp.zeros((pad, D), BF16)
    kcat_ref[pad:pad + past, :] = kp_ref[0]
    kcat_ref[pad + past:, :] = kc_ref[0]
    vcat_ref[pad:pad + past, :] = vp_ref[0]
    vcat_ref[pad + past:, :] = vc_ref[0]

    x = x_ref[0]
    h = (x * (1 + mod_ref[0, 1:2, :]) + mod_ref[0, 0:1, :]).astype(BF16)
    q = jnp.dot(h, wq_ref[...], preferred_element_type=F32) * q_scale
    q_ref[...] = q.astype(BF16)

    row_head = lax.broadcasted_iota(jnp.int32, (2 * chunk, LANES), 0) >= chunk
    lane_head = lax.broadcasted_iota(jnp.int32, (2 * chunk, LANES), 1) >= head_dim
    own = row_head == lane_head
    out_lane_head0 = lax.broadcasted_iota(jnp.int32, (chunk, LANES), 1) < head_dim
    key_col = lax.broadcasted_iota(jnp.int32, (1, win), 1)

    def chunk_loop(masked):
        def one_chunk(g, carry):
            r0 = pl.multiple_of(g * chunk, chunk)
            for pr in range(D // LANES):
                lanes = slice(pr * LANES, (pr + 1) * LANES)
                qp = q_ref[pl.ds(r0, chunk), lanes]
                qbd = jnp.where(own, jnp.concatenate([qp, qp], axis=0), jnp.zeros((), BF16))
                kw = kcat_ref[pl.ds(r0, win), lanes]
                s = lax.dot_general(qbd, kw, (((1,), (1,)), ((), ())), preferred_element_type=F32)
                s = s + bias_ref[pr]
                if masked:
                    s = jnp.where(key_col + r0 >= pad + past, s, NEG_INF)
                m = jnp.max(s, axis=-1, keepdims=True)
                ex = jnp.exp(s - m)
                l = jnp.sum(ex, axis=-1, keepdims=True)
                vw = vcat_ref[pl.ds(r0, win), lanes]
                r = jnp.dot(ex.astype(BF16), vw, preferred_element_type=F32) * (1.0 / l)
                a_ref[pl.ds(r0, chunk), lanes] = jnp.where(out_lane_head0, r[:chunk], r[chunk:]).astype(BF16)
            return carry
        lax.fori_loop(0, n_chunks, one_chunk, 0)

    if mask_first:
        @pl.when(i == 0)
        def _():
            chunk_loop(True)

        @pl.when(i > 0)
        def _():
            chunk_loop(False)
    else:
        chunk_loop(False)

    out = jnp.dot(a_ref[...], wo_ref[...], preferred_element_type=F32)
    v = alpha * x + mod_ref[0, 2:3, :] * out
    o_ref[0] = _layer_norm(v, lng_ref[...], lnb_ref[...])


def _rel_vectors(table, past, chunk, pad):
    win = pad + past + chunk
    width = -(-(win + chunk - 1) // LANES) * LANES
    m = jnp.arange(width)
    m = jnp.where(m < win, m, m - width)
    d = past + pad - m
    return table[:, jnp.clip(d, -REL_CLIP, REL_CLIP) + REL_CLIP]


def _attn(x, mod, mod_row0, k_prev, k_cur, v_prev, v_cur, table, wq, wo, lng, lnb,
          tt, chunk, mask_first, alpha):
    B, T, D = x.shape
    past = WINDOW
    assert past % tt == 0 or tt % past == 0
    assert tt == past or T == tt
    pad = -(past + chunk) % LANES
    rel = _rel_vectors(table, past, chunk, pad)
    npairs = D // LANES
    head_dim = D // N_HEADS
    blk = pl.BlockSpec((1, tt, D), lambda b_, i: (b_, i, 0))
    prev = pl.BlockSpec((1, past, D), lambda b_, i: (b_, jnp.maximum(i - 1, 0), 0))
    return pl.pallas_call(
        functools.partial(_attn_kernel, alpha=alpha, chunk=chunk, past=past, pad=pad,
                          mask_first=mask_first, q_scale=head_dim ** -0.5),
        grid=(B, T // tt),
        in_specs=[blk,
                  pl.BlockSpec((1, 6, D), lambda b_, i: (b_ + mod_row0, 0, 0)),
                  prev, blk, prev, blk,
                  _const_spec(rel.shape),
                  _const_spec((D, D)), _const_spec((D, D)),
                  _const_spec((1, D)), _const_spec((1, D))],
        out_specs=blk,
        out_shape=jax.ShapeDtypeStruct((B, T, D), F32),
        scratch_shapes=[pltpu.VMEM((npairs, 2 * chunk, pad + past + chunk), F32),
                        pltpu.VMEM((pad + past + tt, D), BF16), pltpu.VMEM((pad + past + tt, D), BF16),
                        pltpu.VMEM((tt, D), BF16), pltpu.VMEM((tt, D), BF16)],
        compiler_params=_params(2),
        name="attn",
    )(x, mod, k_prev, k_cur, v_prev, v_cur, rel, wq, wo, lng.reshape(1, D), lnb.reshape(1, D))


def _trunk(x, mods, mods_kv, mod_row0, conv_state, kv_cache, p, bb, tt, chunk):
    B, T, D = x.shape
    depth = p['w_ada'].shape[0]
    n_a = p['w_pw1'].shape[0]
    alpha = (2.0 * depth) ** 0.25
    assert T >= CONV_WIDTH - 1 and T % tt == 0 and B % bb == 0 and mod_row0 % bb == 0
    new_conv = []
    kb = vb = kf = vf = None
    for l in range(depth):
        mod = mods[l]
        if l < n_a:
            x, tail = _conv(x, conv_state[l], mod, mod_row0, p['w_pw1'][l], p['b_pw1'][l],
                            p['w_dw'][l], p['b_dw'][l], p['conv_ln_g'][l], p['conv_ln_b'][l],
                            p['w_pw2'][l], p['b_pw2'][l], p['ln_g'][l, 0], p['ln_b'][l, 0],
                            bb, tt, alpha)
            new_conv.append(tail[:, HALO - (CONV_WIDTH - 1):])
        else:
            j = l - n_a
            if kv_cache is None:
                x = _attn(x, mod, mod_row0, kb, kb, vb, vb, p['rel_bias'][j], p['w_q'][j], p['w_o'][j],
                          p['ln_g'][l, 0], p['ln_b'][l, 0], WINDOW, chunk, True, alpha)
            else:
                x = _attn(x, mod, mod_row0, kv_cache[0], kb, kv_cache[1], vb, p['rel_bias'][j],
                          p['w_q'][j], p['w_o'][j], p['ln_g'][l, 0], p['ln_b'][l, 0],
                          T, chunk, False, alpha)
        i = l // 2
        if l % 2 == 0:
            x = _ffn(x, mod, mod_row0, p['w_ff_gate'], p['w_ff_up'], p['w_ff_down'], i,
                     p['ln_g'][l, 1], p['ln_b'][l, 1], bb, tt, alpha)
        elif bb == 1 and T % ROUTE_ROWS == 0:
            x = _moe_sparse(x, mod, mod_row0, p['w_router'][i], p['b_router'][i], p['w_e_gu'],
                            p['w_e_down'], i, p['ln_g'][l, 1], p['ln_b'][l, 1], alpha)
        else:
            x = _moe(x, mod, mod_row0, p['w_router'][i], p['b_router'][i], p['w_e_gu'],
                     p['w_e_down'], i, p['ln_g'][l, 1], p['ln_b'][l, 1],
                     bb, tt, alpha)
        if l == n_a - 1:
            kb, vb, kf, vf = _kv(x, mods_kv, mod_row0, p['w_k'], p['w_v'], bb, tt)
    return x, jnp.stack(new_conv), kf, vf


def kernel(x_prompt, x_sample, c_prompt, c_sample, state_conv, cache_k, cache_v, w_ada, b_ada, ln_g, ln_b, w_pw1, b_pw1, w_dw, b_dw, conv_ln_g, conv_ln_b, w_pw2, b_pw2, w_ada_kv, b_ada_kv, w_k, w_v, w_q, w_o, rel_bias, w_ff_gate, w_ff_up, w_ff_down, w_router, b_router, w_e_gate, w_e_up, w_e_down):
    B, T, D = x_prompt.shape
    Bs, Ts, _ = x_sample.shape
    H = N_HEADS
    assert Bs + B <= MOD_ROWS and D == H * (LANES // 2)
    depth = w_ada.shape[0]
    n_a = w_pw1.shape[0]

    bf = lambda w: w.astype(BF16)
    p = dict(w_ada=w_ada, ln_g=ln_g, ln_b=ln_b, w_pw1=bf(w_pw1), b_pw1=b_pw1, w_dw=w_dw, b_dw=b_dw,
             conv_ln_g=conv_ln_g, conv_ln_b=conv_ln_b, w_pw2=bf(w_pw2), b_pw2=b_pw2,
             w_k=bf(w_k), w_v=bf(w_v), w_q=bf(w_q), w_o=bf(w_o), rel_bias=rel_bias,
             w_ff_gate=bf(w_ff_gate), w_ff_up=bf(w_ff_up), w_ff_down=bf(w_ff_down),
             w_router=w_router, b_router=b_router,
             w_e_gu=jnp.concatenate([bf(w_e_gate), bf(w_e_up)], axis=-1), w_e_down=bf(w_e_down))

    c_all = jnp.concatenate([c_sample, c_prompt, jnp.zeros((MOD_ROWS - Bs - B, D), F32)], axis=0)
    mods = _ada(c_all, w_ada, b_ada, 1536).reshape(depth, MOD_ROWS, 6, D)
    mods_kv = _ada(c_all, w_ada_kv[None], b_ada_kv[None], 1024).reshape(MOD_ROWS, 2, D)

    pad_state = lambda s: jnp.pad(s, ((0, 0), (0, 0), (HALO - (CONV_WIDTH - 1), 0), (0, 0)))
    zero_state = jnp.zeros((n_a, B, HALO, D), F32)
    y_p, conv_p, kf_p, vf_p = _trunk(x_prompt, mods, mods_kv, Bs, zero_state, None, p, 1, WINDOW,
                                     ATTN_ROWS)

    cache = (cache_k.reshape(Bs, -1, D).astype(BF16), cache_v.reshape(Bs, -1, D).astype(BF16))
    y_s, conv_s, kf_s, vf_s = _trunk(x_sample, mods, mods_kv, 0, pad_state(state_conv), cache, p,
                                     Bs, Ts, Ts)

    L = cache_k.shape[1]
    heads = lambda a: a.reshape(a.shape[0], a.shape[1], H, D // H)
    k_s = jnp.concatenate([cache_k, heads(kf_s)], axis=1)[:, -L:]
    v_s = jnp.concatenate([cache_v, heads(vf_s)], axis=1)[:, -L:]
    return (y_p, y_s, conv_p, heads(kf_p), heads(vf_p), conv_s, k_s, v_s)
```

```python
import functools

import jax
import jax.numpy as jnp
from jax import lax
from jax.experimental import pallas as pl
from jax.experimental.pallas import tpu as pltpu

CHUNK = 64
WINDOW = 512
REL_CLIP = 128
CONV_WIDTH = 31
N_HEADS = 16
LN_EPS = 1e-5
NEG_INF = -1e30

LANES = 128
HALO = 32
MOD_ROWS = 16
VMEM_LIMIT = 56 * 1024 * 1024

F32 = jnp.float32
BF16 = jnp.bfloat16


def _layer_norm(v, g, b):
    mu = jnp.mean(v, axis=-1, keepdims=True)
    d = v - mu
    var = jnp.mean(d * d, axis=-1, keepdims=True)
    return d * lax.rsqrt(var + LN_EPS) * g + b


def _sigmoid(v):
    return 0.5 * jnp.tanh(0.5 * v) + 0.5


def _silu(v):
    return v * _sigmoid(v)


def _const_spec(shape):
    nd = len(shape)
    return pl.BlockSpec(shape, lambda *_: (0,) * nd, pipeline_mode=pl.Buffered(1))


def _params(n_grid):
    return pltpu.CompilerParams(dimension_semantics=("arbitrary",) * n_grid,
                                vmem_limit_bytes=VMEM_LIMIT)


def _ada_kernel(c_ref, w_ref, b_ref, o_ref):
    a = _silu(c_ref[...]).astype(BF16)
    o_ref[0] = jnp.dot(a, w_ref[0].astype(BF16), preferred_element_type=F32) + b_ref[0]


def _ada(c, w, b, tn):
    L, D, N = w.shape
    return pl.pallas_call(
        _ada_kernel,
        grid=(L, N // tn),
        in_specs=[pl.BlockSpec((MOD_ROWS, D), lambda l, j: (0, 0)),
                  pl.BlockSpec((1, D, tn), lambda l, j: (l, 0, j)),
                  pl.BlockSpec((1, 1, tn), lambda l, j: (l, 0, j))],
        out_specs=pl.BlockSpec((1, MOD_ROWS, tn), lambda l, j: (l, 0, j)),
        out_shape=jax.ShapeDtypeStruct((L, MOD_ROWS, N), F32),
        compiler_params=_params(2),
        name="ada",
    )(c, w, b.reshape(L, 1, N))


CONV_ROWS = 64
CONV_LANES = 256
SUBLANES = 8


def _conv_kernel(x_ref, xn_ref, st_ref, mod_ref, w1_ref, b1_ref, wdw_ref, bdw_ref, cg_ref, cb_ref,
                 w2_ref, b2_ref, lng_ref, lnb_ref, o_ref, tail_ref, ext_ref, sh_ref, y_ref, u_ref,
                 *, alpha):
    bb, tt, D = x_ref.shape
    i = pl.program_id(1)

    def glu(xv):
        h = (xv * (1 + mod_ref[:, 1:2, :]) + mod_ref[:, 0:1, :]).astype(BF16).reshape(bb * tt, D)
        z = jnp.dot(h, w1_ref[...], preferred_element_type=F32) + b1_ref[...]
        return (z[:, :D] * _sigmoid(z[:, D:])).reshape(bb, tt, D)

    @pl.when(i == 0)
    def _():
        u_ref[...] = glu(x_ref[...])

    u = u_ref[...]

    @pl.when(i == 0)
    def _():
        ext_ref[:, 0:HALO, :] = st_ref[...]

    @pl.when(i > 0)
    def _():
        ext_ref[:, 0:HALO, :] = ext_ref[:, tt:tt + HALO, :]

    ext_ref[:, HALO:, :] = u

    @pl.when(i == pl.num_programs(1) - 1)
    def _():
        tail_ref[...] = u[:, tt - HALO:, :]

    u_next = glu(xn_ref[...])

    first = HALO - (CONV_WIDTH - 1)
    rows = min(CONV_ROWS, tt)
    for b in range(bb):
        for lb in range(D // CONV_LANES):
            lanes = slice(lb * CONV_LANES, (lb + 1) * CONV_LANES)
            sh_ref[0] = ext_ref[b, :, lanes]
            for s in range(1, SUBLANES):
                sh_ref[s, 0:tt + HALO - SUBLANES, :] = ext_ref[b, s:s + tt + HALO - SUBLANES, lanes]

            for r0 in range(0, tt, rows):
                acc = jnp.zeros((rows, CONV_LANES), F32)
                for j in range(CONV_WIDTH):
                    a, s = divmod(j + first, SUBLANES)
                    acc = acc + wdw_ref[j:j + 1, lanes] * sh_ref[s, r0 + SUBLANES * a:r0 + SUBLANES * a + rows, :]
                y_ref[b, r0:r0 + rows, lanes] = acc + bdw_ref[:, lanes]

    yn = _silu(_layer_norm(y_ref[...], cg_ref[...], cb_ref[...]))
    out = jnp.dot(yn.astype(BF16).reshape(bb * tt, D), w2_ref[...],
                  preferred_element_type=F32) + b2_ref[...]
    gate = mod_ref[:, 2:3, :]
    v = alpha * x_ref[...] + gate * out.reshape(bb, tt, D)
    o_ref[...] = _layer_norm(v, lng_ref[...], lnb_ref[...])
    u_ref[...] = u_next


def _conv(x, state, mod, mod_row0, w1, b1, w_dw, b_dw, cg, cb, w2, b2, lng, lnb, bb, tt, alpha):
    B, T, D = x.shape
    mb = mod_row0 // bb
    nT = T // tt
    wdw = jnp.pad(w_dw, ((0, HALO - CONV_WIDTH), (0, 0)))
    row = lambda a: a.reshape(1, -1)
    return pl.pallas_call(
        functools.partial(_conv_kernel, alpha=alpha),
        grid=(B // bb, nT),
        in_specs=[pl.BlockSpec((bb, tt, D), lambda b_, i: (b_, i, 0)),
                  pl.BlockSpec((bb, tt, D), lambda b_, i: (b_, jnp.minimum(i + 1, nT - 1), 0)),
                  pl.BlockSpec((bb, HALO, D), lambda b_, i: (b_, 0, 0)),
                  pl.BlockSpec((bb, 6, D), lambda b_, i: (b_ + mb, 0, 0)),
                  _const_spec((D, 2 * D)), _const_spec((1, 2 * D)),
                  _const_spec((HALO, D)),
                  _const_spec((1, D)), _const_spec((1, D)), _const_spec((1, D)),
                  _const_spec((D, D)),
                  _const_spec((1, D)), _const_spec((1, D)), _const_spec((1, D))],
        out_specs=[pl.BlockSpec((bb, tt, D), lambda b_, i: (b_, i, 0)),
                   pl.BlockSpec((bb, HALO, D), lambda b_, i: (b_, 0, 0))],
        out_shape=[jax.ShapeDtypeStruct((B, T, D), F32), jax.ShapeDtypeStruct((B, HALO, D), F32)],
        scratch_shapes=[pltpu.VMEM((bb, HALO + tt, D), F32),
                        pltpu.VMEM((SUBLANES, HALO + tt, CONV_LANES), F32),
                        pltpu.VMEM((bb, tt, D), F32), pltpu.VMEM((bb, tt, D), F32)],
        compiler_params=_params(2),
        name="conv",
    )(x, x, state, mod, w1, row(b1), wdw, row(b_dw), row(cg), row(cb), w2, row(b2), row(lng), row(lnb))


def _ff_chunks(F, width):
    return [(f0, min(width, F - f0)) for f0 in range(0, F, width)]


def _swiglu(h, wg_ref, wu_ref, wd_ref, lead):
    F = wg_ref.shape[-1]
    acc = None
    for f0, fw in _ff_chunks(F, 1024):
        g = jnp.dot(h, wg_ref[lead + (slice(None), slice(f0, f0 + fw))], preferred_element_type=F32)
        u = jnp.dot(h, wu_ref[lead + (slice(None), slice(f0, f0 + fw))], preferred_element_type=F32)
        a = (_silu(g) * u).astype(BF16)
        d = jnp.dot(a, wd_ref[lead + (slice(f0, f0 + fw), slice(None))], preferred_element_type=F32)
        acc = d if acc is None else acc + d
    return acc


def _swiglu_packed(h, wgu_ref, wd_ref, lead):
    F = wd_ref.shape[-2]
    gu = jnp.dot(h, wgu_ref[lead], preferred_element_type=F32)
    a = (_silu(gu[:, :F]) * gu[:, F:]).astype(BF16)
    return jnp.dot(a, wd_ref[lead], preferred_element_type=F32)


def _ffn_kernel(x_ref, mod_ref, wg_ref, wu_ref, wd_ref, lng_ref, lnb_ref, o_ref, *, alpha):
    bb, tt, D = x_ref.shape
    shift = mod_ref[:, 3:4, :]
    scale = mod_ref[:, 4:5, :]
    gate = mod_ref[:, 5:6, :]
    x = x_ref[...]
    h = (x * (1 + scale) + shift).astype(BF16).reshape(bb * tt, D)
    f = _swiglu(h, wg_ref, wu_ref, wd_ref, (0,))
    v = alpha * x + gate * f.reshape(bb, tt, D)
    o_ref[...] = _layer_norm(v, lng_ref[...], lnb_ref[...])


def _layer_spec(shape, layer):
    nd = len(shape)
    return pl.BlockSpec((1,) + shape, lambda *_: (layer,) + (0,) * nd, pipeline_mode=pl.Buffered(1))


def _ffn(x, mod, mod_row0, wg, wu, wd, layer, lng, lnb, bb, tt, alpha):
    B, T, D = x.shape
    F = wg.shape[-1]
    mb = mod_row0 // bb
    return pl.pallas_call(
        functools.partial(_ffn_kernel, alpha=alpha),
        grid=(B // bb, T // tt),
        in_specs=[pl.BlockSpec((bb, tt, D), lambda b_, i: (b_, i, 0)),
                  pl.BlockSpec((bb, 6, D), lambda b_, i: (b_ + mb, 0, 0)),
                  _layer_spec((D, F), layer), _layer_spec((D, F), layer), _layer_spec((F, D), layer),
                  _const_spec((1, D)), _const_spec((1, D))],
        out_specs=pl.BlockSpec((bb, tt, D), lambda b_, i: (b_, i, 0)),
        out_shape=jax.ShapeDtypeStruct((B, T, D), F32),
        compiler_params=_params(2),
        name="ffn",
    )(x, mod, wg, wu, wd, lng.reshape(1, D), lnb.reshape(1, D))


def _router_top2(h32, wr_ref, br_ref, n_exp):
    M, D = h32.shape
    rows = []
    for e in range(n_exp):
        prod = h32 * wr_ref[e:e + 1, :]
        part = prod[:, 0:LANES]
        for c in range(1, D // LANES):
            part = part + prod[:, c * LANES:(c + 1) * LANES]
        rows.append(jnp.sum(part.T, axis=0, keepdims=True))
    logits = jnp.concatenate(rows, axis=0) + br_ref[...]
    ex = jnp.exp(logits - jnp.max(logits, axis=0, keepdims=True))
    p = ex / jnp.sum(ex, axis=0, keepdims=True)
    eid = lax.broadcasted_iota(jnp.int32, p.shape, 0)

    def best(v):
        top = jnp.max(v, axis=0, keepdims=True)
        return top, jnp.min(jnp.where(v == top, eid, n_exp), axis=0, keepdims=True)

    p1, i1 = best(p)
    p2, i2 = best(jnp.where(eid == i1, -1.0, p))
    tot = p1 + p2
    return i1, i2, p1 / tot, p2 / tot


def _to_columns(rows, M):
    pad = jnp.zeros((LANES - len(rows), M), F32)
    return jnp.concatenate(list(rows) + [pad], axis=0).T


def _router_gates(h32, wr_ref, br_ref, n_exp):
    i1, i2, g1, g2 = _router_top2(h32, wr_ref, br_ref, n_exp)
    rows = [jnp.where(i1 == e, g1, 0.0) + jnp.where(i2 == e, g2, 0.0) for e in range(n_exp)]
    return _to_columns(rows, h32.shape[0])


def _moe_kernel(x_ref, mod_ref, wr_ref, br_ref, wgu_ref, wd_ref, lng_ref, lnb_ref, o_ref,
                h_ref, gate_ref, acc_ref, *, alpha, n_exp):
    bb, tt, D = x_ref.shape
    e = pl.program_id(2)

    @pl.when(e == 0)
    def _():
        shift = mod_ref[:, 3:4, :]
        scale = mod_ref[:, 4:5, :]
        h32 = (x_ref[...] * (1 + scale) + shift).reshape(bb * tt, D)
        h_ref[...] = h32.astype(BF16)
        gates = _router_gates(h32, wr_ref, br_ref, n_exp)
        for k in range(n_exp):
            gate_ref[k] = gates[:, k:k + 1]
        acc_ref[...] = jnp.zeros_like(acc_ref)

    f = _swiglu_packed(h_ref[...], wgu_ref, wd_ref, (0, 0))
    acc_ref[...] += gate_ref[e] * f

    @pl.when(e == n_exp - 1)
    def _():
        v = alpha * x_ref[...] + mod_ref[:, 5:6, :] * acc_ref[...].reshape(bb, tt, D)
        o_ref[...] = _layer_norm(v, lng_ref[...], lnb_ref[...])


def _moe(x, mod, mod_row0, wr, br, wgu, wd, layer, lng, lnb, bb, tt, alpha):
    B, T, D = x.shape
    _, E, Fe, _ = wd.shape
    mb = mod_row0 // bb
    M = bb * tt
    return pl.pallas_call(
        functools.partial(_moe_kernel, alpha=alpha, n_exp=E),
        grid=(B // bb, T // tt, E),
        in_specs=[pl.BlockSpec((bb, tt, D), lambda b_, i, e: (b_, i, 0)),
                  pl.BlockSpec((bb, 6, D), lambda b_, i, e: (b_ + mb, 0, 0)),
                  _const_spec((E, D)), _const_spec((E, 1)),
                  pl.BlockSpec((1, 1, D, 2 * Fe), lambda b_, i, e: (layer, e, 0, 0)),
                  pl.BlockSpec((1, 1, Fe, D), lambda b_, i, e: (layer, e, 0, 0)),
                  _const_spec((1, D)), _const_spec((1, D))],
        out_specs=pl.BlockSpec((bb, tt, D), lambda b_, i, e: (b_, i, 0)),
        out_shape=jax.ShapeDtypeStruct((B, T, D), F32),
        scratch_shapes=[pltpu.VMEM((M, D), BF16), pltpu.VMEM((E, M, 1), F32),
                        pltpu.VMEM((M, D), F32)],
        compiler_params=_params(3),
        name="moe",
    )(x, mod, wr.T, br.reshape(E, 1), wgu, wd, lng.reshape(1, D), lnb.reshape(1, D))


ROUTE_ROWS = 512
EXPERT_ROWS = 512
META_I1, META_I2, META_G1, META_G2, META_R1, META_R2 = range(6)


def _route_kernel(x_ref, mod_ref, wr_ref, br_ref, meta_ref, cnt_ref, *, n_exp):
    _, tt, D = x_ref.shape
    first = jnp.logical_and(pl.program_id(0) == 0, pl.program_id(1) == 0)

    @pl.when(first)
    def _():
        cnt_ref[...] = jnp.zeros_like(cnt_ref)

    h32 = x_ref[0] * (1 + mod_ref[0, 4:5, :]) + mod_ref[0, 3:4, :]
    i1, i2, g1, g2 = _router_top2(h32, wr_ref, br_ref, n_exp)

    eid = lax.broadcasted_iota(jnp.int32, (n_exp, tt), 0)
    o1 = (eid == i1).astype(F32)
    o2 = (eid == i2).astype(F32)
    upper = (lax.broadcasted_iota(jnp.int32, (tt, tt), 0) <
             lax.broadcasted_iota(jnp.int32, (tt, tt), 1)).astype(BF16)
    c1 = jnp.dot(o1.astype(BF16), upper, preferred_element_type=F32)
    c2 = jnp.dot(o2.astype(BF16), upper, preferred_element_type=F32)
    base = cnt_ref[:, 0:1]
    tot1 = jnp.sum(o1, axis=1, keepdims=True)
    tot2 = jnp.sum(o2, axis=1, keepdims=True)
    r1 = jnp.sum(o1 * (c1 + base), axis=0, keepdims=True)
    r2 = jnp.sum(o2 * (c2 + (base + tot1)), axis=0, keepdims=True)
    cnt_ref[...] = jnp.broadcast_to(base + tot1 + tot2, cnt_ref.shape)

    rows = [i1.astype(F32), i2.astype(F32), g1, g2, r1, r2]
    meta_ref[0] = jnp.concatenate(rows + [jnp.zeros((SUBLANES - len(rows), tt), F32)], axis=0)


def _route(x, mod, mod_row0, wr, br, tt):
    B, T, D = x.shape
    E = wr.shape[1]
    nT = T // tt
    return pl.pallas_call(
        functools.partial(_route_kernel, n_exp=E),
        grid=(B, nT),
        in_specs=[pl.BlockSpec((1, tt, D), lambda b_, i: (b_, i, 0)),
                  pl.BlockSpec((1, 6, D), lambda b_, i: (b_ + mod_row0, 0, 0)),
                  _const_spec((E, D)), _const_spec((E, 1))],
        out_specs=[pl.BlockSpec((1, SUBLANES, tt), lambda b_, i: (b_ * nT + i, 0, 0)),
                   pl.BlockSpec((E, LANES), lambda b_, i: (0, 0))],
        out_shape=[jax.ShapeDtypeStruct((B * nT, SUBLANES, tt), F32),
                   jax.ShapeDtypeStruct((E, LANES), F32)],
        compiler_params=_params(2),
        name="route",
    )(x, mod, wr.T, br.reshape(E, 1))


def _row_copy(src_ref, src_row, dst_ref, dst_row, sem):
    return pltpu.make_async_copy(src_ref.at[pl.ds(src_row, 1)], dst_ref.at[pl.ds(dst_row, 1)], sem)


def _dispatch_kernel(pad_ref, pos_ref, x_ref, mod_ref, xs_ref, rows_ref, zero_ref, sems, *, n_exp):
    _, tb, D = x_ref.shape
    tm = zero_ref.shape[0]
    i = pl.program_id(0) * pl.num_programs(1) + pl.program_id(1)
    last = pl.num_programs(0) * pl.num_programs(1) - 1
    slot = i % 2
    pad_sem = sems.at[2]

    def tail_copy(t):
        return pltpu.make_async_copy(zero_ref, xs_ref.at[pl.ds(pl.multiple_of(t * tm, tm), tm)], pad_sem)

    @pl.when(i == 0)
    def _():
        zero_ref[...] = jnp.zeros_like(zero_ref)
        for e in range(n_exp):
            def fill(r, c):
                _row_copy(zero_ref, 0, xs_ref, r, pad_sem).start()
                return c
            lax.fori_loop(pad_ref[0, e], pad_ref[1, e], fill, 0)

        def fill_tail(t, c):
            tail_copy(t).start()
            return c
        lax.fori_loop(pad_ref[0, n_exp], pad_ref[1, n_exp], fill_tail, 0)

    h32 = x_ref[0] * (1 + mod_ref[0, 4:5, :]) + mod_ref[0, 3:4, :]
    rows_ref[slot] = h32.reshape(tb, D // LANES, LANES)

    def issue(t, c):
        _row_copy(rows_ref.at[slot], t, xs_ref, pos_ref[0, 0, t], sems.at[slot]).start()
        _row_copy(rows_ref.at[slot], t, xs_ref, pos_ref[0, 0, tb + t], sems.at[slot]).start()
        return c
    lax.fori_loop(0, tb, issue, 0, unroll=8)

    def wait_step(s):
        for _ in range(2):
            pltpu.make_async_copy(rows_ref.at[s], xs_ref.at[pl.ds(0, tb)], sems.at[s]).wait()

    @pl.when(i > 0)
    def _():
        wait_step(1 - slot)

    @pl.when(i == last)
    def _():
        wait_step(slot)
        for e in range(n_exp):
            def drain(r, c):
                _row_copy(zero_ref, 0, xs_ref, r, pad_sem).wait()
                return c
            lax.fori_loop(pad_ref[0, e], pad_ref[1, e], drain, 0)

        def drain_tail(t, c):
            tail_copy(t).wait()
            return c
        lax.fori_loop(pad_ref[0, n_exp], pad_ref[1, n_exp], drain_tail, 0)


def _dispatch(pads, pos, x, mod, mod_row0, n_rows, n_exp, tm):
    B, T, D = x.shape
    tb = pos.shape[2] // 2
    nT = T // tb
    S, L = D // LANES, LANES
    return pl.pallas_call(
        functools.partial(_dispatch_kernel, n_exp=n_exp),
        grid_spec=pltpu.PrefetchScalarGridSpec(
            num_scalar_prefetch=1,
            grid=(B, nT),
            in_specs=[pl.BlockSpec((1, 1, 2 * tb), lambda b_, i, pads_: (b_ * nT + i, 0, 0),
                                   memory_space=pltpu.SMEM),
                      pl.BlockSpec((1, tb, D), lambda b_, i, pads_: (b_, i, 0)),
                      pl.BlockSpec((1, 6, D), lambda b_, i, pads_: (b_ + mod_row0, 0, 0))],
            out_specs=pl.BlockSpec(memory_space=pl.ANY),
            scratch_shapes=[pltpu.VMEM((2, tb, S, L), F32), pltpu.VMEM((tm, S, L), F32),
                            pltpu.SemaphoreType.DMA((3,))]),
        out_shape=jax.ShapeDtypeStruct((n_rows, S, L), F32),
        compiler_params=_params(2),
        name="dispatch",
    )(pads, pos, x, mod)


def _experts_kernel(te_ref, nu_ref, xs_ref, wgu_ref, wd_ref, y_ref):
    tm, S, L = xs_ref.shape
    i = pl.program_id(0)

    @pl.when(i < nu_ref[0])
    def _():
        h = xs_ref[...].reshape(tm, S * L).astype(BF16)
        y_ref[...] = _swiglu_packed(h, wgu_ref, wd_ref, (0, 0)).reshape(tm, S, L)

    @pl.when(i >= nu_ref[0])
    def _():
        y_ref[...] = jnp.zeros_like(y_ref)


def _experts(tile_expert, n_used, xs, wgu, wd, layer, tm):
    n_rows, S, L = xs.shape
    _, E, Fe, D = wd.shape
    used = lambda i, te, nu: jnp.minimum(i, nu[0] - 1)
    return pl.pallas_call(
        _experts_kernel,
        grid_spec=pltpu.PrefetchScalarGridSpec(
            num_scalar_prefetch=2,
            grid=(n_rows // tm,),
            in_specs=[pl.BlockSpec((tm, S, L), lambda i, te, nu: (used(i, te, nu), 0, 0)),
                      pl.BlockSpec((1, 1, D, 2 * Fe), lambda i, te, nu: (layer, te[used(i, te, nu)], 0, 0)),
                      pl.BlockSpec((1, 1, Fe, D), lambda i, te, nu: (layer, te[used(i, te, nu)], 0, 0))],
            out_specs=pl.BlockSpec((tm, S, L), lambda i, te, nu: (i, 0, 0))),
        out_shape=jax.ShapeDtypeStruct((n_rows, S, L), F32),
        compiler_params=_params(1),
        name="experts",
    )(tile_expert, n_used, xs, wgu, wd)


def _combine_kernel(pcur_ref, pnext_ref, y_ref, x_ref, mod_ref, meta_ref, lng_ref, lnb_ref, o_ref,
                    buf_ref, sems, *, alpha):
    _, tc, D = x_ref.shape
    step = pl.program_id(0) * pl.num_programs(1) + pl.program_id(1)
    n_steps = pl.num_programs(0) * pl.num_programs(1)
    slot = step % 2

    def issue(p_ref, s):
        def body(t, c):
            _row_copy(y_ref, p_ref[0, 0, t], buf_ref.at[s], t, sems.at[s]).start()
            _row_copy(y_ref, p_ref[0, 0, tc + t], buf_ref.at[s], tc + t, sems.at[s]).start()
            return c
        lax.fori_loop(0, tc, body, 0, unroll=8)

    @pl.when(step == 0)
    def _():
        issue(pcur_ref, 0)

    @pl.when(step + 1 < n_steps)
    def _():
        issue(pnext_ref, 1 - slot)

    pltpu.make_async_copy(y_ref.at[pl.ds(0, 2 * tc)], buf_ref.at[slot], sems.at[slot]).wait()
    y1 = buf_ref[slot, 0:tc].reshape(tc, D)
    y2 = buf_ref[slot, tc:2 * tc].reshape(tc, D)
    gates = _to_columns([meta_ref[0, META_G1:META_G1 + 1, :], meta_ref[0, META_G2:META_G2 + 1, :]], tc)
    f = gates[:, 0:1] * y1 + gates[:, 1:2] * y2
    v = alpha * x_ref[0] + mod_ref[0, 5:6, :] * f
    o_ref[0] = _layer_norm(v, lng_ref[...], lnb_ref[...])


def _combine(pos, y, x, mod, mod_row0, meta, lng, lnb, alpha):
    B, T, D = x.shape
    nb, _, tc2 = pos.shape
    tc = tc2 // 2
    nT = T // tc
    S, L = y.shape[1:]
    blk = pl.BlockSpec((1, tc, D), lambda b_, i: (b_, i, 0))
    return pl.pallas_call(
        functools.partial(_combine_kernel, alpha=alpha),
        grid=(B, nT),
        in_specs=[pl.BlockSpec((1, 1, tc2), lambda b_, i: (b_ * nT + i, 0, 0), memory_space=pltpu.SMEM),
                  pl.BlockSpec((1, 1, tc2), lambda b_, i: (jnp.minimum(b_ * nT + i + 1, nb - 1), 0, 0),
                               memory_space=pltpu.SMEM),
                  pl.BlockSpec(memory_space=pl.ANY),
                  blk,
                  pl.BlockSpec((1, 6, D), lambda b_, i: (b_ + mod_row0, 0, 0)),
                  pl.BlockSpec((1, SUBLANES, tc), lambda b_, i: (b_ * nT + i, 0, 0)),
                  _const_spec((1, D)), _const_spec((1, D))],
        out_specs=blk,
        out_shape=jax.ShapeDtypeStruct((B, T, D), F32),
        scratch_shapes=[pltpu.VMEM((2, tc2, S, L), F32), pltpu.SemaphoreType.DMA((2,))],
        compiler_params=_params(2),
        name="combine",
    )(pos, pos, y, x, mod, meta, lng.reshape(1, D), lnb.reshape(1, D))


def _moe_sparse(x, mod, mod_row0, wr, br, wgu, wd, layer, lng, lnb, alpha):
    B, T, D = x.shape
    E = wd.shape[1]
    N = B * T
    tb, tm = ROUTE_ROWS, EXPERT_ROWS
    assert T % tb == 0 and E == SUBLANES
    meta, cnt = _route(x, mod, mod_row0, wr, br, tb)

    counts = cnt[:, 0].astype(jnp.int32)
    padded = (counts + tm - 1) // tm * tm
    ends = jnp.cumsum(padded)
    offs = ends - padded
    field = lambda k: meta[:, k, :].astype(jnp.int32)
    dest = lambda ids, ranks: jnp.sum(jnp.where(ids[..., None] == jnp.arange(E), offs, 0), -1) + ranks
    pos1 = dest(field(META_I1), field(META_R1))
    pos2 = dest(field(META_I2), field(META_R2))
    pos = jnp.concatenate([pos1, pos2], axis=1).reshape(N // tb, 1, 2 * tb)
    n_rows = 2 * N + E * tm
    n_tiles = n_rows // tm
    n_used = (ends[-1:] // tm).astype(jnp.int32)
    tile_expert = jnp.sum(jnp.arange(n_tiles)[:, None] * tm >= ends[None, :], axis=1)
    tile_expert = jnp.minimum(tile_expert, E - 1).astype(jnp.int32)
    pads = jnp.stack([jnp.concatenate([offs + counts, n_used]),
                      jnp.concatenate([ends, jnp.full((1,), n_tiles)])]).astype(jnp.int32)

    xs = _dispatch(pads, pos, x, mod, mod_row0, n_rows, E, tm)
    y = _experts(tile_expert, n_used, xs, wgu, wd, layer, tm)
    return _combine(pos, y, x, mod, mod_row0, meta, lng, lnb, alpha)


def _kv_kernel(x_ref, mod_ref, wk_ref, wv_ref, kb_ref, vb_ref, kf_ref, vf_ref):
    bb, tt, D = x_ref.shape
    i = pl.program_id(1)
    shift = mod_ref[:, 0:1, :]
    scale = mod_ref[:, 1:2, :]
    h = (x_ref[...] * (1 + scale) + shift).astype(BF16).reshape(bb * tt, D)
    k = jnp.dot(h, wk_ref[...], preferred_element_type=F32).reshape(bb, tt, D)
    v = jnp.dot(h, wv_ref[...], preferred_element_type=F32).reshape(bb, tt, D)
    kb_ref[...] = k.astype(BF16)
    vb_ref[...] = v.astype(BF16)

    @pl.when(i == pl.num_programs(1) - 1)
    def _():
        kf_ref[...] = k
        vf_ref[...] = v


def _kv(x, mod, mod_row0, wk, wv, bb, tt):
    B, T, D = x.shape
    mb = mod_row0 // bb
    blk = pl.BlockSpec((bb, tt, D), lambda b_, i: (b_, i, 0))
    last = pl.BlockSpec((bb, tt, D), lambda b_, i: (b_, 0, 0))
    return pl.pallas_call(
        _kv_kernel,
        grid=(B // bb, T // tt),
        in_specs=[blk,
                  pl.BlockSpec((bb, 2, D), lambda b_, i: (b_ + mb, 0, 0)),
                  _const_spec((D, D)), _const_spec((D, D))],
        out_specs=[blk, blk, last, last],
        out_shape=[jax.ShapeDtypeStruct((B, T, D), BF16), jax.ShapeDtypeStruct((B, T, D), BF16),
                   jax.ShapeDtypeStruct((B, tt, D), F32), jax.ShapeDtypeStruct((B, tt, D), F32)],
        compiler_params=_params(2),
        name="kv",
    )(x, mod, wk, wv)


ATTN_ROWS = 256


def _attn_kernel(x_ref, mod_ref, kp_ref, kc_ref, vp_ref, vc_ref, rel_ref, wq_ref, wo_ref,
                 lng_ref, lnb_ref, o_ref, bias_ref, kcat_ref, vcat_ref, q_ref, a_ref,
                 *, alpha, chunk, past, pad, mask_first, q_scale):
    _, tt, D = x_ref.shape
    n_chunks = tt // chunk
    win = pad + past + chunk
    head_dim = LANES // 2
    i = pl.program_id(1)

    @pl.when(jnp.logical_and(pl.program_id(0) == 0, i == 0))
    def _():
        key = lax.broadcasted_iota(jnp.int32, (chunk, win), 1) - (pad + past)
        start = lax.broadcasted_iota(jnp.int32, (chunk, win), 0) // CHUNK * CHUNK
        visible = jnp.logical_and(key >= start - past, key < start + CHUNK)
        for hd in range(rel_ref.shape[0]):
            vec = jnp.broadcast_to(rel_ref[hd:hd + 1, :], (chunk, rel_ref.shape[1]))
            table = pltpu.roll(vec, 0, 1, stride=1, stride_axis=0)[:, :win]
            bias_ref[hd // 2, hd % 2 * chunk:(hd % 2 + 1) * chunk, :] = jnp.where(visible, table, NEG_INF)

    if pad:
        kcat_ref[0:pad, :] = jnp.zeros((pad, D), BF16)
        vcat_ref[0:pad, :] = jnp.zeros((pad, D), BF16)
    kcat_ref[pad:pad + past, :] = kp_ref[0]
    kcat_ref[pad + past:, :] = kc_ref[0]
    vcat_ref[pad:pad + past, :] = vp_ref[0]
    vcat_ref[pad + past:, :] = vc_ref[0]

    x = x_ref[0]
    h = (x * (1 + mod_ref[0, 1:2, :]) + mod_ref[0, 0:1, :]).astype(BF16)
    q = jnp.dot(h, wq_ref[...], preferred_element_type=F32) * q_scale
    q_ref[...] = q.astype(BF16)

    row_head = lax.broadcasted_iota(jnp.int32, (2 * chunk, LANES), 0) >= chunk
    lane_head = lax.broadcasted_iota(jnp.int32, (2 * chunk, LANES), 1) >= head_dim
    own = row_head == lane_head
    out_lane_head0 = lax.broadcasted_iota(jnp.int32, (chunk, LANES), 1) < head_dim
    key_col = lax.broadcasted_iota(jnp.int32, (1, win), 1)

    def chunk_loop(masked):
        def one_chunk(g, carry):
            r0 = pl.multiple_of(g * chunk, chunk)
            for pr in range(D // LANES):
                lanes = slice(pr * LANES, (pr + 1) * LANES)
                qp = q_ref[pl.ds(r0, chunk), lanes]
                qbd = jnp.where(own, jnp.concatenate([qp, qp], axis=0), jnp.zeros((), BF16))
                kw = kcat_ref[pl.ds(r0, win), lanes]
                s = lax.dot_general(qbd, kw, (((1,), (1,)), ((), ())), preferred_element_type=F32)
                s = s + bias_ref[pr]
                if masked:
                    s = jnp.where(key_col + r0 >= pad + past, s, NEG_INF)
                m = jnp.max(s, axis=-1, keepdims=True)
                ex = jnp.exp(s - m)
                l = jnp.sum(ex, axis=-1, keepdims=True)
                vw = vcat_ref[pl.ds(r0, win), lanes]
                r = jnp.dot(ex.astype(BF16), vw, preferred_element_type=F32) * (1.0 / l)
                a_ref[pl.ds(r0, chunk), lanes] = jnp.where(out_lane_head0, r[:chunk], r[chunk:]).astype(BF16)
            return carry
        lax.fori_loop(0, n_chunks, one_chunk, 0)

    if mask_first:
        @pl.when(i == 0)
        def _():
            chunk_loop(True)

        @pl.when(i > 0)
        def _():
            chunk_loop(False)
    else:
        chunk_loop(False)

    out = jnp.dot(a_ref[...], wo_ref[...], preferred_element_type=F32)
    v = alpha * x + mod_ref[0, 2:3, :] * out
    o_ref[0] = _layer_norm(v, lng_ref[...], lnb_ref[...])


def _rel_vectors(table, past, chunk, pad):
    win = pad + past + chunk
    width = -(-(win + chunk - 1) // LANES) * LANES
    m = jnp.arange(width)
    m = jnp.where(m < win, m, m - width)
    d = past + pad - m
    return table[:, jnp.clip(d, -REL_CLIP, REL_CLIP) + REL_CLIP]


def _attn(x, mod, mod_row0, k_prev, k_cur, v_prev, v_cur, table, wq, wo, lng, lnb,
          tt, chunk, mask_first, alpha):
    B, T, D = x.shape
    past = WINDOW
    assert past % tt == 0 or tt % past == 0
    assert tt == past or T == tt
    pad = -(past + chunk) % LANES
    rel = _rel_vectors(table, past, chunk, pad)
    npairs = D // LANES
    head_dim = D // N_HEADS
    blk = pl.BlockSpec((1, tt, D), lambda b_, i: (b_, i, 0))
    prev = pl.BlockSpec((1, past, D), lambda b_, i: (b_, jnp.maximum(i - 1, 0), 0))
    return pl.pallas_call(
        functools.partial(_attn_kernel, alpha=alpha, chunk=chunk, past=past, pad=pad,
                          mask_first=mask_first, q_scale=head_dim ** -0.5),
        grid=(B, T // tt),
        in_specs=[blk,
                  pl.BlockSpec((1, 6, D), lambda b_, i: (b_ + mod_row0, 0, 0)),
                  prev, blk, prev, blk,
                  _const_spec(rel.shape),
                  _const_spec((D, D)), _const_spec((D, D)),
                  _const_spec((1, D)), _const_spec((1, D))],
        out_specs=blk,
        out_shape=jax.ShapeDtypeStruct((B, T, D), F32),
        scratch_shapes=[pltpu.VMEM((npairs, 2 * chunk, pad + past + chunk), F32),
                        pltpu.VMEM((pad + past + tt, D), BF16), pltpu.VMEM((pad + past + tt, D), BF16),
                        pltpu.VMEM((tt, D), BF16), pltpu.VMEM((tt, D), BF16)],
        compiler_params=_params(2),
        name="attn",
    )(x, mod, k_prev, k_cur, v_prev, v_cur, rel, wq, wo, lng.reshape(1, D), lnb.reshape(1, D))


def _trunk(x, mods, mods_kv, mod_row0, conv_state, kv_cache, p, bb, tt, chunk):
    B, T, D = x.shape
    depth = p['w_ada'].shape[0]
    n_a = p['w_pw1'].shape[0]
    alpha = (2.0 * depth) ** 0.25
    assert T >= CONV_WIDTH - 1 and T % tt == 0 and B % bb == 0 and mod_row0 % bb == 0
    new_conv = []
    kb = vb = kf = vf = None
    for l in range(depth):
        mod = mods[l]
        if l < n_a:
            x, tail = _conv(x, conv_state[l], mod, mod_row0, p['w_pw1'][l], p['b_pw1'][l],
                            p['w_dw'][l], p['b_dw'][l], p['conv_ln_g'][l], p['conv_ln_b'][l],
                            p['w_pw2'][l], p['b_pw2'][l], p['ln_g'][l, 0], p['ln_b'][l, 0],
                            bb, tt, alpha)
            new_conv.append(tail[:, HALO - (CONV_WIDTH - 1):])
        else:
            j = l - n_a
            if kv_cache is None:
                x = _attn(x, mod, mod_row0, kb, kb, vb, vb, p['rel_bias'][j], p['w_q'][j], p['w_o'][j],
                          p['ln_g'][l, 0], p['ln_b'][l, 0], WINDOW, chunk, True, alpha)
            else:
                x = _attn(x, mod, mod_row0, kv_cache[0], kb, kv_cache[1], vb, p['rel_bias'][j],
                          p['w_q'][j], p['w_o'][j], p['ln_g'][l, 0], p['ln_b'][l, 0],
                          T, chunk, False, alpha)
        i = l // 2
        if l % 2 == 0:
            x = _ffn(x, mod, mod_row0, p['w_ff_gate'], p['w_ff_up'], p['w_ff_down'], i,
                     p['ln_g'][l, 1], p['ln_b'][l, 1], bb, tt, alpha)
        elif bb == 1 and T % ROUTE_ROWS == 0:
            x = _moe_sparse(x, mod, mod_row0, p['w_router'][i], p['b_router'][i], p['w_e_gu'],
                            p['w_e_down'], i, p['ln_g'][l, 1], p['ln_b'][l, 1], alpha)
        else:
            x = _moe(x, mod, mod_row0, p['w_router'][i], p['b_router'][i], p['w_e_gu'],
                     p['w_e_down'], i, p['ln_g'][l, 1], p['ln_b'][l, 1],
                     bb, tt, alpha)
        if l == n_a - 1:
            kb, vb, kf, vf = _kv(x, mods_kv, mod_row0, p['w_k'], p['w_v'], bb, tt)
    return x, jnp.stack(new_conv), kf, vf


def kernel(x_prompt, x_sample, c_prompt, c_sample, state_conv, cache_k, cache_v, w_ada, b_ada, ln_g, ln_b, w_pw1, b_pw1, w_dw, b_dw, conv_ln_g, conv_ln_b, w_pw2, b_pw2, w_ada_kv, b_ada_kv, w_k, w_v, w_q, w_o, rel_bias, w_ff_gate, w_ff_up, w_ff_down, w_router, b_router, w_e_gate, w_e_up, w_e_down):
    B, T, D = x_prompt.shape
    Bs, Ts, _ = x_sample.shape
    H = N_HEADS
    assert Bs + B <= MOD_ROWS and D == H * (LANES // 2)
    depth = w_ada.shape[0]
    n_a = w_pw1.shape[0]

    bf = lambda w: w.astype(BF16)
    p = dict(w_ada=w_ada, ln_g=ln_g, ln_b=ln_b, w_pw1=bf(w_pw1), b_pw1=b_pw1, w_dw=w_dw, b_dw=b_dw,
             conv_ln_g=conv_ln_g, conv_ln_b=conv_ln_b, w_pw2=bf(w_pw2), b_pw2=b_pw2,
             w_k=bf(w_k), w_v=bf(w_v), w_q=bf(w_q), w_o=bf(w_o), rel_bias=rel_bias,
             w_ff_gate=bf(w_ff_gate), w_ff_up=bf(w_ff_up), w_ff_down=bf(w_ff_down),
             w_router=w_router, b_router=b_router,
             w_e_gu=jnp.concatenate([bf(w_e_gate), bf(w_e_up)], axis=-1), w_e_down=bf(w_e_down))

    c_all = jnp.concatenate([c_sample, c_prompt, jnp.zeros((MOD_ROWS - Bs - B, D), F32)], axis=0)
    mods = _ada(c_all, w_ada, b_ada, 1536).reshape(depth, MOD_ROWS, 6, D)
    mods_kv = _ada(c_all, w_ada_kv[None], b_ada_kv[None], 1024).reshape(MOD_ROWS, 2, D)

    pad_state = lambda s: jnp.pad(s, ((0, 0), (0, 0), (HALO - (CONV_WIDTH - 1), 0), (0, 0)))
    zero_state = jnp.zeros((n_a, B, HALO, D), F32)
    y_p, conv_p, kf_p, vf_p = _trunk(x_prompt, mods, mods_kv, Bs, zero_state, None, p, 1, WINDOW,
                                     ATTN_ROWS)

    cache = (cache_k.reshape(Bs, -1, D).astype(BF16), cache_v.reshape(Bs, -1, D).astype(BF16))
    y_s, conv_s, kf_s, vf_s = _trunk(x_sample, mods, mods_kv, 0, pad_state(state_conv), cache, p,
                                     Bs, Ts, Ts)

    L = cache_k.shape[1]
    heads = lambda a: a.reshape(a.shape[0], a.shape[1], H, D // H)
    k_s = jnp.concatenate([cache_k, heads(kf_s)], axis=1)[:, -L:]
    v_s = jnp.concatenate([cache_v, heads(vf_s)], axis=1)[:, -L:]
    return (y_p, y_s, conv_p, heads(kf_p), heads(vf_p), conv_s, k_s, v_s)
```

```python
import functools

import jax
import jax.numpy as jnp
from jax import lax
from jax.experimental import pallas as pl
from jax.experimental.pallas import tpu as pltpu

CHUNK = 64
WINDOW = 512
REL_CLIP = 128
CONV_WIDTH = 31
N_HEADS = 16
LN_EPS = 1e-5
NEG_INF = -1e30

LANES = 128
HALO = 32
MOD_ROWS = 16
VMEM_LIMIT = 56 * 1024 * 1024

F32 = jnp.float32
BF16 = jnp.bfloat16


def _layer_norm(v, g, b):
    mu = jnp.mean(v, axis=-1, keepdims=True)
    d = v - mu
    var = jnp.mean(d * d, axis=-1, keepdims=True)
    return d * lax.rsqrt(var + LN_EPS) * g + b


def _sigmoid(v):
    return 0.5 * jnp.tanh(0.5 * v) + 0.5


def _silu(v):
    return v * _sigmoid(v)


def _const_spec(shape):
    nd = len(shape)
    return pl.BlockSpec(shape, lambda *_: (0,) * nd, pipeline_mode=pl.Buffered(1))


def _params(n_grid):
    return pltpu.CompilerParams(dimension_semantics=("arbitrary",) * n_grid,
                                vmem_limit_bytes=VMEM_LIMIT)


def _ada_kernel(c_ref, w_ref, b_ref, o_ref):
    a = _silu(c_ref[...]).astype(BF16)
    o_ref[0] = jnp.dot(a, w_ref[0].astype(BF16), preferred_element_type=F32) + b_ref[0]


def _ada(c, w, b, tn):
    L, D, N = w.shape
    return pl.pallas_call(
        _ada_kernel,
        grid=(L, N // tn),
        in_specs=[pl.BlockSpec((MOD_ROWS, D), lambda l, j: (0, 0)),
                  pl.BlockSpec((1, D, tn), lambda l, j: (l, 0, j)),
                  pl.BlockSpec((1, 1, tn), lambda l, j: (l, 0, j))],
        out_specs=pl.BlockSpec((1, MOD_ROWS, tn), lambda l, j: (l, 0, j)),
        out_shape=jax.ShapeDtypeStruct((L, MOD_ROWS, N), F32),
        compiler_params=_params(2),
        name="ada",
    )(c, w, b.reshape(L, 1, N))


CONV_ROWS = 64
CONV_LANES = 256
SUBLANES = 8


def _conv_kernel(x_ref, xn_ref, st_ref, mod_ref, w1_ref, b1_ref, wdw_ref, bdw_ref, cg_ref, cb_ref,
                 w2_ref, b2_ref, lng_ref, lnb_ref, o_ref, tail_ref, ext_ref, sh_ref, y_ref, u_ref,
                 *, alpha):
    bb, tt, D = x_ref.shape
    i = pl.program_id(1)

    def glu(xv):
        h = (xv * (1 + mod_ref[:, 1:2, :]) + mod_ref[:, 0:1, :]).astype(BF16).reshape(bb * tt, D)
        z = jnp.dot(h, w1_ref[...], preferred_element_type=F32) + b1_ref[...]
        return (z[:, :D] * _sigmoid(z[:, D:])).reshape(bb, tt, D)

    @pl.when(i == 0)
    def _():
        u_ref[...] = glu(x_ref[...])

    u = u_ref[...]

    @pl.when(i == 0)
    def _():
        ext_ref[:, 0:HALO, :] = st_ref[...]

    @pl.when(i > 0)
    def _():
        ext_ref[:, 0:HALO, :] = ext_ref[:, tt:tt + HALO, :]

    ext_ref[:, HALO:, :] = u

    @pl.when(i == pl.num_programs(1) - 1)
    def _():
        tail_ref[...] = u[:, tt - HALO:, :]

    u_next = glu(xn_ref[...])

    first = HALO - (CONV_WIDTH - 1)
    rows = min(CONV_ROWS, tt)
    for b in range(bb):
        for lb in range(D // CONV_LANES):
            lanes = slice(lb * CONV_LANES, (lb + 1) * CONV_LANES)
            sh_ref[0] = ext_ref[b, :, lanes]
            for s in range(1, SUBLANES):
                sh_ref[s, 0:tt + HALO - SUBLANES, :] = ext_ref[b, s:s + tt + HALO - SUBLANES, lanes]

            for r0 in range(0, tt, rows):
                acc = jnp.zeros((rows, CONV_LANES), F32)
                for j in range(CONV_WIDTH):
                    a, s = divmod(j + first, SUBLANES)
                    acc = acc + wdw_ref[j:j + 1, lanes] * sh_ref[s, r0 + SUBLANES * a:r0 + SUBLANES * a + rows, :]
                y_ref[b, r0:r0 + rows, lanes] = acc + bdw_ref[:, lanes]

    yn = _silu(_layer_norm(y_ref[...], cg_ref[...], cb_ref[...]))
    out = jnp.dot(yn.astype(BF16).reshape(bb * tt, D), w2_ref[...],
                  preferred_element_type=F32) + b2_ref[...]
    gate = mod_ref[:, 2:3, :]
    v = alpha * x_ref[...] + gate * out.reshape(bb, tt, D)
    o_ref[...] = _layer_norm(v, lng_ref[...], lnb_ref[...])
    u_ref[...] = u_next


def _conv(x, state, mod, mod_row0, w1, b1, w_dw, b_dw, cg, cb, w2, b2, lng, lnb, bb, tt, alpha):
    B, T, D = x.shape
    mb = mod_row0 // bb
    nT = T // tt
    wdw = jnp.pad(w_dw, ((0, HALO - CONV_WIDTH), (0, 0)))
    row = lambda a: a.reshape(1, -1)
    return pl.pallas_call(
        functools.partial(_conv_kernel, alpha=alpha),
        grid=(B // bb, nT),
        in_specs=[pl.BlockSpec((bb, tt, D), lambda b_, i: (b_, i, 0)),
                  pl.BlockSpec((bb, tt, D), lambda b_, i: (b_, jnp.minimum(i + 1, nT - 1), 0)),
                  pl.BlockSpec((bb, HALO, D), lambda b_, i: (b_, 0, 0)),
                  pl.BlockSpec((bb, 6, D), lambda b_, i: (b_ + mb, 0, 0)),
                  _const_spec((D, 2 * D)), _const_spec((1, 2 * D)),
                  _const_spec((HALO, D)),
                  _const_spec((1, D)), _const_spec((1, D)), _const_spec((1, D)),
                  _const_spec((D, D)),
                  _const_spec((1, D)), _const_spec((1, D)), _const_spec((1, D))],
        out_specs=[pl.BlockSpec((bb, tt, D), lambda b_, i: (b_, i, 0)),
                   pl.BlockSpec((bb, HALO, D), lambda b_, i: (b_, 0, 0))],
        out_shape=[jax.ShapeDtypeStruct((B, T, D), F32), jax.ShapeDtypeStruct((B, HALO, D), F32)],
        scratch_shapes=[pltpu.VMEM((bb, HALO + tt, D), F32),
                        pltpu.VMEM((SUBLANES, HALO + tt, CONV_LANES), F32),
                        pltpu.VMEM((bb, tt, D), F32), pltpu.VMEM((bb, tt, D), F32)],
        compiler_params=_params(2),
        name="conv",
    )(x, x, state, mod, w1, row(b1), wdw, row(b_dw), row(cg), row(cb), w2, row(b2), row(lng), row(lnb))


def _ff_chunks(F, width):
    return [(f0, min(width, F - f0)) for f0 in range(0, F, width)]


def _swiglu(h, wg_ref, wu_ref, wd_ref, lead):
    F = wg_ref.shape[-1]
    acc = None
    for f0, fw in _ff_chunks(F, 1024):
        g = jnp.dot(h, wg_ref[lead + (slice(None), slice(f0, f0 + fw))], preferred_element_type=F32)
        u = jnp.dot(h, wu_ref[lead + (slice(None), slice(f0, f0 + fw))], preferred_element_type=F32)
        a = (_silu(g) * u).astype(BF16)
        d = jnp.dot(a, wd_ref[lead + (slice(f0, f0 + fw), slice(None))], preferred_element_type=F32)
        acc = d if acc is None else acc + d
    return acc


def _swiglu_packed(h, wgu_ref, wd_ref, lead):
    F = wd_ref.shape[-2]
    gu = jnp.dot(h, wgu_ref[lead], preferred_element_type=F32)
    a = (_silu(gu[:, :F]) * gu[:, F:]).astype(BF16)
    return jnp.dot(a, wd_ref[lead], preferred_element_type=F32)


def _ffn_kernel(x_ref, mod_ref, wg_ref, wu_ref, wd_ref, lng_ref, lnb_ref, o_ref, *, alpha):
    bb, tt, D = x_ref.shape
    shift = mod_ref[:, 3:4, :]
    scale = mod_ref[:, 4:5, :]
    gate = mod_ref[:, 5:6, :]
    x = x_ref[...]
    h = (x * (1 + scale) + shift).astype(BF16).reshape(bb * tt, D)
    f = _swiglu(h, wg_ref, wu_ref, wd_ref, (0,))
    v = alpha * x + gate * f.reshape(bb, tt, D)
    o_ref[...] = _layer_norm(v, lng_ref[...], lnb_ref[...])


def _layer_spec(shape, layer):
    nd = len(shape)
    return pl.BlockSpec((1,) + shape, lambda *_: (layer,) + (0,) * nd, pipeline_mode=pl.Buffered(1))


def _ffn(x, mod, mod_row0, wg, wu, wd, layer, lng, lnb, bb, tt, alpha):
    B, T, D = x.shape
    F = wg.shape[-1]
    mb = mod_row0 // bb
    return pl.pallas_call(
        functools.partial(_ffn_kernel, alpha=alpha),
        grid=(B // bb, T // tt),
        in_specs=[pl.BlockSpec((bb, tt, D), lambda b_, i: (b_, i, 0)),
                  pl.BlockSpec((bb, 6, D), lambda b_, i: (b_ + mb, 0, 0)),
                  _layer_spec((D, F), layer), _layer_spec((D, F), layer), _layer_spec((F, D), layer),
                  _const_spec((1, D)), _const_spec((1, D))],
        out_specs=pl.BlockSpec((bb, tt, D), lambda b_, i: (b_, i, 0)),
        out_shape=jax.ShapeDtypeStruct((B, T, D), F32),
        compiler_params=_params(2),
        name="ffn",
    )(x, mod, wg, wu, wd, lng.reshape(1, D), lnb.reshape(1, D))


def _router_top2(h32, wr_ref, br_ref, n_exp):
    M, D = h32.shape
    rows = []
    for e in range(n_exp):
        prod = h32 * wr_ref[e:e + 1, :]
        part = prod[:, 0:LANES]
        for c in range(1, D // LANES):
            part = part + prod[:, c * LANES:(c + 1) * LANES]
        rows.append(jnp.sum(part.T, axis=0, keepdims=True))
    logits = jnp.concatenate(rows, axis=0) + br_ref[...]
    ex = jnp.exp(logits - jnp.max(logits, axis=0, keepdims=True))
    p = ex / jnp.sum(ex, axis=0, keepdims=True)
    eid = lax.broadcasted_iota(jnp.int32, p.shape, 0)

    def best(v):
        top = jnp.max(v, axis=0, keepdims=True)
        return top, jnp.min(jnp.where(v == top, eid, n_exp), axis=0, keepdims=True)

    p1, i1 = best(p)
    p2, i2 = best(jnp.where(eid == i1, -1.0, p))
    tot = p1 + p2
    return i1, i2, p1 / tot, p2 / tot


def _to_columns(rows, M):
    pad = jnp.zeros((LANES - len(rows), M), F32)
    return jnp.concatenate(list(rows) + [pad], axis=0).T


def _router_gates(h32, wr_ref, br_ref, n_exp):
    i1, i2, g1, g2 = _router_top2(h32, wr_ref, br_ref, n_exp)
    rows = [jnp.where(i1 == e, g1, 0.0) + jnp.where(i2 == e, g2, 0.0) for e in range(n_exp)]
    return _to_columns(rows, h32.shape[0])


def _moe_kernel(x_ref, mod_ref, wr_ref, br_ref, wgu_ref, wd_ref, lng_ref, lnb_ref, o_ref,
                h_ref, gate_ref, acc_ref, *, alpha, n_exp):
    bb, tt, D = x_ref.shape
    e = pl.program_id(2)

    @pl.when(e == 0)
    def _():
        shift = mod_ref[:, 3:4, :]
        scale = mod_ref[:, 4:5, :]
        h32 = (x_ref[...] * (1 + scale) + shift).reshape(bb * tt, D)
        h_ref[...] = h32.astype(BF16)
        gates = _router_gates(h32, wr_ref, br_ref, n_exp)
        for k in range(n_exp):
            gate_ref[k] = gates[:, k:k + 1]
        acc_ref[...] = jnp.zeros_like(acc_ref)

    f = _swiglu_packed(h_ref[...], wgu_ref, wd_ref, (0, 0))
    acc_ref[...] += gate_ref[e] * f

    @pl.when(e == n_exp - 1)
    def _():
        v = alpha * x_ref[...] + mod_ref[:, 5:6, :] * acc_ref[...].reshape(bb, tt, D)
        o_ref[...] = _layer_norm(v, lng_ref[...], lnb_ref[...])


def _moe(x, mod, mod_row0, wr, br, wgu, wd, layer, lng, lnb, bb, tt, alpha):
    B, T, D = x.shape
    _, E, Fe, _ = wd.shape
    mb = mod_row0 // bb
    M = bb * tt
    return pl.pallas_call(
        functools.partial(_moe_kernel, alpha=alpha, n_exp=E),
        grid=(B // bb, T // tt, E),
        in_specs=[pl.BlockSpec((bb, tt, D), lambda b_, i, e: (b_, i, 0)),
                  pl.BlockSpec((bb, 6, D), lambda b_, i, e: (b_ + mb, 0, 0)),
                  _const_spec((E, D)), _const_spec((E, 1)),
                  pl.BlockSpec((1, 1, D, 2 * Fe), lambda b_, i, e: (layer, e, 0, 0)),
                  pl.BlockSpec((1, 1, Fe, D), lambda b_, i, e: (layer, e, 0, 0)),
                  _const_spec((1, D)), _const_spec((1, D))],
        out_specs=pl.BlockSpec((bb, tt, D), lambda b_, i, e: (b_, i, 0)),
        out_shape=jax.ShapeDtypeStruct((B, T, D), F32),
        scratch_shapes=[pltpu.VMEM((M, D), BF16), pltpu.VMEM((E, M, 1), F32),
                        pltpu.VMEM((M, D), F32)],
        compiler_params=_params(3),
        name="moe",
    )(x, mod, wr.T, br.reshape(E, 1), wgu, wd, lng.reshape(1, D), lnb.reshape(1, D))


ROUTE_ROWS = 512
EXPERT_ROWS = 512
META_I1, META_I2, META_G1, META_G2, META_R1, META_R2 = range(6)


def _route_kernel(x_ref, mod_ref, wr_ref, br_ref, meta_ref, cnt_ref, *, n_exp):
    _, tt, D = x_ref.shape
    first = jnp.logical_and(pl.program_id(0) == 0, pl.program_id(1) == 0)

    @pl.when(first)
    def _():
        cnt_ref[...] = jnp.zeros_like(cnt_ref)

    h32 = x_ref[0] * (1 + mod_ref[0, 4:5, :]) + mod_ref[0, 3:4, :]
    i1, i2, g1, g2 = _router_top2(h32, wr_ref, br_ref, n_exp)

    eid = lax.broadcasted_iota(jnp.int32, (n_exp, tt), 0)
    o1 = (eid == i1).astype(F32)
    o2 = (eid == i2).astype(F32)
    upper = (lax.broadcasted_iota(jnp.int32, (tt, tt), 0) <
             lax.broadcasted_iota(jnp.int32, (tt, tt), 1)).astype(BF16)
    c1 = jnp.dot(o1.astype(BF16), upper, preferred_element_type=F32)
    c2 = jnp.dot(o2.astype(BF16), upper, preferred_element_type=F32)
    base = cnt_ref[:, 0:1]
    tot1 = jnp.sum(o1, axis=1, keepdims=True)
    tot2 = jnp.sum(o2, axis=1, keepdims=True)
    r1 = jnp.sum(o1 * (c1 + base), axis=0, keepdims=True)
    r2 = jnp.sum(o2 * (c2 + (base + tot1)), axis=0, keepdims=True)
    cnt_ref[...] = jnp.broadcast_to(base + tot1 + tot2, cnt_ref.shape)

    rows = [i1.astype(F32), i2.astype(F32), g1, g2, r1, r2]
    meta_ref[0] = jnp.concatenate(rows + [jnp.zeros((SUBLANES - len(rows), tt), F32)], axis=0)


def _route(x, mod, mod_row0, wr, br, tt):
    B, T, D = x.shape
    E = wr.shape[1]
    nT = T // tt
    return pl.pallas_call(
        functools.partial(_route_kernel, n_exp=E),
        grid=(B, nT),
        in_specs=[pl.BlockSpec((1, tt, D), lambda b_, i: (b_, i, 0)),
                  pl.BlockSpec((1, 6, D), lambda b_, i: (b_ + mod_row0, 0, 0)),
                  _const_spec((E, D)), _const_spec((E, 1))],
        out_specs=[pl.BlockSpec((1, SUBLANES, tt), lambda b_, i: (b_ * nT + i, 0, 0)),
                   pl.BlockSpec((E, LANES), lambda b_, i: (0, 0))],
        out_shape=[jax.ShapeDtypeStruct((B * nT, SUBLANES, tt), F32),
                   jax.ShapeDtypeStruct((E, LANES), F32)],
        compiler_params=_params(2),
        name="route",
    )(x, mod, wr.T, br.reshape(E, 1))


def _row_copy(src_ref, src_row, dst_ref, dst_row, sem):
    return pltpu.make_async_copy(src_ref.at[pl.ds(src_row, 1)], dst_ref.at[pl.ds(dst_row, 1)], sem)


def _dispatch_kernel(pad_ref, pos_ref, x_ref, mod_ref, xs_ref, rows_ref, zero_ref, sems, *, n_exp):
    _, tb, D = x_ref.shape
    tm = zero_ref.shape[0]
    i = pl.program_id(0) * pl.num_programs(1) + pl.program_id(1)
    last = pl.num_programs(0) * pl.num_programs(1) - 1
    slot = i % 2
    pad_sem = sems.at[2]

    def tail_copy(t):
        return pltpu.make_async_copy(zero_ref, xs_ref.at[pl.ds(pl.multiple_of(t * tm, tm), tm)], pad_sem)

    @pl.when(i == 0)
    def _():
        zero_ref[...] = jnp.zeros_like(zero_ref)
        for e in range(n_exp):
            def fill(r, c):
                _row_copy(zero_ref, 0, xs_ref, r, pad_sem).start()
                return c
            lax.fori_loop(pad_ref[0, e], pad_ref[1, e], fill, 0)

        def fill_tail(t, c):
            tail_copy(t).start()
            return c
        lax.fori_loop(pad_ref[0, n_exp], pad_ref[1, n_exp], fill_tail, 0)

    h32 = x_ref[0] * (1 + mod_ref[0, 4:5, :]) + mod_ref[0, 3:4, :]
    rows_ref[slot] = h32.reshape(tb, D // LANES, LANES)

    def issue(t, c):
        _row_copy(rows_ref.at[slot], t, xs_ref, pos_ref[0, 0, t], sems.at[slot]).start()
        _row_copy(rows_ref.at[slot], t, xs_ref, pos_ref[0, 0, tb + t], sems.at[slot]).start()
        return c
    lax.fori_loop(0, tb, issue, 0, unroll=8)

    def wait_step(s):
        for _ in range(2):
            pltpu.make_async_copy(rows_ref.at[s], xs_ref.at[pl.ds(0, tb)], sems.at[s]).wait()

    @pl.when(i > 0)
    def _():
        wait_step(1 - slot)

    @pl.when(i == last)
    def _():
        wait_step(slot)
        for e in range(n_exp):
            def drain(r, c):
                _row_copy(zero_ref, 0, xs_ref, r, pad_sem).wait()
                return c
            lax.fori_loop(pad_ref[0, e], pad_ref[1, e], drain, 0)

        def drain_tail(t, c):
            tail_copy(t).wait()
            return c
        lax.fori_loop(pad_ref[0, n_exp], pad_ref[1, n_exp], drain_tail, 0)


def _dispatch(pads, pos, x, mod, mod_row0, n_rows, n_exp, tm):
    B, T, D = x.shape
    tb = pos.shape[2] // 2
    nT = T // tb
    S, L = D // LANES, LANES
    return pl.pallas_call(
        functools.partial(_dispatch_kernel, n_exp=n_exp),
        grid_spec=pltpu.PrefetchScalarGridSpec(
            num_scalar_prefetch=1,
            grid=(B, nT),
            in_specs=[pl.BlockSpec((1, 1, 2 * tb), lambda b_, i, pads_: (b_ * nT + i, 0, 0),
                                   memory_space=pltpu.SMEM),
                      pl.BlockSpec((1, tb, D), lambda b_, i, pads_: (b_, i, 0)),
                      pl.BlockSpec((1, 6, D), lambda b_, i, pads_: (b_ + mod_row0, 0, 0))],
            out_specs=pl.BlockSpec(memory_space=pl.ANY),
            scratch_shapes=[pltpu.VMEM((2, tb, S, L), F32), pltpu.VMEM((tm, S, L), F32),
                            pltpu.SemaphoreType.DMA((3,))]),
        out_shape=jax.ShapeDtypeStruct((n_rows, S, L), F32),
        compiler_params=_params(2),
        name="dispatch",
    )(pads, pos, x, mod)


def _experts_kernel(te_ref, nu_ref, xs_ref, wgu_ref, wd_ref, y_ref):
    tm, S, L = xs_ref.shape
    i = pl.program_id(0)

    @pl.when(i < nu_ref[0])
    def _():
        h = xs_ref[...].reshape(tm, S * L).astype(BF16)
        y_ref[...] = _swiglu_packed(h, wgu_ref, wd_ref, (0, 0)).reshape(tm, S, L)

    @pl.when(i >= nu_ref[0])
    def _():
        y_ref[...] = jnp.zeros_like(y_ref)


def _experts(tile_expert, n_used, xs, wgu, wd, layer, tm):
    n_rows, S, L = xs.shape
    _, E, Fe, D = wd.shape
    used = lambda i, te, nu: jnp.minimum(i, nu[0] - 1)
    return pl.pallas_call(
        _experts_kernel,
        grid_spec=pltpu.PrefetchScalarGridSpec(
            num_scalar_prefetch=2,
            grid=(n_rows // tm,),
            in_specs=[pl.BlockSpec((tm, S, L), lambda i, te, nu: (used(i, te, nu), 0, 0)),
                      pl.BlockSpec((1, 1, D, 2 * Fe), lambda i, te, nu: (layer, te[used(i, te, nu)], 0, 0)),
                      pl.BlockSpec((1, 1, Fe, D), lambda i, te, nu: (layer, te[used(i, te, nu)], 0, 0))],
            out_specs=pl.BlockSpec((tm, S, L), lambda i, te, nu: (i, 0, 0))),
        out_shape=jax.ShapeDtypeStruct((n_rows, S, L), F32),
        compiler_params=_params(1),
        name="experts",
    )(tile_expert, n_used, xs, wgu, wd)


def _combine_kernel(*refs, alpha, with_kv):
    if with_kv:
        (pcur_ref, pnext_ref, y_ref, x_ref, mod_ref, meta_ref, lng_ref, lnb_ref, modkv_ref, wk_ref, wv_ref,
         o_ref, kb_ref, vb_ref, kf_ref, vf_ref, buf_ref, sems) = refs
    else:
        (pcur_ref, pnext_ref, y_ref, x_ref, mod_ref, meta_ref, lng_ref, lnb_ref,
         o_ref, buf_ref, sems) = refs
    _, tc, D = x_ref.shape
    step = pl.program_id(0) * pl.num_programs(1) + pl.program_id(1)
    n_steps = pl.num_programs(0) * pl.num_programs(1)
    slot = step % 2

    def start_rows(p_ref, s, t):
        _row_copy(y_ref, p_ref[0, 0, t], buf_ref.at[s], t, sems.at[s]).start()
        _row_copy(y_ref, p_ref[0, 0, tc + t], buf_ref.at[s], tc + t, sems.at[s]).start()

    def issue(p_ref, s):
        def body(t, c):
            start_rows(p_ref, s, t)
            return c
        lax.fori_loop(0, tc, body, 0, unroll=8)

    def wait(s):
        pltpu.make_async_copy(y_ref.at[pl.ds(0, 2 * tc)], buf_ref.at[s], sems.at[s]).wait()

    @pl.when(step == 0)
    def _():
        issue(pcur_ref, 0)

    if with_kv:
        wait(slot)
        for t in range(tc):
            start_rows(pnext_ref, 1 - slot, t)
    else:
        @pl.when(step + 1 < n_steps)
        def _():
            issue(pnext_ref, 1 - slot)
        wait(slot)

    y1 = buf_ref[slot, 0:tc].reshape(tc, D)
    y2 = buf_ref[slot, tc:2 * tc].reshape(tc, D)
    gates = _to_columns([meta_ref[0, META_G1:META_G1 + 1, :], meta_ref[0, META_G2:META_G2 + 1, :]], tc)
    f = gates[:, 0:1] * y1 + gates[:, 1:2] * y2
    v = alpha * x_ref[0] + mod_ref[0, 5:6, :] * f
    x_new = _layer_norm(v, lng_ref[...], lnb_ref[...])
    o_ref[0] = x_new

    if with_kv:
        h = (x_new * (1 + modkv_ref[0, 1:2, :]) + modkv_ref[0, 0:1, :]).astype(BF16)
        k = jnp.dot(h, wk_ref[...], preferred_element_type=F32)
        vv = jnp.dot(h, wv_ref[...], preferred_element_type=F32)
        kb_ref[0] = k.astype(BF16)
        vb_ref[0] = vv.astype(BF16)

        @pl.when(pl.program_id(1) == pl.num_programs(1) - 1)
        def _():
            kf_ref[0] = k
            vf_ref[0] = vv

        @pl.when(step == n_steps - 1)
        def _():
            wait(1 - slot)


def _combine(pos, y, x, mod, mod_row0, meta, lng, lnb, alpha, kv=None):
    B, T, D = x.shape
    nb, _, tc2 = pos.shape
    tc = tc2 // 2
    nT = T // tc
    S, L = y.shape[1:]
    blk = pl.BlockSpec((1, tc, D), lambda b_, i: (b_, i, 0))
    last = pl.BlockSpec((1, tc, D), lambda b_, i: (b_, 0, 0))
    in_specs = [pl.BlockSpec((1, 1, tc2), lambda b_, i: (b_ * nT + i, 0, 0), memory_space=pltpu.SMEM),
                pl.BlockSpec((1, 1, tc2), lambda b_, i: (jnp.minimum(b_ * nT + i + 1, nb - 1), 0, 0),
                             memory_space=pltpu.SMEM),
                pl.BlockSpec(memory_space=pl.ANY),
                blk,
                pl.BlockSpec((1, 6, D), lambda b_, i: (b_ + mod_row0, 0, 0)),
                pl.BlockSpec((1, SUBLANES, tc), lambda b_, i: (b_ * nT + i, 0, 0)),
                _const_spec((1, D)), _const_spec((1, D))]
    args = [pos, pos, y, x, mod, meta, lng.reshape(1, D), lnb.reshape(1, D)]
    out_specs, out_shape = blk, jax.ShapeDtypeStruct((B, T, D), F32)
    if kv is not None:
        in_specs += [pl.BlockSpec((1, 2, D), lambda b_, i: (b_ + mod_row0, 0, 0)),
                     _const_spec((D, D)), _const_spec((D, D))]
        args += list(kv)
        out_specs = [blk, blk, blk, last, last]
        out_shape = [out_shape, jax.ShapeDtypeStruct((B, T, D), BF16), jax.ShapeDtypeStruct((B, T, D), BF16),
                     jax.ShapeDtypeStruct((B, tc, D), F32), jax.ShapeDtypeStruct((B, tc, D), F32)]
    return pl.pallas_call(
        functools.partial(_combine_kernel, alpha=alpha, with_kv=kv is not None),
        grid=(B, nT),
        in_specs=in_specs,
        out_specs=out_specs,
        out_shape=out_shape,
        scratch_shapes=[pltpu.VMEM((2, tc2, S, L), F32), pltpu.SemaphoreType.DMA((2,))],
        compiler_params=_params(2),
        name="combine",
    )(*args)


def _moe_sparse(x, mod, mod_row0, wr, br, wgu, wd, layer, lng, lnb, alpha, kv=None):
    B, T, D = x.shape
    E = wd.shape[1]
    N = B * T
    tb, tm = ROUTE_ROWS, EXPERT_ROWS
    assert T % tb == 0 and E == SUBLANES
    meta, cnt = _route(x, mod, mod_row0, wr, br, tb)

    counts = cnt[:, 0].astype(jnp.int32)
    padded = (counts + tm - 1) // tm * tm
    ends = jnp.cumsum(padded)
    offs = ends - padded
    field = lambda k: meta[:, k, :].astype(jnp.int32)
    dest = lambda ids, ranks: jnp.sum(jnp.where(ids[..., None] == jnp.arange(E), offs, 0), -1) + ranks
    pos1 = dest(field(META_I1), field(META_R1))
    pos2 = dest(field(META_I2), field(META_R2))
    pos = jnp.concatenate([pos1, pos2], axis=1).reshape(N // tb, 1, 2 * tb)
    n_rows = 2 * N + E * tm
    n_tiles = n_rows // tm
    n_used = (ends[-1:] // tm).astype(jnp.int32)
    tile_expert = jnp.sum(jnp.arange(n_tiles)[:, None] * tm >= ends[None, :], axis=1)
    tile_expert = jnp.minimum(tile_expert, E - 1).astype(jnp.int32)
    pads = jnp.stack([jnp.concatenate([offs + counts, n_used]),
                      jnp.concatenate([ends, jnp.full((1,), n_tiles)])]).astype(jnp.int32)

    xs = _dispatch(pads, pos, x, mod, mod_row0, n_rows, E, tm)
    y = _experts(tile_expert, n_used, xs, wgu, wd, layer, tm)
    return _combine(pos, y, x, mod, mod_row0, meta, lng, lnb, alpha, kv)


def _kv_kernel(x_ref, mod_ref, wk_ref, wv_ref, kb_ref, vb_ref, kf_ref, vf_ref):
    bb, tt, D = x_ref.shape
    i = pl.program_id(1)
    shift = mod_ref[:, 0:1, :]
    scale = mod_ref[:, 1:2, :]
    h = (x_ref[...] * (1 + scale) + shift).astype(BF16).reshape(bb * tt, D)
    k = jnp.dot(h, wk_ref[...], preferred_element_type=F32).reshape(bb, tt, D)
    v = jnp.dot(h, wv_ref[...], preferred_element_type=F32).reshape(bb, tt, D)
    kb_ref[...] = k.astype(BF16)
    vb_ref[...] = v.astype(BF16)

    @pl.when(i == pl.num_programs(1) - 1)
    def _():
        kf_ref[...] = k
        vf_ref[...] = v


def _kv(x, mod, mod_row0, wk, wv, bb, tt):
    B, T, D = x.shape
    mb = mod_row0 // bb
    blk = pl.BlockSpec((bb, tt, D), lambda b_, i: (b_, i, 0))
    last = pl.BlockSpec((bb, tt, D), lambda b_, i: (b_, 0, 0))
    return pl.pallas_call(
        _kv_kernel,
        grid=(B // bb, T // tt),
        in_specs=[blk,
                  pl.BlockSpec((bb, 2, D), lambda b_, i: (b_ + mb, 0, 0)),
                  _const_spec((D, D)), _const_spec((D, D))],
        out_specs=[blk, blk, last, last],
        out_shape=[jax.ShapeDtypeStruct((B, T, D), BF16), jax.ShapeDtypeStruct((B, T, D), BF16),
                   jax.ShapeDtypeStruct((B, tt, D), F32), jax.ShapeDtypeStruct((B, tt, D), F32)],
        compiler_params=_params(2),
        name="kv",
    )(x, mod, wk, wv)


ATTN_ROWS = 256


def _attn_kernel(x_ref, mod_ref, kp_ref, kc_ref, vp_ref, vc_ref, rel_ref, wq_ref, wo_ref,
                 lng_ref, lnb_ref, o_ref, bias_ref, kcat_ref, vcat_ref, q_ref, a_ref,
                 *, alpha, chunk, past, pad, mask_first, q_scale):
    _, tt, D = x_ref.shape
    n_chunks = tt // chunk
    win = pad + past + chunk
    head_dim = LANES // 2
    i = pl.program_id(1)

    @pl.when(jnp.logical_and(pl.program_id(0) == 0, i == 0))
    def _():
        key = lax.broadcasted_iota(jnp.int32, (chunk, win), 1) - (pad + past)
        start = lax.broadcasted_iota(jnp.int32, (chunk, win), 0) // CHUNK * CHUNK
        visible = jnp.logical_and(key >= start - past, key < start + CHUNK)
        for hd in range(rel_ref.shape[0]):
            vec = jnp.broadcast_to(rel_ref[hd:hd + 1, :], (chunk, rel_ref.shape[1]))
            table = pltpu.roll(vec, 0, 1, stride=1, stride_axis=0)[:, :win]
            bias_ref[hd // 2, hd % 2 * chunk:(hd % 2 + 1) * chunk, :] = jnp.where(visible, table, NEG_INF)

    if pad:
        kcat_ref[0:pad, :] = jnp.zeros((pad, D), BF16)
        vcat_ref[0:pad, :] = jnp.zeros((pad, D), BF16)
    kcat_ref[pad:pad + past, :] = kp_ref[0]
    kcat_ref[pad + past:, :] = kc_ref[0]
    vcat_ref[pad:pad + past, :] = vp_ref[0]
    vcat_ref[pad + past:, :] = vc_ref[0]

    x = x_ref[0]
    h = (x * (1 + mod_ref[0, 1:2, :]) + mod_ref[0, 0:1, :]).astype(BF16)
    q = jnp.dot(h, wq_ref[...], preferred_element_type=F32) * q_scale
    q_ref[...] = q.astype(BF16)

    row_head = lax.broadcasted_iota(jnp.int32, (2 * chunk, LANES), 0) >= chunk
    lane_head = lax.broadcasted_iota(jnp.int32, (2 * chunk, LANES), 1) >= head_dim
    own = row_head == lane_head
    out_lane_head0 = lax.broadcasted_iota(jnp.int32, (chunk, LANES), 1) < head_dim
    key_col = lax.broadcasted_iota(jnp.int32, (1, win), 1)

    def chunk_loop(masked):
        def one_chunk(g, carry):
            r0 = pl.multiple_of(g * chunk, chunk)
            for pr in range(D // LANES):
                lanes = slice(pr * LANES, (pr + 1) * LANES)
                qp = q_ref[pl.ds(r0, chunk), lanes]
                qbd = jnp.where(own, jnp.concatenate([qp, qp], axis=0), jnp.zeros((), BF16))
                kw = kcat_ref[pl.ds(r0, win), lanes]
                s = lax.dot_general(qbd, kw, (((1,), (1,)), ((), ())), preferred_element_type=F32)
                s = s + bias_ref[pr]
                if masked:
                    s = jnp.where(key_col + r0 >= pad + past, s, NEG_INF)
                m = jnp.max(s, axis=-1, keepdims=True)
                ex = jnp.exp(s - m)
                l = jnp.sum(ex, axis=-1, keepdims=True)
                vw = vcat_ref[pl.ds(r0, win), lanes]
                r = jnp.dot(ex.astype(BF16), vw, preferred_element_type=F32) * (1.0 / l)
                a_ref[pl.ds(r0, chunk), lanes] = jnp.where(out_lane_head0, r[:chunk], r[chunk:]).astype(BF16)
            return carry
        lax.fori_loop(0, n_chunks, one_chunk, 0)

    if mask_first:
        @pl.when(i == 0)
        def _():
            chunk_loop(True)

        @pl.when(i > 0)
        def _():
            chunk_loop(False)
    else:
        chunk_loop(False)

    out = jnp.dot(a_ref[...], wo_ref[...], preferred_element_type=F32)
    v = alpha * x + mod_ref[0, 2:3, :] * out
    o_ref[0] = _layer_norm(v, lng_ref[...], lnb_ref[...])


def _rel_vectors(table, past, chunk, pad):
    win = pad + past + chunk
    width = -(-(win + chunk - 1) // LANES) * LANES
    m = jnp.arange(width)
    m = jnp.where(m < win, m, m - width)
    d = past + pad - m
    return table[:, jnp.clip(d, -REL_CLIP, REL_CLIP) + REL_CLIP]


def _attn(x, mod, mod_row0, k_prev, k_cur, v_prev, v_cur, table, wq, wo, lng, lnb,
          tt, chunk, mask_first, alpha):
    B, T, D = x.shape
    past = WINDOW
    assert past % tt == 0 or tt % past == 0
    assert tt == past or T == tt
    pad = -(past + chunk) % LANES
    rel = _rel_vectors(table, past, chunk, pad)
    npairs = D // LANES
    head_dim = D // N_HEADS
    blk = pl.BlockSpec((1, tt, D), lambda b_, i: (b_, i, 0))
    prev = pl.BlockSpec((1, past, D), lambda b_, i: (b_, jnp.maximum(i - 1, 0), 0))
    return pl.pallas_call(
        functools.partial(_attn_kernel, alpha=alpha, chunk=chunk, past=past, pad=pad,
                          mask_first=mask_first, q_scale=head_dim ** -0.5),
        grid=(B, T // tt),
        in_specs=[blk,
                  pl.BlockSpec((1, 6, D), lambda b_, i: (b_ + mod_row0, 0, 0)),
                  prev, blk, prev, blk,
                  _const_spec(rel.shape),
                  _const_spec((D, D)), _const_spec((D, D)),
                  _const_spec((1, D)), _const_spec((1, D))],
        out_specs=blk,
        out_shape=jax.ShapeDtypeStruct((B, T, D), F32),
        scratch_shapes=[pltpu.VMEM((npairs, 2 * chunk, pad + past + chunk), F32),
                        pltpu.VMEM((pad + past + tt, D), BF16), pltpu.VMEM((pad + past + tt, D), BF16),
                        pltpu.VMEM((tt, D), BF16), pltpu.VMEM((tt, D), BF16)],
        compiler_params=_params(2),
        name="attn",
    )(x, mod, k_prev, k_cur, v_prev, v_cur, rel, wq, wo, lng.reshape(1, D), lnb.reshape(1, D))


def _trunk(x, mods, mods_kv, mod_row0, conv_state, kv_cache, p, bb, tt, chunk):
    B, T, D = x.shape
    depth = p['w_ada'].shape[0]
    n_a = p['w_pw1'].shape[0]
    alpha = (2.0 * depth) ** 0.25
    assert T >= CONV_WIDTH - 1 and T % tt == 0 and B % bb == 0 and mod_row0 % bb == 0
    new_conv = []
    kb = vb = kf = vf = None
    for l in range(depth):
        mod = mods[l]
        if l < n_a:
            x, tail = _conv(x, conv_state[l], mod, mod_row0, p['w_pw1'][l], p['b_pw1'][l],
                            p['w_dw'][l], p['b_dw'][l], p['conv_ln_g'][l], p['conv_ln_b'][l],
                            p['w_pw2'][l], p['b_pw2'][l], p['ln_g'][l, 0], p['ln_b'][l, 0],
                            bb, tt, alpha)
            new_conv.append(tail[:, HALO - (CONV_WIDTH - 1):])
        else:
            j = l - n_a
            if kv_cache is None:
                x = _attn(x, mod, mod_row0, kb, kb, vb, vb, p['rel_bias'][j], p['w_q'][j], p['w_o'][j],
                          p['ln_g'][l, 0], p['ln_b'][l, 0], WINDOW, chunk, True, alpha)
            else:
                x = _attn(x, mod, mod_row0, kv_cache[0], kb, kv_cache[1], vb, p['rel_bias'][j],
                          p['w_q'][j], p['w_o'][j], p['ln_g'][l, 0], p['ln_b'][l, 0],
                          T, chunk, False, alpha)
        i = l // 2
        if l % 2 == 0:
            x = _ffn(x, mod, mod_row0, p['w_ff_gate'], p['w_ff_up'], p['w_ff_down'], i,
                     p['ln_g'][l, 1], p['ln_b'][l, 1], bb, tt, alpha)
        elif bb == 1 and T % ROUTE_ROWS == 0:
            kv = (mods_kv, p['w_k'], p['w_v']) if l == n_a - 1 and ROUTE_ROWS == WINDOW else None
            x = _moe_sparse(x, mod, mod_row0, p['w_router'][i], p['b_router'][i], p['w_e_gu'],
                            p['w_e_down'], i, p['ln_g'][l, 1], p['ln_b'][l, 1], alpha, kv)
            if kv is not None:
                x, kb, vb, kf, vf = x
        else:
            x = _moe(x, mod, mod_row0, p['w_router'][i], p['b_router'][i], p['w_e_gu'],
                     p['w_e_down'], i, p['ln_g'][l, 1], p['ln_b'][l, 1],
                     bb, tt, alpha)
        if l == n_a - 1 and kb is None:
            kb, vb, kf, vf = _kv(x, mods_kv, mod_row0, p['w_k'], p['w_v'], bb, tt)
    return x, jnp.stack(new_conv), kf, vf


def kernel(x_prompt, x_sample, c_prompt, c_sample, state_conv, cache_k, cache_v, w_ada, b_ada, ln_g, ln_b, w_pw1, b_pw1, w_dw, b_dw, conv_ln_g, conv_ln_b, w_pw2, b_pw2, w_ada_kv, b_ada_kv, w_k, w_v, w_q, w_o, rel_bias, w_ff_gate, w_ff_up, w_ff_down, w_router, b_router, w_e_gate, w_e_up, w_e_down):
    B, T, D = x_prompt.shape
    Bs, Ts, _ = x_sample.shape
    H = N_HEADS
    assert Bs + B <= MOD_ROWS and D == H * (LANES // 2)
    depth = w_ada.shape[0]
    n_a = w_pw1.shape[0]

    bf = lambda w: w.astype(BF16)
    p = dict(w_ada=w_ada, ln_g=ln_g, ln_b=ln_b, w_pw1=bf(w_pw1), b_pw1=b_pw1, w_dw=w_dw, b_dw=b_dw,
             conv_ln_g=conv_ln_g, conv_ln_b=conv_ln_b, w_pw2=bf(w_pw2), b_pw2=b_pw2,
             w_k=bf(w_k), w_v=bf(w_v), w_q=bf(w_q), w_o=bf(w_o), rel_bias=rel_bias,
             w_ff_gate=bf(w_ff_gate), w_ff_up=bf(w_ff_up), w_ff_down=bf(w_ff_down),
             w_router=w_router, b_router=b_router,
             w_e_gu=jnp.concatenate([bf(w_e_gate), bf(w_e_up)], axis=-1), w_e_down=bf(w_e_down))

    c_all = jnp.concatenate([c_sample, c_prompt, jnp.zeros((MOD_ROWS - Bs - B, D), F32)], axis=0)
    mods = _ada(c_all, w_ada, b_ada, 1536).reshape(depth, MOD_ROWS, 6, D)
    mods_kv = _ada(c_all, w_ada_kv[None], b_ada_kv[None], 1024).reshape(MOD_ROWS, 2, D)

    pad_state = lambda s: jnp.pad(s, ((0, 0), (0, 0), (HALO - (CONV_WIDTH - 1), 0), (0, 0)))
    zero_state = jnp.zeros((n_a, B, HALO, D), F32)
    y_p, conv_p, kf_p, vf_p = _trunk(x_prompt, mods, mods_kv, Bs, zero_state, None, p, 1, WINDOW,
                                     ATTN_ROWS)

    cache = (cache_k.reshape(Bs, -1, D).astype(BF16), cache_v.reshape(Bs, -1, D).astype(BF16))
    y_s, conv_s, kf_s, vf_s = _trunk(x_sample, mods, mods_kv, 0, pad_state(state_conv), cache, p,
                                     Bs, Ts, Ts)

    L = cache_k.shape[1]
    heads = lambda a: a.reshape(a.shape[0], a.shape[1], H, D // H)
    k_s = jnp.concatenate([cache_k, heads(kf_s)], axis=1)[:, -L:]
    v_s = jnp.concatenate([cache_v, heads(vf_s)], axis=1)[:, -L:]
    return (y_p, y_s, conv_p, heads(kf_p), heads(vf_p), conv_s, k_s, v_s)
```

```python
import functools

import jax
import jax.numpy as jnp
from jax import lax
from jax.experimental import pallas as pl
from jax.experimental.pallas import tpu as pltpu

CHUNK = 64
WINDOW = 512
REL_CLIP = 128
CONV_WIDTH = 31
N_HEADS = 16
LN_EPS = 1e-5
NEG_INF = -1e30

LANES = 128
HALO = 32
MOD_ROWS = 16
VMEM_LIMIT = 56 * 1024 * 1024

F32 = jnp.float32
BF16 = jnp.bfloat16


def _layer_norm(v, g, b):
    mu = jnp.mean(v, axis=-1, keepdims=True)
    d = v - mu
    var = jnp.mean(d * d, axis=-1, keepdims=True)
    return d * lax.rsqrt(var + LN_EPS) * g + b


def _sigmoid(v):
    return 0.5 * jnp.tanh(0.5 * v) + 0.5


def _silu(v):
    return v * _sigmoid(v)


def _const_spec(shape):
    nd = len(shape)
    return pl.BlockSpec(shape, lambda *_: (0,) * nd, pipeline_mode=pl.Buffered(1))


def _params(n_grid):
    return pltpu.CompilerParams(dimension_semantics=("arbitrary",) * n_grid,
                                vmem_limit_bytes=VMEM_LIMIT)


def _ada_kernel(c_ref, w_ref, b_ref, o_ref):
    a = _silu(c_ref[...]).astype(BF16)
    o_ref[0] = jnp.dot(a, w_ref[0].astype(BF16), preferred_element_type=F32) + b_ref[0]


def _ada(c, w, b, tn):
    L, D, N = w.shape
    return pl.pallas_call(
        _ada_kernel,
        grid=(L, N // tn),
        in_specs=[pl.BlockSpec((MOD_ROWS, D), lambda l, j: (0, 0)),
                  pl.BlockSpec((1, D, tn), lambda l, j: (l, 0, j)),
                  pl.BlockSpec((1, 1, tn), lambda l, j: (l, 0, j))],
        out_specs=pl.BlockSpec((1, MOD_ROWS, tn), lambda l, j: (l, 0, j)),
        out_shape=jax.ShapeDtypeStruct((L, MOD_ROWS, N), F32),
        compiler_params=_params(2),
        name="ada",
    )(c, w, b.reshape(L, 1, N))


CONV_ROWS = 64
CONV_LANES = 256
SUBLANES = 8


def _conv_kernel(x_ref, xn_ref, st_ref, mod_ref, w1_ref, b1_ref, wdw_ref, bdw_ref, cg_ref, cb_ref,
                 w2_ref, b2_ref, lng_ref, lnb_ref, o_ref, tail_ref, ext_ref, sh_ref, y_ref, u_ref,
                 *, alpha):
    bb, tt, D = x_ref.shape
    i = pl.program_id(1)

    def glu(xv):
        h = (xv * (1 + mod_ref[:, 1:2, :]) + mod_ref[:, 0:1, :]).astype(BF16).reshape(bb * tt, D)
        z = jnp.dot(h, w1_ref[...], preferred_element_type=F32) + b1_ref[...]
        return (z[:, :D] * _sigmoid(z[:, D:])).reshape(bb, tt, D)

    @pl.when(i == 0)
    def _():
        u_ref[...] = glu(x_ref[...])

    u = u_ref[...]

    @pl.when(i == 0)
    def _():
        ext_ref[:, 0:HALO, :] = st_ref[...]

    @pl.when(i > 0)
    def _():
        ext_ref[:, 0:HALO, :] = ext_ref[:, tt:tt + HALO, :]

    ext_ref[:, HALO:, :] = u

    @pl.when(i == pl.num_programs(1) - 1)
    def _():
        tail_ref[...] = u[:, tt - HALO:, :]

    u_next = glu(xn_ref[...])

    first = HALO - (CONV_WIDTH - 1)
    rows = min(CONV_ROWS, tt)
    for b in range(bb):
        for lb in range(D // CONV_LANES):
            lanes = slice(lb * CONV_LANES, (lb + 1) * CONV_LANES)
            sh_ref[0] = ext_ref[b, :, lanes]
            for s in range(1, SUBLANES):
                sh_ref[s, 0:tt + HALO - SUBLANES, :] = ext_ref[b, s:s + tt + HALO - SUBLANES, lanes]

            for r0 in range(0, tt, rows):
                acc = jnp.zeros((rows, CONV_LANES), F32)
                for j in range(CONV_WIDTH):
                    a, s = divmod(j + first, SUBLANES)
                    acc = acc + wdw_ref[j:j + 1, lanes] * sh_ref[s, r0 + SUBLANES * a:r0 + SUBLANES * a + rows, :]
                y_ref[b, r0:r0 + rows, lanes] = acc + bdw_ref[:, lanes]

    yn = _silu(_layer_norm(y_ref[...], cg_ref[...], cb_ref[...]))
    out = jnp.dot(yn.astype(BF16).reshape(bb * tt, D), w2_ref[...],
                  preferred_element_type=F32) + b2_ref[...]
    gate = mod_ref[:, 2:3, :]
    v = alpha * x_ref[...] + gate * out.reshape(bb, tt, D)
    o_ref[...] = _layer_norm(v, lng_ref[...], lnb_ref[...])
    u_ref[...] = u_next


def _conv(x, state, mod, mod_row0, w1, b1, w_dw, b_dw, cg, cb, w2, b2, lng, lnb, bb, tt, alpha):
    B, T, D = x.shape
    mb = mod_row0 // bb
    nT = T // tt
    wdw = jnp.pad(w_dw, ((0, HALO - CONV_WIDTH), (0, 0)))
    row = lambda a: a.reshape(1, -1)
    return pl.pallas_call(
        functools.partial(_conv_kernel, alpha=alpha),
        grid=(B // bb, nT),
        in_specs=[pl.BlockSpec((bb, tt, D), lambda b_, i: (b_, i, 0)),
                  pl.BlockSpec((bb, tt, D), lambda b_, i: (b_, jnp.minimum(i + 1, nT - 1), 0)),
                  pl.BlockSpec((bb, HALO, D), lambda b_, i: (b_, 0, 0)),
                  pl.BlockSpec((bb, 6, D), lambda b_, i: (b_ + mb, 0, 0)),
                  _const_spec((D, 2 * D)), _const_spec((1, 2 * D)),
                  _const_spec((HALO, D)),
                  _const_spec((1, D)), _const_spec((1, D)), _const_spec((1, D)),
                  _const_spec((D, D)),
                  _const_spec((1, D)), _const_spec((1, D)), _const_spec((1, D))],
        out_specs=[pl.BlockSpec((bb, tt, D), lambda b_, i: (b_, i, 0)),
                   pl.BlockSpec((bb, HALO, D), lambda b_, i: (b_, 0, 0))],
        out_shape=[jax.ShapeDtypeStruct((B, T, D), F32), jax.ShapeDtypeStruct((B, HALO, D), F32)],
        scratch_shapes=[pltpu.VMEM((bb, HALO + tt, D), F32),
                        pltpu.VMEM((SUBLANES, HALO + tt, CONV_LANES), F32),
                        pltpu.VMEM((bb, tt, D), F32), pltpu.VMEM((bb, tt, D), F32)],
        compiler_params=_params(2),
        name="conv",
    )(x, x, state, mod, w1, row(b1), wdw, row(b_dw), row(cg), row(cb), w2, row(b2), row(lng), row(lnb))


def _ff_chunks(F, width):
    return [(f0, min(width, F - f0)) for f0 in range(0, F, width)]


def _swiglu(h, wg_ref, wu_ref, wd_ref, lead):
    F = wg_ref.shape[-1]
    acc = None
    for f0, fw in _ff_chunks(F, 1024):
        g = jnp.dot(h, wg_ref[lead + (slice(None), slice(f0, f0 + fw))], preferred_element_type=F32)
        u = jnp.dot(h, wu_ref[lead + (slice(None), slice(f0, f0 + fw))], preferred_element_type=F32)
        a = (_silu(g) * u).astype(BF16)
        d = jnp.dot(a, wd_ref[lead + (slice(f0, f0 + fw), slice(None))], preferred_element_type=F32)
        acc = d if acc is None else acc + d
    return acc


def _swiglu_packed(h, wgu_ref, wd_ref, lead):
    F = wd_ref.shape[-2]
    gu = jnp.dot(h, wgu_ref[lead], preferred_element_type=F32)
    a = (_silu(gu[:, :F]) * gu[:, F:]).astype(BF16)
    return jnp.dot(a, wd_ref[lead], preferred_element_type=F32)


PACK_ROWS = 512


def _pack_kernel(g_ref, u_ref, o_ref):
    F = g_ref.shape[-1]
    o_ref[0, 0, :, :F] = g_ref[0, 0].astype(BF16)
    o_ref[0, 0, :, F:] = u_ref[0, 0].astype(BF16)


def _pack_gate_up(wg, wu):
    L, E, D, F = wg.shape
    src = pl.BlockSpec((1, 1, PACK_ROWS, F), lambda l, e, r: (l, e, r, 0))
    return pl.pallas_call(
        _pack_kernel,
        grid=(L, E, D // PACK_ROWS),
        in_specs=[src, src],
        out_specs=pl.BlockSpec((1, 1, PACK_ROWS, 2 * F), lambda l, e, r: (l, e, r, 0)),
        out_shape=jax.ShapeDtypeStruct((L, E, D, 2 * F), BF16),
        compiler_params=_params(3),
        name="pack",
    )(wg, wu)


def _ffn_kernel(x_ref, mod_ref, wg_ref, wu_ref, wd_ref, lng_ref, lnb_ref, o_ref, *, alpha):
    bb, tt, D = x_ref.shape
    shift = mod_ref[:, 3:4, :]
    scale = mod_ref[:, 4:5, :]
    gate = mod_ref[:, 5:6, :]
    x = x_ref[...]
    h = (x * (1 + scale) + shift).astype(BF16).reshape(bb * tt, D)
    f = _swiglu(h, wg_ref, wu_ref, wd_ref, (0,))
    v = alpha * x + gate * f.reshape(bb, tt, D)
    o_ref[...] = _layer_norm(v, lng_ref[...], lnb_ref[...])


def _layer_spec(shape, layer):
    nd = len(shape)
    return pl.BlockSpec((1,) + shape, lambda *_: (layer,) + (0,) * nd, pipeline_mode=pl.Buffered(1))


def _ffn(x, mod, mod_row0, wg, wu, wd, layer, lng, lnb, bb, tt, alpha):
    B, T, D = x.shape
    F = wg.shape[-1]
    mb = mod_row0 // bb
    return pl.pallas_call(
        functools.partial(_ffn_kernel, alpha=alpha),
        grid=(B // bb, T // tt),
        in_specs=[pl.BlockSpec((bb, tt, D), lambda b_, i: (b_, i, 0)),
                  pl.BlockSpec((bb, 6, D), lambda b_, i: (b_ + mb, 0, 0)),
                  _layer_spec((D, F), layer), _layer_spec((D, F), layer), _layer_spec((F, D), layer),
                  _const_spec((1, D)), _const_spec((1, D))],
        out_specs=pl.BlockSpec((bb, tt, D), lambda b_, i: (b_, i, 0)),
        out_shape=jax.ShapeDtypeStruct((B, T, D), F32),
        compiler_params=_params(2),
        name="ffn",
    )(x, mod, wg, wu, wd, lng.reshape(1, D), lnb.reshape(1, D))


def _router_top2(h32, wr_ref, br_ref, n_exp):
    M, D = h32.shape
    rows = []
    for e in range(n_exp):
        prod = h32 * wr_ref[e:e + 1, :]
        part = prod[:, 0:LANES]
        for c in range(1, D // LANES):
            part = part + prod[:, c * LANES:(c + 1) * LANES]
        rows.append(jnp.sum(part.T, axis=0, keepdims=True))
    logits = jnp.concatenate(rows, axis=0) + br_ref[...]
    ex = jnp.exp(logits - jnp.max(logits, axis=0, keepdims=True))
    p = ex / jnp.sum(ex, axis=0, keepdims=True)
    eid = lax.broadcasted_iota(jnp.int32, p.shape, 0)

    def best(v):
        top = jnp.max(v, axis=0, keepdims=True)
        return top, jnp.min(jnp.where(v == top, eid, n_exp), axis=0, keepdims=True)

    p1, i1 = best(p)
    p2, i2 = best(jnp.where(eid == i1, -1.0, p))
    tot = p1 + p2
    return i1, i2, p1 / tot, p2 / tot


def _to_columns(rows, M):
    pad = jnp.zeros((LANES - len(rows), M), F32)
    return jnp.concatenate(list(rows) + [pad], axis=0).T


def _router_gates(h32, wr_ref, br_ref, n_exp):
    i1, i2, g1, g2 = _router_top2(h32, wr_ref, br_ref, n_exp)
    rows = [jnp.where(i1 == e, g1, 0.0) + jnp.where(i2 == e, g2, 0.0) for e in range(n_exp)]
    return _to_columns(rows, h32.shape[0])


def _moe_kernel(x_ref, mod_ref, wr_ref, br_ref, wgu_ref, wd_ref, lng_ref, lnb_ref, o_ref,
                h_ref, gate_ref, acc_ref, *, alpha, n_exp):
    bb, tt, D = x_ref.shape
    e = pl.program_id(2)

    @pl.when(e == 0)
    def _():
        shift = mod_ref[:, 3:4, :]
        scale = mod_ref[:, 4:5, :]
        h32 = (x_ref[...] * (1 + scale) + shift).reshape(bb * tt, D)
        h_ref[...] = h32.astype(BF16)
        gates = _router_gates(h32, wr_ref, br_ref, n_exp)
        for k in range(n_exp):
            gate_ref[k] = gates[:, k:k + 1]
        acc_ref[...] = jnp.zeros_like(acc_ref)

    f = _swiglu_packed(h_ref[...], wgu_ref, wd_ref, (0, 0))
    acc_ref[...] += gate_ref[e] * f

    @pl.when(e == n_exp - 1)
    def _():
        v = alpha * x_ref[...] + mod_ref[:, 5:6, :] * acc_ref[...].reshape(bb, tt, D)
        o_ref[...] = _layer_norm(v, lng_ref[...], lnb_ref[...])


def _moe(x, mod, mod_row0, wr, br, wgu, wd, layer, lng, lnb, bb, tt, alpha):
    B, T, D = x.shape
    _, E, Fe, _ = wd.shape
    mb = mod_row0 // bb
    M = bb * tt
    return pl.pallas_call(
        functools.partial(_moe_kernel, alpha=alpha, n_exp=E),
        grid=(B // bb, T // tt, E),
        in_specs=[pl.BlockSpec((bb, tt, D), lambda b_, i, e: (b_, i, 0)),
                  pl.BlockSpec((bb, 6, D), lambda b_, i, e: (b_ + mb, 0, 0)),
                  _const_spec((E, D)), _const_spec((E, 1)),
                  pl.BlockSpec((1, 1, D, 2 * Fe), lambda b_, i, e: (layer, e, 0, 0)),
                  pl.BlockSpec((1, 1, Fe, D), lambda b_, i, e: (layer, e, 0, 0)),
                  _const_spec((1, D)), _const_spec((1, D))],
        out_specs=pl.BlockSpec((bb, tt, D), lambda b_, i, e: (b_, i, 0)),
        out_shape=jax.ShapeDtypeStruct((B, T, D), F32),
        scratch_shapes=[pltpu.VMEM((M, D), BF16), pltpu.VMEM((E, M, 1), F32),
                        pltpu.VMEM((M, D), F32)],
        compiler_params=_params(3),
        name="moe",
    )(x, mod, wr.T, br.reshape(E, 1), wgu, wd, lng.reshape(1, D), lnb.reshape(1, D))


ROUTE_ROWS = 512
EXPERT_ROWS = 512
META_I1, META_I2, META_G1, META_G2, META_R1, META_R2 = range(6)


def _route_kernel(x_ref, mod_ref, wr_ref, br_ref, meta_ref, cnt_ref, *, n_exp):
    _, tt, D = x_ref.shape
    first = jnp.logical_and(pl.program_id(0) == 0, pl.program_id(1) == 0)

    @pl.when(first)
    def _():
        cnt_ref[...] = jnp.zeros_like(cnt_ref)

    h32 = x_ref[0] * (1 + mod_ref[0, 4:5, :]) + mod_ref[0, 3:4, :]
    i1, i2, g1, g2 = _router_top2(h32, wr_ref, br_ref, n_exp)

    eid = lax.broadcasted_iota(jnp.int32, (n_exp, tt), 0)
    o1 = (eid == i1).astype(F32)
    o2 = (eid == i2).astype(F32)
    upper = (lax.broadcasted_iota(jnp.int32, (tt, tt), 0) <
             lax.broadcasted_iota(jnp.int32, (tt, tt), 1)).astype(BF16)
    c1 = jnp.dot(o1.astype(BF16), upper, preferred_element_type=F32)
    c2 = jnp.dot(o2.astype(BF16), upper, preferred_element_type=F32)
    base = cnt_ref[:, 0:1]
    tot1 = jnp.sum(o1, axis=1, keepdims=True)
    tot2 = jnp.sum(o2, axis=1, keepdims=True)
    r1 = jnp.sum(o1 * (c1 + base), axis=0, keepdims=True)
    r2 = jnp.sum(o2 * (c2 + (base + tot1)), axis=0, keepdims=True)
    cnt_ref[...] = jnp.broadcast_to(base + tot1 + tot2, cnt_ref.shape)

    rows = [i1.astype(F32), i2.astype(F32), g1, g2, r1, r2]
    meta_ref[0] = jnp.concatenate(rows + [jnp.zeros((SUBLANES - len(rows), tt), F32)], axis=0)


def _route(x, mod, mod_row0, wr, br, tt):
    B, T, D = x.shape
    E = wr.shape[1]
    nT = T // tt
    return pl.pallas_call(
        functools.partial(_route_kernel, n_exp=E),
        grid=(B, nT),
        in_specs=[pl.BlockSpec((1, tt, D), lambda b_, i: (b_, i, 0)),
                  pl.BlockSpec((1, 6, D), lambda b_, i: (b_ + mod_row0, 0, 0)),
                  _const_spec((E, D)), _const_spec((E, 1))],
        out_specs=[pl.BlockSpec((1, SUBLANES, tt), lambda b_, i: (b_ * nT + i, 0, 0)),
                   pl.BlockSpec((E, LANES), lambda b_, i: (0, 0))],
        out_shape=[jax.ShapeDtypeStruct((B * nT, SUBLANES, tt), F32),
                   jax.ShapeDtypeStruct((E, LANES), F32)],
        compiler_params=_params(2),
        name="route",
    )(x, mod, wr.T, br.reshape(E, 1))


def _row_copy(src_ref, src_row, dst_ref, dst_row, sem):
    return pltpu.make_async_copy(src_ref.at[pl.ds(src_row, 1)], dst_ref.at[pl.ds(dst_row, 1)], sem)


def _dispatch_kernel(pad_ref, pos_ref, x_ref, mod_ref, xs_ref, rows_ref, zero_ref, sems, *, n_exp):
    _, tb, D = x_ref.shape
    tm = zero_ref.shape[0]
    i = pl.program_id(0) * pl.num_programs(1) + pl.program_id(1)
    last = pl.num_programs(0) * pl.num_programs(1) - 1
    slot = i % 2
    pad_sem = sems.at[2]

    def tail_copy(t):
        return pltpu.make_async_copy(zero_ref, xs_ref.at[pl.ds(pl.multiple_of(t * tm, tm), tm)], pad_sem)

    @pl.when(i == 0)
    def _():
        zero_ref[...] = jnp.zeros_like(zero_ref)
        for e in range(n_exp):
            def fill(r, c):
                _row_copy(zero_ref, 0, xs_ref, r, pad_sem).start()
                return c
            lax.fori_loop(pad_ref[0, e], pad_ref[1, e], fill, 0)

        def fill_tail(t, c):
            tail_copy(t).start()
            return c
        lax.fori_loop(pad_ref[0, n_exp], pad_ref[1, n_exp], fill_tail, 0)

    h32 = x_ref[0] * (1 + mod_ref[0, 4:5, :]) + mod_ref[0, 3:4, :]
    rows_ref[slot] = h32.reshape(tb, D // LANES, LANES)

    def issue(t, c):
        _row_copy(rows_ref.at[slot], t, xs_ref, pos_ref[0, 0, t], sems.at[slot]).start()
        _row_copy(rows_ref.at[slot], t, xs_ref, pos_ref[0, 0, tb + t], sems.at[slot]).start()
        return c
    lax.fori_loop(0, tb, issue, 0, unroll=8)

    def wait_step(s):
        for _ in range(2):
            pltpu.make_async_copy(rows_ref.at[s], xs_ref.at[pl.ds(0, tb)], sems.at[s]).wait()

    @pl.when(i > 0)
    def _():
        wait_step(1 - slot)

    @pl.when(i == last)
    def _():
        wait_step(slot)
        for e in range(n_exp):
            def drain(r, c):
                _row_copy(zero_ref, 0, xs_ref, r, pad_sem).wait()
                return c
            lax.fori_loop(pad_ref[0, e], pad_ref[1, e], drain, 0)

        def drain_tail(t, c):
            tail_copy(t).wait()
            return c
        lax.fori_loop(pad_ref[0, n_exp], pad_ref[1, n_exp], drain_tail, 0)


def _dispatch(pads, pos, x, mod, mod_row0, n_rows, n_exp, tm):
    B, T, D = x.shape
    tb = pos.shape[2] // 2
    nT = T // tb
    S, L = D // LANES, LANES
    return pl.pallas_call(
        functools.partial(_dispatch_kernel, n_exp=n_exp),
        grid_spec=pltpu.PrefetchScalarGridSpec(
            num_scalar_prefetch=1,
            grid=(B, nT),
            in_specs=[pl.BlockSpec((1, 1, 2 * tb), lambda b_, i, pads_: (b_ * nT + i, 0, 0),
                                   memory_space=pltpu.SMEM),
                      pl.BlockSpec((1, tb, D), lambda b_, i, pads_: (b_, i, 0)),
                      pl.BlockSpec((1, 6, D), lambda b_, i, pads_: (b_ + mod_row0, 0, 0))],
            out_specs=pl.BlockSpec(memory_space=pl.ANY),
            scratch_shapes=[pltpu.VMEM((2, tb, S, L), F32), pltpu.VMEM((tm, S, L), F32),
                            pltpu.SemaphoreType.DMA((3,))]),
        out_shape=jax.ShapeDtypeStruct((n_rows, S, L), F32),
        compiler_params=_params(2),
        name="dispatch",
    )(pads, pos, x, mod)


def _experts_kernel(te_ref, nu_ref, xs_ref, wgu_ref, wd_ref, y_ref):
    tm, S, L = xs_ref.shape
    i = pl.program_id(0)

    @pl.when(i < nu_ref[0])
    def _():
        h = xs_ref[...].reshape(tm, S * L).astype(BF16)
        y_ref[...] = _swiglu_packed(h, wgu_ref, wd_ref, (0, 0)).reshape(tm, S, L)

    @pl.when(i >= nu_ref[0])
    def _():
        y_ref[...] = jnp.zeros_like(y_ref)


def _experts(tile_expert, n_used, xs, wgu, wd, layer, tm):
    n_rows, S, L = xs.shape
    _, E, Fe, D = wd.shape
    used = lambda i, te, nu: jnp.minimum(i, nu[0] - 1)
    return pl.pallas_call(
        _experts_kernel,
        grid_spec=pltpu.PrefetchScalarGridSpec(
            num_scalar_prefetch=2,
            grid=(n_rows // tm,),
            in_specs=[pl.BlockSpec((tm, S, L), lambda i, te, nu: (used(i, te, nu), 0, 0)),
                      pl.BlockSpec((1, 1, D, 2 * Fe), lambda i, te, nu: (layer, te[used(i, te, nu)], 0, 0)),
                      pl.BlockSpec((1, 1, Fe, D), lambda i, te, nu: (layer, te[used(i, te, nu)], 0, 0))],
            out_specs=pl.BlockSpec((tm, S, L), lambda i, te, nu: (i, 0, 0))),
        out_shape=jax.ShapeDtypeStruct((n_rows, S, L), F32),
        compiler_params=_params(1),
        name="experts",
    )(tile_expert, n_used, xs, wgu, wd)


def _combine_kernel(*refs, alpha, with_kv):
    if with_kv:
        (pcur_ref, pnext_ref, y_ref, x_ref, mod_ref, meta_ref, lng_ref, lnb_ref, modkv_ref, wk_ref, wv_ref,
         o_ref, kb_ref, vb_ref, kf_ref, vf_ref, buf_ref, sems) = refs
    else:
        (pcur_ref, pnext_ref, y_ref, x_ref, mod_ref, meta_ref, lng_ref, lnb_ref,
         o_ref, buf_ref, sems) = refs
    _, tc, D = x_ref.shape
    step = pl.program_id(0) * pl.num_programs(1) + pl.program_id(1)
    n_steps = pl.num_programs(0) * pl.num_programs(1)
    slot = step % 2

    def start_rows(p_ref, s, t):
        _row_copy(y_ref, p_ref[0, 0, t], buf_ref.at[s], t, sems.at[s]).start()
        _row_copy(y_ref, p_ref[0, 0, tc + t], buf_ref.at[s], tc + t, sems.at[s]).start()

    def issue(p_ref, s):
        def body(t, c):
            start_rows(p_ref, s, t)
            return c
        lax.fori_loop(0, tc, body, 0, unroll=8)

    def wait(s):
        pltpu.make_async_copy(y_ref.at[pl.ds(0, 2 * tc)], buf_ref.at[s], sems.at[s]).wait()

    @pl.when(step == 0)
    def _():
        issue(pcur_ref, 0)

    if with_kv:
        wait(slot)
        for t in range(tc):
            start_rows(pnext_ref, 1 - slot, t)
    else:
        @pl.when(step + 1 < n_steps)
        def _():
            issue(pnext_ref, 1 - slot)
        wait(slot)

    y1 = buf_ref[slot, 0:tc].reshape(tc, D)
    y2 = buf_ref[slot, tc:2 * tc].reshape(tc, D)
    gates = _to_columns([meta_ref[0, META_G1:META_G1 + 1, :], meta_ref[0, META_G2:META_G2 + 1, :]], tc)
    f = gates[:, 0:1] * y1 + gates[:, 1:2] * y2
    v = alpha * x_ref[0] + mod_ref[0, 5:6, :] * f
    x_new = _layer_norm(v, lng_ref[...], lnb_ref[...])
    o_ref[0] = x_new

    if with_kv:
        h = (x_new * (1 + modkv_ref[0, 1:2, :]) + modkv_ref[0, 0:1, :]).astype(BF16)
        k = jnp.dot(h, wk_ref[...], preferred_element_type=F32)
        vv = jnp.dot(h, wv_ref[...], preferred_element_type=F32)
        kb_ref[0] = k.astype(BF16)
        vb_ref[0] = vv.astype(BF16)

        @pl.when(pl.program_id(1) == pl.num_programs(1) - 1)
        def _():
            kf_ref[0] = k
            vf_ref[0] = vv

        @pl.when(step == n_steps - 1)
        def _():
            wait(1 - slot)


def _combine(pos, y, x, mod, mod_row0, meta, lng, lnb, alpha, kv=None):
    B, T, D = x.shape
    nb, _, tc2 = pos.shape
    tc = tc2 // 2
    nT = T // tc
    S, L = y.shape[1:]
    blk = pl.BlockSpec((1, tc, D), lambda b_, i: (b_, i, 0))
    last = pl.BlockSpec((1, tc, D), lambda b_, i: (b_, 0, 0))
    in_specs = [pl.BlockSpec((1, 1, tc2), lambda b_, i: (b_ * nT + i, 0, 0), memory_space=pltpu.SMEM),
                pl.BlockSpec((1, 1, tc2), lambda b_, i: (jnp.minimum(b_ * nT + i + 1, nb - 1), 0, 0),
                             memory_space=pltpu.SMEM),
                pl.BlockSpec(memory_space=pl.ANY),
                blk,
                pl.BlockSpec((1, 6, D), lambda b_, i: (b_ + mod_row0, 0, 0)),
                pl.BlockSpec((1, SUBLANES, tc), lambda b_, i: (b_ * nT + i, 0, 0)),
                _const_spec((1, D)), _const_spec((1, D))]
    args = [pos, pos, y, x, mod, meta, lng.reshape(1, D), lnb.reshape(1, D)]
    out_specs, out_shape = blk, jax.ShapeDtypeStruct((B, T, D), F32)
    if kv is not None:
        in_specs += [pl.BlockSpec((1, 2, D), lambda b_, i: (b_ + mod_row0, 0, 0)),
                     _const_spec((D, D)), _const_spec((D, D))]
        args += list(kv)
        out_specs = [blk, blk, blk, last, last]
        out_shape = [out_shape, jax.ShapeDtypeStruct((B, T, D), BF16), jax.ShapeDtypeStruct((B, T, D), BF16),
                     jax.ShapeDtypeStruct((B, tc, D), F32), jax.ShapeDtypeStruct((B, tc, D), F32)]
    return pl.pallas_call(
        functools.partial(_combine_kernel, alpha=alpha, with_kv=kv is not None),
        grid=(B, nT),
        in_specs=in_specs,
        out_specs=out_specs,
        out_shape=out_shape,
        scratch_shapes=[pltpu.VMEM((2, tc2, S, L), F32), pltpu.SemaphoreType.DMA((2,))],
        compiler_params=_params(2),
        name="combine",
    )(*args)


def _moe_sparse(x, mod, mod_row0, wr, br, wgu, wd, layer, lng, lnb, alpha, kv=None):
    B, T, D = x.shape
    E = wd.shape[1]
    N = B * T
    tb, tm = ROUTE_ROWS, EXPERT_ROWS
    assert T % tb == 0 and E == SUBLANES
    meta, cnt = _route(x, mod, mod_row0, wr, br, tb)

    counts = cnt[:, 0].astype(jnp.int32)
    padded = (counts + tm - 1) // tm * tm
    ends = jnp.cumsum(padded)
    offs = ends - padded
    field = lambda k: meta[:, k, :].astype(jnp.int32)
    dest = lambda ids, ranks: jnp.sum(jnp.where(ids[..., None] == jnp.arange(E), offs, 0), -1) + ranks
    pos1 = dest(field(META_I1), field(META_R1))
    pos2 = dest(field(META_I2), field(META_R2))
    pos = jnp.concatenate([pos1, pos2], axis=1).reshape(N // tb, 1, 2 * tb)
    n_rows = 2 * N + E * tm
    n_tiles = n_rows // tm
    n_used = (ends[-1:] // tm).astype(jnp.int32)
    tile_expert = jnp.sum(jnp.arange(n_tiles)[:, None] * tm >= ends[None, :], axis=1)
    tile_expert = jnp.minimum(tile_expert, E - 1).astype(jnp.int32)
    pads = jnp.stack([jnp.concatenate([offs + counts, n_used]),
                      jnp.concatenate([ends, jnp.full((1,), n_tiles)])]).astype(jnp.int32)

    xs = _dispatch(pads, pos, x, mod, mod_row0, n_rows, E, tm)
    y = _experts(tile_expert, n_used, xs, wgu, wd, layer, tm)
    return _combine(pos, y, x, mod, mod_row0, meta, lng, lnb, alpha, kv)


def _kv_kernel(x_ref, mod_ref, wk_ref, wv_ref, kb_ref, vb_ref, kf_ref, vf_ref):
    bb, tt, D = x_ref.shape
    i = pl.program_id(1)
    shift = mod_ref[:, 0:1, :]
    scale = mod_ref[:, 1:2, :]
    h = (x_ref[...] * (1 + scale) + shift).astype(BF16).reshape(bb * tt, D)
    k = jnp.dot(h, wk_ref[...], preferred_element_type=F32).reshape(bb, tt, D)
    v = jnp.dot(h, wv_ref[...], preferred_element_type=F32).reshape(bb, tt, D)
    kb_ref[...] = k.astype(BF16)
    vb_ref[...] = v.astype(BF16)

    @pl.when(i == pl.num_programs(1) - 1)
    def _():
        kf_ref[...] = k
        vf_ref[...] = v


def _kv(x, mod, mod_row0, wk, wv, bb, tt):
    B, T, D = x.shape
    mb = mod_row0 // bb
    blk = pl.BlockSpec((bb, tt, D), lambda b_, i: (b_, i, 0))
    last = pl.BlockSpec((bb, tt, D), lambda b_, i: (b_, 0, 0))
    return pl.pallas_call(
        _kv_kernel,
        grid=(B // bb, T // tt),
        in_specs=[blk,
                  pl.BlockSpec((bb, 2, D), lambda b_, i: (b_ + mb, 0, 0)),
                  _const_spec((D, D)), _const_spec((D, D))],
        out_specs=[blk, blk, last, last],
        out_shape=[jax.ShapeDtypeStruct((B, T, D), BF16), jax.ShapeDtypeStruct((B, T, D), BF16),
                   jax.ShapeDtypeStruct((B, tt, D), F32), jax.ShapeDtypeStruct((B, tt, D), F32)],
        compiler_params=_params(2),
        name="kv",
    )(x, mod, wk, wv)


ATTN_ROWS = 256


def _attn_kernel(x_ref, mod_ref, kp_ref, kc_ref, vp_ref, vc_ref, rel_ref, wq_ref, wo_ref,
                 lng_ref, lnb_ref, o_ref, bias_ref, kcat_ref, vcat_ref, q_ref, a_ref,
                 *, alpha, chunk, past, pad, mask_first, q_scale):
    _, tt, D = x_ref.shape
    n_chunks = tt // chunk
    win = pad + past + chunk
    head_dim = LANES // 2
    i = pl.program_id(1)

    @pl.when(jnp.logical_and(pl.program_id(0) == 0, i == 0))
    def _():
        key = lax.broadcasted_iota(jnp.int32, (chunk, win), 1) - (pad + past)
        start = lax.broadcasted_iota(jnp.int32, (chunk, win), 0) // CHUNK * CHUNK
        visible = jnp.logical_and(key >= start - past, key < start + CHUNK)
        for hd in range(rel_ref.shape[0]):
            vec = jnp.broadcast_to(rel_ref[hd:hd + 1, :], (chunk, rel_ref.shape[1]))
            table = pltpu.roll(vec, 0, 1, stride=1, stride_axis=0)[:, :win]
            bias_ref[hd // 2, hd % 2 * chunk:(hd % 2 + 1) * chunk, :] = jnp.where(visible, table, NEG_INF)

    if pad:
        kcat_ref[0:pad, :] = jnp.zeros((pad, D), BF16)
        vcat_ref[0:pad, :] = jnp.zeros((pad, D), BF16)
    kcat_ref[pad:pad + past, :] = kp_ref[0]
    kcat_ref[pad + past:, :] = kc_ref[0]
    vcat_ref[pad:pad + past, :] = vp_ref[0]
    vcat_ref[pad + past:, :] = vc_ref[0]

    x = x_ref[0]
    h = (x * (1 + mod_ref[0, 1:2, :]) + mod_ref[0, 0:1, :]).astype(BF16)
    q = jnp.dot(h, wq_ref[...], preferred_element_type=F32) * q_scale
    q_ref[...] = q.astype(BF16)

    row_head = lax.broadcasted_iota(jnp.int32, (2 * chunk, LANES), 0) >= chunk
    lane_head = lax.broadcasted_iota(jnp.int32, (2 * chunk, LANES), 1) >= head_dim
    own = row_head == lane_head
    out_lane_head0 = lax.broadcasted_iota(jnp.int32, (chunk, LANES), 1) < head_dim
    key_col = lax.broadcasted_iota(jnp.int32, (1, win), 1)

    def chunk_loop(masked):
        def one_chunk(g, carry):
            r0 = pl.multiple_of(g * chunk, chunk)
            for pr in range(D // LANES):
                lanes = slice(pr * LANES, (pr + 1) * LANES)
                qp = q_ref[pl.ds(r0, chunk), lanes]
                qbd = jnp.where(own, jnp.concatenate([qp, qp], axis=0), jnp.zeros((), BF16))
                kw = kcat_ref[pl.ds(r0, win), lanes]
                s = lax.dot_general(qbd, kw, (((1,), (1,)), ((), ())), preferred_element_type=F32)
                s = s + bias_ref[pr]
                if masked:
                    s = jnp.where(key_col + r0 >= pad + past, s, NEG_INF)
                m = jnp.max(s, axis=-1, keepdims=True)
                ex = jnp.exp(s - m)
                l = jnp.sum(ex, axis=-1, keepdims=True)
                vw = vcat_ref[pl.ds(r0, win), lanes]
                r = jnp.dot(ex.astype(BF16), vw, preferred_element_type=F32) * (1.0 / l)
                a_ref[pl.ds(r0, chunk), lanes] = jnp.where(out_lane_head0, r[:chunk], r[chunk:]).astype(BF16)
            return carry
        lax.fori_loop(0, n_chunks, one_chunk, 0)

    if mask_first:
        @pl.when(i == 0)
        def _():
            chunk_loop(True)

        @pl.when(i > 0)
        def _():
            chunk_loop(False)
    else:
        chunk_loop(False)

    out = jnp.dot(a_ref[...], wo_ref[...], preferred_element_type=F32)
    v = alpha * x + mod_ref[0, 2:3, :] * out
    o_ref[0] = _layer_norm(v, lng_ref[...], lnb_ref[...])


def _rel_vectors(table, past, chunk, pad):
    win = pad + past + chunk
    width = -(-(win + chunk - 1) // LANES) * LANES
    m = jnp.arange(width)
    m = jnp.where(m < win, m, m - width)
    d = past + pad - m
    return table[:, jnp.clip(d, -REL_CLIP, REL_CLIP) + REL_CLIP]


def _attn(x, mod, mod_row0, k_prev, k_cur, v_prev, v_cur, table, wq, wo, lng, lnb,
          tt, chunk, mask_first, alpha):
    B, T, D = x.shape
    past = WINDOW
    assert past % tt == 0 or tt % past == 0
    assert tt == past or T == tt
    pad = -(past + chunk) % LANES
    rel = _rel_vectors(table, past, chunk, pad)
    npairs = D // LANES
    head_dim = D // N_HEADS
    blk = pl.BlockSpec((1, tt, D), lambda b_, i: (b_, i, 0))
    prev = pl.BlockSpec((1, past, D), lambda b_, i: (b_, jnp.maximum(i - 1, 0), 0))
    return pl.pallas_call(
        functools.partial(_attn_kernel, alpha=alpha, chunk=chunk, past=past, pad=pad,
                          mask_first=mask_first, q_scale=head_dim ** -0.5),
        grid=(B, T // tt),
        in_specs=[blk,
                  pl.BlockSpec((1, 6, D), lambda b_, i: (b_ + mod_row0, 0, 0)),
                  prev, blk, prev, blk,
                  _const_spec(rel.shape),
                  _const_spec((D, D)), _const_spec((D, D)),
                  _const_spec((1, D)), _const_spec((1, D))],
        out_specs=blk,
        out_shape=jax.ShapeDtypeStruct((B, T, D), F32),
        scratch_shapes=[pltpu.VMEM((npairs, 2 * chunk, pad + past + chunk), F32),
                        pltpu.VMEM((pad + past + tt, D), BF16), pltpu.VMEM((pad + past + tt, D), BF16),
                        pltpu.VMEM((tt, D), BF16), pltpu.VMEM((tt, D), BF16)],
        compiler_params=_params(2),
        name="attn",
    )(x, mod, k_prev, k_cur, v_prev, v_cur, rel, wq, wo, lng.reshape(1, D), lnb.reshape(1, D))


def _trunk(x, mods, mods_kv, mod_row0, conv_state, kv_cache, p, bb, tt, chunk):
    B, T, D = x.shape
    depth = p['w_ada'].shape[0]
    n_a = p['w_pw1'].shape[0]
    alpha = (2.0 * depth) ** 0.25
    assert T >= CONV_WIDTH - 1 and T % tt == 0 and B % bb == 0 and mod_row0 % bb == 0
    new_conv = []
    kb = vb = kf = vf = None
    for l in range(depth):
        mod = mods[l]
        if l < n_a:
            x, tail = _conv(x, conv_state[l], mod, mod_row0, p['w_pw1'][l], p['b_pw1'][l],
                            p['w_dw'][l], p['b_dw'][l], p['conv_ln_g'][l], p['conv_ln_b'][l],
                            p['w_pw2'][l], p['b_pw2'][l], p['ln_g'][l, 0], p['ln_b'][l, 0],
                            bb, tt, alpha)
            new_conv.append(tail[:, HALO - (CONV_WIDTH - 1):])
        else:
            j = l - n_a
            if kv_cache is None:
                x = _attn(x, mod, mod_row0, kb, kb, vb, vb, p['rel_bias'][j], p['w_q'][j], p['w_o'][j],
                          p['ln_g'][l, 0], p['ln_b'][l, 0], WINDOW, chunk, True, alpha)
            else:
                x = _attn(x, mod, mod_row0, kv_cache[0], kb, kv_cache[1], vb, p['rel_bias'][j],
                          p['w_q'][j], p['w_o'][j], p['ln_g'][l, 0], p['ln_b'][l, 0],
                          T, chunk, False, alpha)
        i = l // 2
        if l % 2 == 0:
            x = _ffn(x, mod, mod_row0, p['w_ff_gate'], p['w_ff_up'], p['w_ff_down'], i,
                     p['ln_g'][l, 1], p['ln_b'][l, 1], bb, tt, alpha)
        elif bb == 1 and T % ROUTE_ROWS == 0:
            kv = (mods_kv, p['w_k'], p['w_v']) if l == n_a - 1 and ROUTE_ROWS == WINDOW else None
            x = _moe_sparse(x, mod, mod_row0, p['w_router'][i], p['b_router'][i], p['w_e_gu'],
                            p['w_e_down'], i, p['ln_g'][l, 1], p['ln_b'][l, 1], alpha, kv)
            if kv is not None:
                x, kb, vb, kf, vf = x
        else:
            x = _moe(x, mod, mod_row0, p['w_router'][i], p['b_router'][i], p['w_e_gu'],
                     p['w_e_down'], i, p['ln_g'][l, 1], p['ln_b'][l, 1],
                     bb, tt, alpha)
        if l == n_a - 1 and kb is None:
            kb, vb, kf, vf = _kv(x, mods_kv, mod_row0, p['w_k'], p['w_v'], bb, tt)
    return x, jnp.stack(new_conv), kf, vf


def kernel(x_prompt, x_sample, c_prompt, c_sample, state_conv, cache_k, cache_v, w_ada, b_ada, ln_g, ln_b, w_pw1, b_pw1, w_dw, b_dw, conv_ln_g, conv_ln_b, w_pw2, b_pw2, w_ada_kv, b_ada_kv, w_k, w_v, w_q, w_o, rel_bias, w_ff_gate, w_ff_up, w_ff_down, w_router, b_router, w_e_gate, w_e_up, w_e_down):
    B, T, D = x_prompt.shape
    Bs, Ts, _ = x_sample.shape
    H = N_HEADS
    assert Bs + B <= MOD_ROWS and D == H * (LANES // 2)
    depth = w_ada.shape[0]
    n_a = w_pw1.shape[0]

    bf = lambda w: w.astype(BF16)
    p = dict(w_ada=w_ada, ln_g=ln_g, ln_b=ln_b, w_pw1=bf(w_pw1), b_pw1=b_pw1, w_dw=w_dw, b_dw=b_dw,
             conv_ln_g=conv_ln_g, conv_ln_b=conv_ln_b, w_pw2=bf(w_pw2), b_pw2=b_pw2,
             w_k=bf(w_k), w_v=bf(w_v), w_q=bf(w_q), w_o=bf(w_o), rel_bias=rel_bias,
             w_ff_gate=bf(w_ff_gate), w_ff_up=bf(w_ff_up), w_ff_down=bf(w_ff_down),
             w_router=w_router, b_router=b_router,
             w_e_gu=_pack_gate_up(w_e_gate, w_e_up), w_e_down=bf(w_e_down))

    c_all = jnp.concatenate([c_sample, c_prompt, jnp.zeros((MOD_ROWS - Bs - B, D), F32)], axis=0)
    mods = _ada(c_all, w_ada, b_ada, 1536).reshape(depth, MOD_ROWS, 6, D)
    mods_kv = _ada(c_all, w_ada_kv[None], b_ada_kv[None], 1024).reshape(MOD_ROWS, 2, D)

    pad_state = lambda s: jnp.pad(s, ((0, 0), (0, 0), (HALO - (CONV_WIDTH - 1), 0), (0, 0)))
    zero_state = jnp.zeros((n_a, B, HALO, D), F32)
    y_p, conv_p, kf_p, vf_p = _trunk(x_prompt, mods, mods_kv, Bs, zero_state, None, p, 1, WINDOW,
                                     ATTN_ROWS)

    cache = (cache_k.reshape(Bs, -1, D).astype(BF16), cache_v.reshape(Bs, -1, D).astype(BF16))
    y_s, conv_s, kf_s, vf_s = _trunk(x_sample, mods, mods_kv, 0, pad_state(state_conv), cache, p,
                                     Bs, Ts, Ts)

    L = cache_k.shape[1]
    heads = lambda a: a.reshape(a.shape[0], a.shape[1], H, D // H)
    k_s = jnp.concatenate([cache_k, heads(kf_s)], axis=1)[:, -L:]
    v_s = jnp.concatenate([cache_v, heads(vf_s)], axis=1)[:, -L:]
    return (y_p, y_s, conv_p, heads(kf_p), heads(vf_p), conv_s, k_s, v_s)
```
